```python
import jax, jax.numpy as jnp
from jax import lax
import numpy as np

D_MODEL = 2048
BATCH = 4
SEQ = 2048
DEPTH = 2

CHUNK = 64
Q_BLOCK = 128
MIX_WIDTH = D_MODEL
EPS = 1e-6
MLA_HEADS = 8
MLA_Q_LORA = 512
MLA_KV_LORA = 256
MLA_NOPE = 128
MLA_ROPE = 64
MLA_V = 128
ROPE_BASE = 10000.0
SB_HEADS = 8
SB_DIM = 64
FOX_HEADS = 8
FOX_DIM = 64
IN_SIZES = (MLA_Q_LORA, MLA_KV_LORA, MLA_ROPE, 3 * SB_HEADS * SB_DIM, 3 * FOX_HEADS * FOX_DIM, FOX_HEADS)
IN_WIDTH = MLA_Q_LORA + MLA_KV_LORA + MLA_ROPE + 3 * SB_HEADS * SB_DIM + 3 * FOX_HEADS * FOX_DIM + FOX_HEADS
OUT_SIZES = (MLA_HEADS * MLA_V, SB_HEADS * SB_DIM, FOX_HEADS * FOX_DIM)
D_FF = 4 * D_MODEL

kernel_name = 'hybrid_mla_stickbreak_fox_block'


def rms_norm(x, g):
    xf = x.astype(jnp.float32)
    y = xf * lax.rsqrt(jnp.mean(xf * xf, axis=-1, keepdims=True) + EPS)
    return (y * g.astype(jnp.float32)).astype(x.dtype)


def split_sizes(p, sizes):
    idx, acc = [], 0
    for s in sizes[:-1]:
        acc += s
        idx.append(acc)
    return jnp.split(p, idx, axis=-1)


def rope_tables(seq_len, dim):
    pos = jnp.arange(seq_len, dtype=jnp.float32)
    inv = ROPE_BASE ** (-jnp.arange(0, dim, 2, dtype=jnp.float32) / dim)
    ang = pos[:, None] * inv[None, :]
    return jnp.cos(ang), jnp.sin(ang)


def apply_rope(x, cos, sin):
    x1, x2 = jnp.split(x.astype(jnp.float32), 2, axis=-1)
    return jnp.concatenate([x1 * cos - x2 * sin, x2 * cos + x1 * sin], axis=-1).astype(x.dtype)


def head_rms_norm(o, g):
    B, S, H, d = o.shape
    return rms_norm(o, g.reshape(H, d)).reshape(B, S, H * d)


def mla_attention(q_nope, q_rope, k_nope, k_rope, v):
    S = q_nope.shape[1]
    scale = (MLA_NOPE + MLA_ROPE) ** -0.5
    outs = []
    for i in range(S // Q_BLOCK):
        q0, q1 = i * Q_BLOCK, (i + 1) * Q_BLOCK
        s = (jnp.einsum('bqhd,bkhd->bhqk', q_nope[:, q0:q1], k_nope[:, :q1])
             + jnp.einsum('bqhd,bkd->bhqk', q_rope[:, q0:q1], k_rope[:, :q1])).astype(jnp.float32) * scale
        t_pos = jnp.arange(q0, q1)[:, None]
        s_pos = jnp.arange(q1)[None, :]
        allowed = s_pos < (t_pos // CHUNK + 1) * CHUNK
        p = jax.nn.softmax(jnp.where(allowed, s, -jnp.inf), axis=-1)
        outs.append(jnp.einsum('bhqk,bkhd->bqhd', p.astype(v.dtype), v[:, :q1]))
    return jnp.concatenate(outs, axis=1)


def stick_breaking_attention(q, k, v):
    S, d = q.shape[1], q.shape[3]
    scale = d ** -0.5
    outs = []
    for i in range(S // Q_BLOCK):
        q0, q1 = i * Q_BLOCK, (i + 1) * Q_BLOCK
        z = jnp.einsum('bqhd,bkhd->bhqk', q[:, q0:q1], k[:, :q1]).astype(jnp.float32) * scale
        t_pos = jnp.arange(q0, q1)[:, None]
        s_pos = jnp.arange(q1)[None, :]
        strict = s_pos < t_pos
        log_keep = jnp.where(strict, jax.nn.log_sigmoid(-z), 0.0)
        log_stick = lax.cumsum(log_keep, axis=3, reverse=True) - log_keep
        a = jnp.where(strict, jnp.exp(jax.nn.log_sigmoid(z) + log_stick), 0.0)
        outs.append(jnp.einsum('bhqk,bkhd->bqhd', a.astype(v.dtype), v[:, :q1]))
    return jnp.concatenate(outs, axis=1)


def forgetting_attention(q, k, v, log_f):
    S, d = q.shape[1], q.shape[3]
    scale = d ** -0.5
    F = jnp.swapaxes(jnp.cumsum(log_f, axis=1), 1, 2)
    outs = []
    for i in range(S // Q_BLOCK):
        q0, q1 = i * Q_BLOCK, (i + 1) * Q_BLOCK
        s = jnp.einsum('bqhd,bkhd->bhqk', q[:, q0:q1], k[:, :q1]).astype(jnp.float32) * scale
        s = s + F[:, :, q0:q1, None] - F[:, :, None, :q1]
        t_pos = jnp.arange(q0, q1)[:, None]
        s_pos = jnp.arange(q1)[None, :]
        p = jax.nn.softmax(jnp.where(s_pos <= t_pos, s, -jnp.inf), axis=-1)
        outs.append(jnp.einsum('bhqk,bkhd->bqhd', p.astype(v.dtype), v[:, :q1]))
    return jnp.concatenate(outs, axis=1)


def setup_inputs(seed: int = 0) -> dict:
    key = jax.random.key(seed)
    ks = jax.random.split(key, 20)
    n = jax.random.normal
    f32 = jnp.float32
    L, D = DEPTH, D_MODEL
    return {
        'x': n(ks[0], (BATCH, SEQ, D), f32),
        'c': n(ks[1], (BATCH, D), f32),
        'w_ada': n(ks[2], (L, D, 6 * D), f32) * (0.5 * D ** -0.5),
        'b_ada': n(ks[3], (L, 6 * D), f32) * 0.01,
        'norm_mix': 1.0 + 0.02 * n(ks[4], (L, D), f32),
        'w_in': n(ks[5], (L, D, IN_WIDTH), f32) * D ** -0.5,
        'q_norm': 1.0 + 0.02 * n(ks[6], (L, MLA_Q_LORA), f32),
        'w_uq': n(ks[7], (L, MLA_Q_LORA, MLA_HEADS * (MLA_NOPE + MLA_ROPE)), f32) * MLA_Q_LORA ** -0.5,
        'kv_norm': 1.0 + 0.02 * n(ks[8], (L, MLA_KV_LORA), f32),
        'w_ukv': n(ks[9], (L, MLA_KV_LORA, MLA_HEADS * (MLA_NOPE + MLA_V)), f32) * MLA_KV_LORA ** -0.5,
        'b_forget': 0.1 * n(ks[10], (L, FOX_HEADS), f32),
        'out_norm': 1.0 + 0.02 * n(ks[11], (L, MIX_WIDTH), f32),
        'w_out': n(ks[12], (L, MIX_WIDTH, D), f32) * MIX_WIDTH ** -0.5,
        'norm_ffn': 1.0 + 0.02 * n(ks[13], (L, D), f32),
        'w_ff1': n(ks[14], (L, D, D_FF), f32) * D ** -0.5,
        'w_ff2': n(ks[15], (L, D_FF, D), f32) * D_FF ** -0.5,
        'final_norm': 1.0 + 0.02 * n(ks[16], (D,), f32),
    }


def reference(x, c, w_ada, b_ada, norm_mix, w_in, q_norm, w_uq, kv_norm, w_ukv, b_forget,
              out_norm, w_out, norm_ffn, w_ff1, w_ff2, final_norm):
    B, S, _ = x.shape
    cos, sin = rope_tables(S, MLA_ROPE)
    c_act = jax.nn.silu(c)
    for l in range(DEPTH):
        mod = c_act @ w_ada[l] + b_ada[l]
        sh_a, sc_a, g_a, sh_m, sc_m, g_m = [m[:, None, :] for m in jnp.split(mod, 6, axis=-1)]

        h = rms_norm(x, norm_mix[l]) * (1.0 + sc_a) + sh_a
        p = h @ w_in[l]
        p_cq, p_ckv, p_kr, p_sb, p_fox, p_f = split_sizes(p, IN_SIZES)

        q = (rms_norm(p_cq, q_norm[l]) @ w_uq[l]).reshape(B, S, MLA_HEADS, MLA_NOPE + MLA_ROPE)
        kv = (rms_norm(p_ckv, kv_norm[l]) @ w_ukv[l]).reshape(B, S, MLA_HEADS, MLA_NOPE + MLA_V)
        q_nope = q[..., :MLA_NOPE]
        q_rope = apply_rope(q[..., MLA_NOPE:], cos[None, :, None, :], sin[None, :, None, :])
        k_nope, v_mla = kv[..., :MLA_NOPE], kv[..., MLA_NOPE:]
        k_rope = apply_rope(p_kr, cos[None], sin[None])
        o_mla = mla_attention(q_nope, q_rope, k_nope, k_rope, v_mla)

        q_sb, k_sb, v_sb = [t.reshape(B, S, SB_HEADS, SB_DIM) for t in jnp.split(p_sb, 3, axis=-1)]
        o_sb = stick_breaking_attention(q_sb, k_sb, v_sb)

        q_fx, k_fx, v_fx = [t.reshape(B, S, FOX_HEADS, FOX_DIM) for t in jnp.split(p_fox, 3, axis=-1)]
        log_f = jax.nn.log_sigmoid((p_f + b_forget[l]).astype(jnp.float32))
        o_fx = forgetting_attention(q_fx, k_fx, v_fx, log_f)

        g_mla, g_sb, g_fx = split_sizes(out_norm[l], OUT_SIZES)
        o = jnp.concatenate([head_rms_norm(o_mla, g_mla), head_rms_norm(o_sb, g_sb),
                             head_rms_norm(o_fx, g_fx)], axis=-1)
        x = x + g_a * (o @ w_out[l])

        h = rms_norm(x, norm_ffn[l]) * (1.0 + sc_m) + sh_m
        x = x + g_m * (jnp.square(jax.nn.relu(h @ w_ff1[l])) @ w_ff2[l])
    return rms_norm(x, final_norm)
```

```python
import functools

import numpy as np
import jax
import jax.numpy as jnp
from jax import lax
from jax.experimental import pallas as pl
from jax.experimental.pallas import tpu as pltpu

F32 = jnp.float32
BF16 = jnp.bfloat16

D_MODEL = 2048
N_BATCH = 4
SEQ_LEN = 2048
N_LAYERS = 2
CHUNK_LEN = 64
NORM_EPS = 1e-6
MLA_H = 8
MLA_QL = 512
MLA_KVL = 256
MLA_NOPE_D = 128
MLA_ROPE_D = 64
MLA_V_D = 128
ROPE_THETA = 10000.0
SB_H = 8
SB_D = 64
FOX_H = 8
FOX_D = 64
FF_DIM = 4 * D_MODEL
N_TOK = N_BATCH * SEQ_LEN

LANES = 128
SUBLANES = 8
VMEM_LIMIT = 56 * 1024 * 1024

C_CQ = 0
C_CKV = C_CQ + MLA_QL
C_KR = C_CKV + MLA_KVL
C_F = C_KR + 2 * LANES
C_ATT = C_F + LANES
ATT_W = 3 * SB_H * SB_D + 3 * FOX_H * FOX_D
IN_W = C_ATT + ATT_W
HEAD_GRP = SB_H * SB_D

PROJ_TM = 512
FFN_TM = 512
FFN_TF = 512
ADA_TN = 1024
ATT_TQ = 128
ATT_TK = 256
CUM_T = 256


def _dot(a, b):
    return jnp.dot(a, b, preferred_element_type=F32)


def _dot_nt(a, b):
    return lax.dot_general(a, b, (((1,), (1,)), ((), ())), preferred_element_type=F32)


def _split3(x):
    hi = x.astype(BF16)
    r1 = x - hi.astype(F32)
    mid = r1.astype(BF16)
    lo = (r1 - mid.astype(F32)).astype(BF16)
    return hi, mid, lo


def _softplus(z):
    return jnp.maximum(z, 0.0) + jnp.log1p(jnp.exp(-jnp.abs(z)))


def _rms(x, gain):
    return x * lax.rsqrt(jnp.mean(x * x, axis=-1, keepdims=True) + NORM_EPS) * gain


def _resident(shape):
    nd = len(shape)
    return pl.BlockSpec(shape, lambda *_: (0,) * nd, pipeline_mode=pl.Buffered(1))


def _params(sem):
    return pltpu.CompilerParams(dimension_semantics=sem, vmem_limit_bytes=VMEM_LIMIT)


def _ada_body(c_ref, w_ref, b_ref, o_ref):
    c = c_ref[...]
    c_act = (c * jax.nn.sigmoid(c)).astype(BF16)
    o_ref[0] = _dot(c_act, w_ref[0].astype(BF16)) + b_ref[0]


def _ada_mod(c_pad, w_ada, b_ada):
    n_out = 6 * D_MODEL
    return pl.pallas_call(
        _ada_body,
        grid=(N_LAYERS, n_out // ADA_TN),
        in_specs=[
            pl.BlockSpec((SUBLANES, D_MODEL), lambda l, j: (0, 0)),
            pl.BlockSpec((1, D_MODEL, ADA_TN), lambda l, j: (l, 0, j)),
            pl.BlockSpec((1, 1, ADA_TN), lambda l, j: (l, 0, j)),
        ],
        out_specs=pl.BlockSpec((1, SUBLANES, ADA_TN), lambda l, j: (l, 0, j)),
        out_shape=jax.ShapeDtypeStruct((N_LAYERS, SUBLANES, n_out), F32),
        compiler_params=_params(("arbitrary", "arbitrary")),
        name="ada_mod",
    )(c_pad, w_ada, b_ada.reshape(N_LAYERS, 1, n_out))


def _proj_body(x_ref, mod_ref, gn_ref, win_ref, qn_ref, wuq_ref, kvn_ref, wukv_ref, bf_ref,
               cos_ref, sin_ref, q_out, kv_out, kr_out, att_out, lf_out, h_ref):
    mod = mod_ref[0, 0]
    x = x_ref[...]
    h = _rms(x, gn_ref[...]) * (1.0 + mod[1:2]) + mod[0:1]
    h_ref[...] = h.astype(BF16)

    def proj(c0, width):
        return _dot(h_ref[...], win_ref[:, c0:c0 + width])

    mla_scale = (MLA_NOPE_D + MLA_ROPE_D) ** -0.5
    n_nope = MLA_H * MLA_NOPE_D
    n_rope = MLA_H * MLA_ROPE_D

    cq = _rms(proj(C_CQ, MLA_QL), qn_ref[...]).astype(BF16)
    q = _dot(cq, wuq_ref[...])
    q_out[:, :n_nope] = (q[:, :n_nope] * mla_scale).astype(BF16)
    q_rot = q[:, n_nope:n_nope + n_rope] * cos_ref[...] + q[:, n_nope + n_rope:] * sin_ref[...]
    q_out[:, n_nope:] = (q_rot * mla_scale).astype(BF16)

    ckv = _rms(proj(C_CKV, MLA_KVL), kvn_ref[...]).astype(BF16)
    kv_out[...] = _dot(ckv, wukv_ref[...]).astype(BF16)

    kr = proj(C_KR, 2 * LANES)
    kr_out[...] = (kr[:, :LANES] * cos_ref[:, :LANES] + kr[:, LANES:] * sin_ref[:, :LANES]).astype(BF16)

    lf_out[...] = -_softplus(-(proj(C_F, LANES) + bf_ref[...]))

    sb_scale = SB_D ** -0.5
    fox_scale = FOX_D ** -0.5
    scales = (sb_scale, 1.0, 1.0, fox_scale, 1.0, 1.0)
    for g, sc in enumerate(scales):
        blk = proj(C_ATT + g * HEAD_GRP, HEAD_GRP)
        if sc != 1.0:
            blk = blk * sc
        att_out[:, g * HEAD_GRP:(g + 1) * HEAD_GRP] = blk.astype(BF16)


def _project(x2, mod4, layer, gn, win, qn, wuq, kvn, wukv, bfp, cos_t, sin_t):
    tm = PROJ_TM
    tiles_per_seq = SEQ_LEN // tm
    q_w = MLA_H * (MLA_NOPE_D + MLA_ROPE_D)
    kv_w = MLA_H * (MLA_NOPE_D + MLA_V_D)
    rope_w = MLA_H * MLA_ROPE_D
    row = lambda i: (i, 0)
    return pl.pallas_call(
        _proj_body,
        grid=(N_TOK // tm,),
        in_specs=[
            pl.BlockSpec((tm, D_MODEL), row),
            pl.BlockSpec((1, 1, 6, D_MODEL), lambda i: (layer, i // tiles_per_seq, 0, 0)),
            _resident((1, D_MODEL)),
            _resident((D_MODEL, IN_W)),
            _resident((1, MLA_QL)),
            _resident((MLA_QL, q_w + rope_w)),
            _resident((1, MLA_KVL)),
            _resident((MLA_KVL, kv_w)),
            _resident((1, LANES)),
            pl.BlockSpec((tm, rope_w), lambda i: (i % tiles_per_seq, 0)),
            pl.BlockSpec((tm, rope_w), lambda i: (i % tiles_per_seq, 0)),
        ],
        out_specs=[
            pl.BlockSpec((tm, q_w), row),
            pl.BlockSpec((tm, kv_w), row),
            pl.BlockSpec((tm, LANES), row),
            pl.BlockSpec((tm, ATT_W), row),
            pl.BlockSpec((tm, LANES), row),
        ],
        out_shape=[
            jax.ShapeDtypeStruct((N_TOK, q_w), BF16),
            jax.ShapeDtypeStruct((N_TOK, kv_w), BF16),
            jax.ShapeDtypeStruct((N_TOK, LANES), BF16),
            jax.ShapeDtypeStruct((N_TOK, ATT_W), BF16),
            jax.ShapeDtypeStruct((N_TOK, LANES), F32),
        ],
        scratch_shapes=[pltpu.VMEM((tm, D_MODEL), BF16)],
        compiler_params=_params(("arbitrary",)),
        name="project",
    )(x2, mod4, gn, win, qn, wuq, kvn, wukv, bfp, cos_t, sin_t)


def _fgate_body(lf_ref, tri_ref, pq_ref, pk_ref, cq_ref, ck_ref, qa_out, ka_out):
    carry = jnp.zeros((1, LANES), F32)
    for blk in range(SEQ_LEN // CUM_T):
        rows = slice(blk * CUM_T, (blk + 1) * CUM_T)
        hi, mid, lo = _split3(lf_ref[rows, :])
        tri = tri_ref[...]
        f_blk = _dot(tri, hi) + _dot(tri, mid) + _dot(tri, lo) + carry
        carry = f_blk[CUM_T - 1:CUM_T, :]
        f_hi, f_mid, f_lo = _split3(f_blk)
        qa = _dot(f_hi, pq_ref[0]) + _dot(f_mid, pq_ref[1]) + _dot(f_lo, pq_ref[2]) + cq_ref[...]
        ka = ck_ref[...] - (_dot(f_hi, pk_ref[0]) + _dot(f_mid, pk_ref[1]) + _dot(f_lo, pk_ref[2]))
        qa_out[rows, :] = qa.astype(BF16)
        ka_out[rows, :] = ka.astype(BF16)


def _fgate_constants():
    tri = np.tril(np.ones((CUM_T, CUM_T), np.float32))
    n_pair = FOX_H // 2
    pq = np.zeros((3, LANES, n_pair * LANES), np.float32)
    pk = np.zeros((3, LANES, n_pair * LANES), np.float32)
    cq = np.zeros((1, n_pair * LANES), np.float32)
    ck = np.zeros((1, n_pair * LANES), np.float32)
    for h in range(FOX_H):
        base = (h // 2) * LANES + (h % 2) * SUBLANES
        for t in range(3):
            pq[t, h, base + t] = 1.0
            pk[t, h, base + 3 + t] = 1.0
            cq[0, base + 3 + t] = 1.0
            ck[0, base + t] = 1.0
    as_bf16 = lambda a: jnp.asarray(a, BF16)
    return as_bf16(tri), as_bf16(pq), as_bf16(pk), jnp.asarray(cq), jnp.asarray(ck)


def _fgate(logf):
    tri, pq, pk, cq, ck = _fgate_constants()
    aug_w = (FOX_H // 2) * LANES
    seq_blk = lambda b: (b, 0)
    return pl.pallas_call(
        _fgate_body,
        grid=(N_BATCH,),
        in_specs=[
            pl.BlockSpec((SEQ_LEN, LANES), seq_blk),
            _resident((CUM_T, CUM_T)),
            _resident((3, LANES, aug_w)),
            _resident((3, LANES, aug_w)),
            _resident((1, aug_w)),
            _resident((1, aug_w)),
        ],
        out_specs=[pl.BlockSpec((SEQ_LEN, aug_w), seq_blk), pl.BlockSpec((SEQ_LEN, aug_w), seq_blk)],
        out_shape=[jax.ShapeDtypeStruct((N_TOK, aug_w), BF16), jax.ShapeDtypeStruct((N_TOK, aug_w), BF16)],
        compiler_params=_params(("arbitrary",)),
        name="fgate",
    )(logf, tri, pq, pk, cq, ck)


def _lane_band(shape, lo, width):
    lane = lax.broadcasted_iota(jnp.int32, shape, len(shape) - 1)
    return (lane >= lo) & (lane < lo + width)


def _positions(r0, k0):
    rows = r0 + lax.broadcasted_iota(jnp.int32, (ATT_TQ, ATT_TK), 0)
    cols = k0 + lax.broadcasted_iota(jnp.int32, (ATT_TQ, ATT_TK), 1)
    return rows, cols


def _softmax_step(s, v_blk, m_ref, l_ref, acc_ref):
    m_prev = m_ref[...]
    m_new = jnp.maximum(m_prev, jnp.max(s, axis=-1, keepdims=True))
    alpha = jnp.exp(m_prev - m_new)
    p = jnp.exp(s - m_new)
    l_ref[...] = alpha * l_ref[...] + jnp.sum(p, axis=-1, keepdims=True)
    acc_ref[...] = alpha * acc_ref[...] + _dot(p.astype(BF16), v_blk)
    m_ref[...] = m_new


def _softmax_rows(qcat, kcat_at, v_at, mask_fn, r0, i, m_ref, l_ref, acc_ref):
    m_ref[...] = jnp.full(m_ref.shape, -jnp.inf, F32)
    l_ref[...] = jnp.zeros(l_ref.shape, F32)
    acc_ref[...] = jnp.zeros(acc_ref.shape, F32)
    diag = lax.div(i * ATT_TQ, ATT_TK)

    def full_block(j, carry):
        k0 = pl.multiple_of(j * ATT_TK, ATT_TK)
        _softmax_step(_dot_nt(qcat, kcat_at(k0)), v_at(k0), m_ref, l_ref, acc_ref)
        return carry

    lax.fori_loop(0, diag, full_block, 0)
    k0 = pl.multiple_of(diag * ATT_TK, ATT_TK)
    rows, cols = _positions(r0, k0)
    s = jnp.where(mask_fn(rows, cols), _dot_nt(qcat, kcat_at(k0)), -jnp.inf)
    _softmax_step(s, v_at(k0), m_ref, l_ref, acc_ref)
    return acc_ref[...] / l_ref[...]


def _pair_norm(o, gain, head_d):
    first = _lane_band(o.shape, 0, head_d)
    sq = o * o
    ss_a = jnp.sum(jnp.where(first, sq, 0.0), axis=-1, keepdims=True)
    ss_b = jnp.sum(sq, axis=-1, keepdims=True) - ss_a
    inv = jnp.where(first, lax.rsqrt(ss_a / head_d + NORM_EPS), lax.rsqrt(ss_b / head_d + NORM_EPS))
    return o * inv * gain


def _mla_body(qn_ref, qr_ref, kn_ref, v_ref, kr_ref, g_ref, o_ref, kcat_ref, m_ref, l_ref, acc_ref):
    d = MLA_NOPE_D
    for hh in range(2):
        kcat_ref[hh, :, :d] = kn_ref[:, hh * d:(hh + 1) * d]
        kcat_ref[hh, :, d:] = kr_ref[...]

    def chunk_mask(rows, cols):
        return cols < ((rows >> 6) + 1) * CHUNK_LEN

    def q_block(i, carry):
        r0 = pl.multiple_of(i * ATT_TQ, ATT_TQ)
        for hh in range(2):
            qn = qn_ref[pl.ds(r0, ATT_TQ), hh * d:(hh + 1) * d]
            qr = qr_ref[pl.ds(r0, ATT_TQ), :]
            qr = jnp.where(_lane_band(qr.shape, hh * MLA_ROPE_D, MLA_ROPE_D), qr, jnp.zeros_like(qr))
            qcat = jnp.concatenate([qn, qr], axis=1)
            o = _softmax_rows(
                qcat,
                lambda k0: kcat_ref[hh, pl.ds(k0, ATT_TK), :],
                lambda k0: v_ref[pl.ds(k0, ATT_TK), hh * MLA_V_D:(hh + 1) * MLA_V_D],
                chunk_mask, r0, i, m_ref, l_ref, acc_ref)
            o_ref[pl.ds(r0, ATT_TQ), hh * MLA_V_D:(hh + 1) * MLA_V_D] = _rms(
                o, g_ref[:, hh * MLA_V_D:(hh + 1) * MLA_V_D]).astype(BF16)
        return carry

    lax.fori_loop(0, SEQ_LEN // ATT_TQ, q_block, 0)


def _mla_attention(q_mla, kv_mla, krope, g_mla):
    n_pair = MLA_H // 2
    pair_w = 2 * MLA_NOPE_D
    return pl.pallas_call(
        _mla_body,
        grid=(N_BATCH, n_pair),
        in_specs=[
            pl.BlockSpec((SEQ_LEN, pair_w), lambda b, p: (b, p)),
            pl.BlockSpec((SEQ_LEN, LANES), lambda b, p: (b, MLA_H * MLA_NOPE_D // LANES + p)),
            pl.BlockSpec((SEQ_LEN, pair_w), lambda b, p: (b, p)),
            pl.BlockSpec((SEQ_LEN, pair_w), lambda b, p: (b, n_pair + p)),
            pl.BlockSpec((SEQ_LEN, LANES), lambda b, p: (b, 0)),
            pl.BlockSpec((1, pair_w), lambda b, p: (0, p)),
        ],
        out_specs=pl.BlockSpec((SEQ_LEN, pair_w), lambda b, p: (b, p)),
        out_shape=jax.ShapeDtypeStruct((N_TOK, MLA_H * MLA_V_D), BF16),
        scratch_shapes=[
            pltpu.VMEM((2, SEQ_LEN, 2 * LANES), BF16),
            pltpu.VMEM((ATT_TQ, 1), F32),
            pltpu.VMEM((ATT_TQ, 1), F32),
            pltpu.VMEM((ATT_TQ, MLA_V_D), F32),
        ],
        compiler_params=_params(("arbitrary", "arbitrary")),
        name="mla_attention",
    )(q_mla, q_mla, kv_mla, kv_mla, krope, g_mla)


def _fox_body(q_ref, k_ref, v_ref, qa_ref, ka_ref, g_ref, o_ref, kcat_ref, m_ref, l_ref, acc_ref):
    kcat_ref[:, :LANES] = k_ref[...]
    kcat_ref[:, LANES:] = ka_ref[...]

    def causal(rows, cols):
        return cols <= rows

    def q_block(i, carry):
        r0 = pl.multiple_of(i * ATT_TQ, ATT_TQ)
        q = q_ref[pl.ds(r0, ATT_TQ), :]
        qa = qa_ref[pl.ds(r0, ATT_TQ), :]
        zero = jnp.zeros_like(q)
        outs = []
        for hh in range(2):
            qcat = jnp.concatenate([
                jnp.where(_lane_band(q.shape, hh * FOX_D, FOX_D), q, zero),
                jnp.where(_lane_band(qa.shape, hh * SUBLANES, SUBLANES), qa, zero)], axis=1)
            outs.append(_softmax_rows(
                qcat,
                lambda k0: kcat_ref[pl.ds(k0, ATT_TK), :],
                lambda k0: v_ref[pl.ds(k0, ATT_TK), :],
                causal, r0, i, m_ref, l_ref, acc_ref))
        o = jnp.where(_lane_band(outs[0].shape, 0, FOX_D), outs[0], outs[1])
        o_ref[pl.ds(r0, ATT_TQ), :] = _pair_norm(o, g_ref[...], FOX_D).astype(BF16)
        return carry

    lax.fori_loop(0, SEQ_LEN // ATT_TQ, q_block, 0)


def _fox_attention(p_att, qaug, kaug, g_fx):
    n_pair = FOX_H // 2
    grp = HEAD_GRP // LANES
    return pl.pallas_call(
        _fox_body,
        grid=(N_BATCH, n_pair),
        in_specs=[
            pl.BlockSpec((SEQ_LEN, LANES), lambda b, p: (b, 3 * grp + p)),
            pl.BlockSpec((SEQ_LEN, LANES), lambda b, p: (b, 4 * grp + p)),
            pl.BlockSpec((SEQ_LEN, LANES), lambda b, p: (b, 5 * grp + p)),
            pl.BlockSpec((SEQ_LEN, LANES), lambda b, p: (b, p)),
            pl.BlockSpec((SEQ_LEN, LANES), lambda b, p: (b, p)),
            pl.BlockSpec((1, LANES), lambda b, p: (0, p)),
        ],
        out_specs=pl.BlockSpec((SEQ_LEN, LANES), lambda b, p: (b, p)),
        out_shape=jax.ShapeDtypeStruct((N_TOK, FOX_H * FOX_D), BF16),
        scratch_shapes=[
            pltpu.VMEM((SEQ_LEN, 2 * LANES), BF16),
            pltpu.VMEM((ATT_TQ, 1), F32),
            pltpu.VMEM((ATT_TQ, 1), F32),
            pltpu.VMEM((ATT_TQ, LANES), F32),
        ],
        compiler_params=_params(("arbitrary", "arbitrary")),
        name="fox_attention",
    )(p_att, p_att, p_att, qaug, kaug, g_fx)


def _sb_body(q_ref, k_ref, v_ref, tri_ref, g_ref, o_ref, c_ref, acc_ref):
    def sb_step(q_m, k0, mask):
        z = _dot_nt(q_m, k_ref[pl.ds(k0, ATT_TK), :])
        log_keep = -_softplus(z)
        if mask is not None:
            log_keep = jnp.where(mask, log_keep, 0.0)
        hi = log_keep.astype(BF16)
        lo = (log_keep - hi.astype(F32)).astype(BF16)
        suffix = _dot(hi, tri_ref[...]) + _dot(lo, tri_ref[...])
        w = jnp.exp(z + suffix + c_ref[...])
        if mask is not None:
            w = jnp.where(mask, w, 0.0)
        acc_ref[...] += _dot(w.astype(BF16), v_ref[pl.ds(k0, ATT_TK), :])
        c_ref[...] += suffix[:, 0:1]

    def q_block(i, carry):
        r0 = pl.multiple_of(i * ATT_TQ, ATT_TQ)
        q = q_ref[pl.ds(r0, ATT_TQ), :]
        zero = jnp.zeros_like(q)
        diag = lax.div(i * ATT_TQ, ATT_TK)
        outs = []
        for hh in range(2):
            q_m = jnp.where(_lane_band(q.shape, hh * SB_D, SB_D), q, zero)
            c_ref[...] = jnp.zeros(c_ref.shape, F32)
            acc_ref[...] = jnp.zeros(acc_ref.shape, F32)
            k0 = pl.multiple_of(diag * ATT_TK, ATT_TK)
            rows, cols = _positions(r0, k0)
            sb_step(q_m, k0, cols < rows)

            def left_block(jj, carry2):
                sb_step(q_m, pl.multiple_of((diag - 1 - jj) * ATT_TK, ATT_TK), None)
                return carry2

            lax.fori_loop(0, diag, left_block, 0)
            outs.append(acc_ref[...])
        o = jnp.where(_lane_band(outs[0].shape, 0, SB_D), outs[0], outs[1])
        o_ref[pl.ds(r0, ATT_TQ), :] = _pair_norm(o, g_ref[...], SB_D).astype(BF16)
        return carry

    lax.fori_loop(0, SEQ_LEN // ATT_TQ, q_block, 0)


def _sb_attention(p_att, g_sb):
    n_pair = SB_H // 2
    grp = HEAD_GRP // LANES
    tri = jnp.asarray(np.tril(np.ones((ATT_TK, ATT_TK), np.float32)), BF16)
    return pl.pallas_call(
        _sb_body,
        grid=(N_BATCH, n_pair),
        in_specs=[
            pl.BlockSpec((SEQ_LEN, LANES), lambda b, p: (b, p)),
            pl.BlockSpec((SEQ_LEN, LANES), lambda b, p: (b, grp + p)),
            pl.BlockSpec((SEQ_LEN, LANES), lambda b, p: (b, 2 * grp + p)),
            _resident((ATT_TK, ATT_TK)),
            pl.BlockSpec((1, LANES), lambda b, p: (0, p)),
        ],
        out_specs=pl.BlockSpec((SEQ_LEN, LANES), lambda b, p: (b, p)),
        out_shape=jax.ShapeDtypeStruct((N_TOK, SB_H * SB_D), BF16),
        scratch_shapes=[pltpu.VMEM((ATT_TQ, 1), F32), pltpu.VMEM((ATT_TQ, LANES), F32)],
        compiler_params=_params(("arbitrary", "arbitrary")),
        name="sb_attention",
    )(p_att, p_att, p_att, tri, g_sb)


def _outproj_body(x_ref, mod_ref, om_ref, os_ref, of_ref, w_ref, o_ref):
    n_mla = MLA_H * MLA_V_D
    n_sb = SB_H * SB_D
    y = (_dot(om_ref[...], w_ref[:n_mla, :]) + _dot(os_ref[...], w_ref[n_mla:n_mla + n_sb, :])
         + _dot(of_ref[...], w_ref[n_mla + n_sb:, :]))
    o_ref[...] = x_ref[...] + mod_ref[0, 0][2:3] * y


def _out_project(x2, mod4, layer, o_mla, o_sb, o_fx, w_out):
    tm = PROJ_TM
    tiles_per_seq = SEQ_LEN // tm
    row = lambda i: (i, 0)
    return pl.pallas_call(
        _outproj_body,
        grid=(N_TOK // tm,),
        in_specs=[
            pl.BlockSpec((tm, D_MODEL), row),
            pl.BlockSpec((1, 1, 6, D_MODEL), lambda i: (layer, i // tiles_per_seq, 0, 0)),
            pl.BlockSpec((tm, MLA_H * MLA_V_D), row),
            pl.BlockSpec((tm, SB_H * SB_D), row),
            pl.BlockSpec((tm, FOX_H * FOX_D), row),
            _resident((D_MODEL, D_MODEL)),
        ],
        out_specs=pl.BlockSpec((tm, D_MODEL), row),
        out_shape=jax.ShapeDtypeStruct((N_TOK, D_MODEL), F32),
        compiler_params=_params(("arbitrary",)),
        name="out_project",
    )(x2, mod4, o_mla, o_sb, o_fx, w_out)


def _ffn_body(x_ref, mod_ref, gn_ref, w1_ref, w2_ref, gf_ref, o_ref, h_ref, acc_ref, *, final):
    j = pl.program_id(1)
    mod = mod_ref[0, 0]

    @pl.when(j == 0)
    def _():
        h = _rms(x_ref[...], gn_ref[...]) * (1.0 + mod[4:5]) + mod[3:4]
        h_ref[...] = h.astype(BF16)
        acc_ref[...] = jnp.zeros(acc_ref.shape, F32)

    a = jnp.maximum(_dot(h_ref[...], w1_ref[...]), 0.0)
    acc_ref[...] += _dot((a * a).astype(BF16), w2_ref[...])

    @pl.when(j == pl.num_programs(1) - 1)
    def _():
        y = x_ref[...] + mod[5:6] * acc_ref[...]
        o_ref[...] = _rms(y, gf_ref[...]) if final else y


def _ffn(x2, mod4, layer, gn, w1, w2, g_final, final):
    tm, tf = FFN_TM, FFN_TF
    tiles_per_seq = SEQ_LEN // tm
    return pl.pallas_call(
        functools.partial(_ffn_body, final=final),
        grid=(N_TOK // tm, FF_DIM // tf),
        in_specs=[
            pl.BlockSpec((tm, D_MODEL), lambda i, j: (i, 0)),
            pl.BlockSpec((1, 1, 6, D_MODEL), lambda i, j: (layer, i // tiles_per_seq, 0, 0)),
            pl.BlockSpec((1, D_MODEL), lambda i, j: (0, 0)),
            pl.BlockSpec((D_MODEL, tf), lambda i, j: (0, j)),
            pl.BlockSpec((tf, D_MODEL), lambda i, j: (j, 0)),
            pl.BlockSpec((1, D_MODEL), lambda i, j: (0, 0)),
        ],
        out_specs=pl.BlockSpec((tm, D_MODEL), lambda i, j: (i, 0)),
        out_shape=jax.ShapeDtypeStruct((N_TOK, D_MODEL), F32),
        scratch_shapes=[pltpu.VMEM((tm, D_MODEL), BF16), pltpu.VMEM((tm, D_MODEL), F32)],
        compiler_params=_params(("arbitrary", "arbitrary")),
        name="ffn",
    )(x2, mod4, gn, w1, w2, g_final)


def _swap_halves(w):
    half = w.shape[-1] // 2
    return jnp.concatenate([w[..., half:], w[..., :half]], axis=-1)


def _pack_w_in(w):
    o_kr = MLA_QL + MLA_KVL
    o_sb = o_kr + MLA_ROPE_D
    o_f = o_sb + ATT_W
    kr = w[:, o_kr:o_sb]
    kr_sw = _swap_halves(kr)
    f_pad = jnp.pad(w[:, o_f:], ((0, 0), (0, LANES - FOX_H)))
    return jnp.concatenate([w[:, :o_kr], kr, kr, kr_sw, kr_sw, f_pad, w[:, o_sb:o_f]], axis=1).astype(BF16)


def _pack_w_uq(w):
    w3 = w.reshape(MLA_QL, MLA_H, MLA_NOPE_D + MLA_ROPE_D)
    nope = w3[:, :, :MLA_NOPE_D].reshape(MLA_QL, -1)
    rope = w3[:, :, MLA_NOPE_D:]
    return jnp.concatenate(
        [nope, rope.reshape(MLA_QL, -1), _swap_halves(rope).reshape(MLA_QL, -1)], axis=1).astype(BF16)


def _pack_w_ukv(w):
    w3 = w.reshape(MLA_KVL, MLA_H, MLA_NOPE_D + MLA_V_D)
    return jnp.concatenate(
        [w3[:, :, :MLA_NOPE_D].reshape(MLA_KVL, -1), w3[:, :, MLA_NOPE_D:].reshape(MLA_KVL, -1)],
        axis=1).astype(BF16)


def _rope_tables():
    pos = jnp.arange(SEQ_LEN, dtype=F32)
    inv = ROPE_THETA ** (-jnp.arange(0, MLA_ROPE_D, 2, dtype=F32) / MLA_ROPE_D)
    ang = pos[:, None] * inv[None, :]
    cos, sin = jnp.cos(ang), jnp.sin(ang)
    cos_t = jnp.tile(jnp.concatenate([cos, cos], axis=1), (1, MLA_H))
    sin_t = jnp.tile(jnp.concatenate([-sin, sin], axis=1), (1, MLA_H))
    return cos_t, sin_t


def kernel(x, c, w_ada, b_ada, norm_mix, w_in, q_norm, w_uq, kv_norm, w_ukv, b_forget, out_norm,
           w_out, norm_ffn, w_ff1, w_ff2, final_norm):
    assert x.shape == (N_BATCH, SEQ_LEN, D_MODEL) and c.shape == (N_BATCH, D_MODEL)
    cos_t, sin_t = _rope_tables()
    c_pad = jnp.pad(c, ((0, SUBLANES - N_BATCH), (0, 0)))
    mod = _ada_mod(c_pad, w_ada, b_ada)
    mod4 = mod[:, :N_BATCH].reshape(N_LAYERS, N_BATCH, 6, D_MODEL)

    n_mla = MLA_H * MLA_V_D
    n_sb = SB_H * SB_D
    row = lambda v: v.reshape(1, -1)
    x2 = x.reshape(N_TOK, D_MODEL)
    for l in range(N_LAYERS):
        bfp = jnp.pad(b_forget[l], (0, LANES - FOX_H)).reshape(1, LANES)
        q_mla, kv_mla, krope, p_att, logf = _project(
            x2, mod4, l, row(norm_mix[l]), _pack_w_in(w_in[l]), row(q_norm[l]), _pack_w_uq(w_uq[l]),
            row(kv_norm[l]), _pack_w_ukv(w_ukv[l]), bfp, cos_t, sin_t)
        qaug, kaug = _fgate(logf)
        g = out_norm[l]
        o_mla = _mla_attention(q_mla, kv_mla, krope, row(g[:n_mla]))
        o_sb = _sb_attention(p_att, row(g[n_mla:n_mla + n_sb]))
        o_fx = _fox_attention(p_att, qaug, kaug, row(g[n_mla + n_sb:]))
        x2 = _out_project(x2, mod4, l, o_mla, o_sb, o_fx, w_out[l].astype(BF16))
        x2 = _ffn(x2, mod4, l, row(norm_ffn[l]), w_ff1[l].astype(BF16), w_ff2[l].astype(BF16),
                  row(final_norm), final=(l == N_LAYERS - 1))
    return x2.reshape(N_BATCH, SEQ_LEN, D_MODEL)
```

```python
import functools

import numpy as np
import jax
import jax.numpy as jnp
from jax import lax
from jax.experimental import pallas as pl
from jax.experimental.pallas import tpu as pltpu

F32 = jnp.float32
BF16 = jnp.bfloat16

D_MODEL = 2048
N_BATCH = 4
SEQ_LEN = 2048
N_LAYERS = 2
CHUNK_LEN = 64
NORM_EPS = 1e-6
MLA_H = 8
MLA_QL = 512
MLA_KVL = 256
MLA_NOPE_D = 128
MLA_ROPE_D = 64
MLA_V_D = 128
ROPE_THETA = 10000.0
SB_H = 8
SB_D = 64
FOX_H = 8
FOX_D = 64
FF_DIM = 4 * D_MODEL
N_TOK = N_BATCH * SEQ_LEN

LANES = 128
SUBLANES = 8
VMEM_LIMIT = 56 * 1024 * 1024

C_CQ = 0
C_CKV = C_CQ + MLA_QL
C_KR = C_CKV + MLA_KVL
C_F = C_KR + 2 * LANES
C_ATT = C_F + LANES
ATT_W = 3 * SB_H * SB_D + 3 * FOX_H * FOX_D
IN_W = C_ATT + ATT_W
HEAD_GRP = SB_H * SB_D

PROJ_TM = 512
FFN_TM = 512
FFN_TF = 512
ADA_TN = 1024
ATT_T = 256
N_QB = SEQ_LEN // ATT_T
CUM_T = 256


def _dot(a, b):
    return jnp.dot(a, b, preferred_element_type=F32)


def _dot_nt(a, b):
    return lax.dot_general(a, b, (((1,), (1,)), ((), ())), preferred_element_type=F32)


def _split3(x):
    hi = x.astype(BF16)
    r1 = x - hi.astype(F32)
    mid = r1.astype(BF16)
    lo = (r1 - mid.astype(F32)).astype(BF16)
    return hi, mid, lo


def _softplus(z):
    return jnp.maximum(z, 0.0) + jnp.log1p(jnp.exp(-jnp.abs(z)))


def _rms(x, gain):
    return x * lax.rsqrt(jnp.mean(x * x, axis=-1, keepdims=True) + NORM_EPS) * gain


def _resident(shape):
    nd = len(shape)
    return pl.BlockSpec(shape, lambda *_: (0,) * nd, pipeline_mode=pl.Buffered(1))


def _params(sem):
    return pltpu.CompilerParams(dimension_semantics=sem, vmem_limit_bytes=VMEM_LIMIT)


def _ada_body(c_ref, w_ref, b_ref, o_ref):
    c = c_ref[...]
    c_act = (c * jax.nn.sigmoid(c)).astype(BF16)
    o_ref[0] = _dot(c_act, w_ref[0].astype(BF16)) + b_ref[0]


def _ada_mod(c_pad, w_ada, b_ada):
    n_out = 6 * D_MODEL
    return pl.pallas_call(
        _ada_body,
        grid=(N_LAYERS, n_out // ADA_TN),
        in_specs=[
            pl.BlockSpec((SUBLANES, D_MODEL), lambda l, j: (0, 0)),
            pl.BlockSpec((1, D_MODEL, ADA_TN), lambda l, j: (l, 0, j)),
            pl.BlockSpec((1, 1, ADA_TN), lambda l, j: (l, 0, j)),
        ],
        out_specs=pl.BlockSpec((1, SUBLANES, ADA_TN), lambda l, j: (l, 0, j)),
        out_shape=jax.ShapeDtypeStruct((N_LAYERS, SUBLANES, n_out), F32),
        compiler_params=_params(("arbitrary", "arbitrary")),
        name="ada_mod",
    )(c_pad, w_ada, b_ada.reshape(N_LAYERS, 1, n_out))


def _proj_body(x_ref, mod_ref, gn_ref, win_ref, qn_ref, wuq_ref, kvn_ref, wukv_ref, bf_ref,
               cos_ref, sin_ref, q_out, kv_out, kr_out, att_out, lf_out, h_ref):
    mod = mod_ref[0, 0]
    x = x_ref[...]
    h = _rms(x, gn_ref[...]) * (1.0 + mod[1:2]) + mod[0:1]
    h_ref[...] = h.astype(BF16)

    def proj(c0, width):
        return _dot(h_ref[...], win_ref[:, c0:c0 + width])

    mla_scale = (MLA_NOPE_D + MLA_ROPE_D) ** -0.5
    n_nope = MLA_H * MLA_NOPE_D
    n_rope = MLA_H * MLA_ROPE_D

    cq = _rms(proj(C_CQ, MLA_QL), qn_ref[...]).astype(BF16)
    q = _dot(cq, wuq_ref[...])
    q_out[:, :n_nope] = (q[:, :n_nope] * mla_scale).astype(BF16)
    q_rot = q[:, n_nope:n_nope + n_rope] * cos_ref[...] + q[:, n_nope + n_rope:] * sin_ref[...]
    q_out[:, n_nope:] = (q_rot * mla_scale).astype(BF16)

    ckv = _rms(proj(C_CKV, MLA_KVL), kvn_ref[...]).astype(BF16)
    kv_out[...] = _dot(ckv, wukv_ref[...]).astype(BF16)

    kr = proj(C_KR, 2 * LANES)
    kr_out[...] = (kr[:, :LANES] * cos_ref[:, :LANES] + kr[:, LANES:] * sin_ref[:, :LANES]).astype(BF16)

    lf_out[...] = -_softplus(-(proj(C_F, LANES) + bf_ref[...]))

    sb_scale = SB_D ** -0.5
    fox_scale = FOX_D ** -0.5
    scales = (sb_scale, 1.0, 1.0, fox_scale, 1.0, 1.0)
    for g, sc in enumerate(scales):
        blk = proj(C_ATT + g * HEAD_GRP, HEAD_GRP)
        if sc != 1.0:
            blk = blk * sc
        att_out[:, g * HEAD_GRP:(g + 1) * HEAD_GRP] = blk.astype(BF16)


def _project(x2, mod4, layer, gn, win, qn, wuq, kvn, wukv, bfp, cos_t, sin_t):
    tm = PROJ_TM
    tiles_per_seq = SEQ_LEN // tm
    q_w = MLA_H * (MLA_NOPE_D + MLA_ROPE_D)
    kv_w = MLA_H * (MLA_NOPE_D + MLA_V_D)
    rope_w = MLA_H * MLA_ROPE_D
    row = lambda i: (i, 0)
    return pl.pallas_call(
        _proj_body,
        grid=(N_TOK // tm,),
        in_specs=[
            pl.BlockSpec((tm, D_MODEL), row),
            pl.BlockSpec((1, 1, 6, D_MODEL), lambda i: (layer, i // tiles_per_seq, 0, 0)),
            _resident((1, D_MODEL)),
            _resident((D_MODEL, IN_W)),
            _resident((1, MLA_QL)),
            _resident((MLA_QL, q_w + rope_w)),
            _resident((1, MLA_KVL)),
            _resident((MLA_KVL, kv_w)),
            _resident((1, LANES)),
            pl.BlockSpec((tm, rope_w), lambda i: (i % tiles_per_seq, 0)),
            pl.BlockSpec((tm, rope_w), lambda i: (i % tiles_per_seq, 0)),
        ],
        out_specs=[
            pl.BlockSpec((tm, q_w), row),
            pl.BlockSpec((tm, kv_w), row),
            pl.BlockSpec((tm, LANES), row),
            pl.BlockSpec((tm, ATT_W), row),
            pl.BlockSpec((tm, LANES), row),
        ],
        out_shape=[
            jax.ShapeDtypeStruct((N_TOK, q_w), BF16),
            jax.ShapeDtypeStruct((N_TOK, kv_w), BF16),
            jax.ShapeDtypeStruct((N_TOK, LANES), BF16),
            jax.ShapeDtypeStruct((N_TOK, ATT_W), BF16),
            jax.ShapeDtypeStruct((N_TOK, LANES), F32),
        ],
        scratch_shapes=[pltpu.VMEM((tm, D_MODEL), BF16)],
        compiler_params=_params(("arbitrary",)),
        name="project",
    )(x2, mod4, gn, win, qn, wuq, kvn, wukv, bfp, cos_t, sin_t)


def _fgate_body(lf_ref, tri_ref, pq_ref, pk_ref, cq_ref, ck_ref, qa_out, ka_out):
    carry = jnp.zeros((1, LANES), F32)
    for blk in range(SEQ_LEN // CUM_T):
        rows = slice(blk * CUM_T, (blk + 1) * CUM_T)
        hi, mid, lo = _split3(lf_ref[rows, :])
        tri = tri_ref[...]
        f_blk = _dot(tri, hi) + _dot(tri, mid) + _dot(tri, lo) + carry
        carry = f_blk[CUM_T - 1:CUM_T, :]
        f_hi, f_mid, f_lo = _split3(f_blk)
        qa = _dot(f_hi, pq_ref[0]) + _dot(f_mid, pq_ref[1]) + _dot(f_lo, pq_ref[2]) + cq_ref[...]
        ka = ck_ref[...] - (_dot(f_hi, pk_ref[0]) + _dot(f_mid, pk_ref[1]) + _dot(f_lo, pk_ref[2]))
        qa_out[rows, :] = qa.astype(BF16)
        ka_out[rows, :] = ka.astype(BF16)


def _fgate_constants():
    tri = np.tril(np.ones((CUM_T, CUM_T), np.float32))
    n_pair = FOX_H // 2
    pq = np.zeros((3, LANES, n_pair * LANES), np.float32)
    pk = np.zeros((3, LANES, n_pair * LANES), np.float32)
    cq = np.zeros((1, n_pair * LANES), np.float32)
    ck = np.zeros((1, n_pair * LANES), np.float32)
    for h in range(FOX_H):
        base = (h // 2) * LANES + (h % 2) * SUBLANES
        for t in range(3):
            pq[t, h, base + t] = 1.0
            pk[t, h, base + 3 + t] = 1.0
            cq[0, base + 3 + t] = 1.0
            ck[0, base + t] = 1.0
    as_bf16 = lambda a: jnp.asarray(a, BF16)
    return as_bf16(tri), as_bf16(pq), as_bf16(pk), jnp.asarray(cq), jnp.asarray(ck)


def _fgate(logf):
    tri, pq, pk, cq, ck = _fgate_constants()
    aug_w = (FOX_H // 2) * LANES
    seq_blk = lambda b: (b, 0)
    return pl.pallas_call(
        _fgate_body,
        grid=(N_BATCH,),
        in_specs=[
            pl.BlockSpec((SEQ_LEN, LANES), seq_blk),
            _resident((CUM_T, CUM_T)),
            _resident((3, LANES, aug_w)),
            _resident((3, LANES, aug_w)),
            _resident((1, aug_w)),
            _resident((1, aug_w)),
        ],
        out_specs=[pl.BlockSpec((SEQ_LEN, aug_w), seq_blk), pl.BlockSpec((SEQ_LEN, aug_w), seq_blk)],
        out_shape=[jax.ShapeDtypeStruct((N_TOK, aug_w), BF16), jax.ShapeDtypeStruct((N_TOK, aug_w), BF16)],
        compiler_params=_params(("arbitrary",)),
        name="fgate",
    )(logf, tri, pq, pk, cq, ck)


def _lane_band(shape, lo, width):
    lane = lax.broadcasted_iota(jnp.int32, shape, len(shape) - 1)
    return (lane >= lo) & (lane < lo + width)


def _rel_positions():
    rows = lax.broadcasted_iota(jnp.int32, (ATT_T, ATT_T), 0)
    cols = lax.broadcasted_iota(jnp.int32, (ATT_T, ATT_T), 1)
    return rows, cols


def _col_block(a, kb):
    return a[:, kb * ATT_T:(kb + 1) * ATT_T]


def _weights_times_v(p_ref, hh, nkb, v):
    p = jnp.concatenate([p_ref[hh, kb] for kb in range(nkb)], axis=1) if nkb > 1 else p_ref[hh, 0]
    return _dot(p, v)


def _softmax_section(i, qcats, kcat_at, vaug_at, mask_fn, s_ref, p_ref):
    nkb = i + 1
    klen = nkb * ATT_T
    row_max = []
    for hh in range(2):
        s = _dot_nt(qcats[hh], kcat_at(hh, klen))
        blocks = [_col_block(s, kb) for kb in range(nkb)]
        blocks[-1] = jnp.where(mask_fn(*_rel_positions()), blocks[-1], -jnp.inf)
        m = None
        for kb in range(nkb):
            s_ref[hh, kb] = blocks[kb]
            bm = jnp.maximum(blocks[kb][:, :LANES], blocks[kb][:, LANES:])
            m = bm if m is None else jnp.maximum(m, bm)
        row_max.append(jnp.max(m, axis=-1, keepdims=True))

    def exp_block(kb, carry):
        for hh in range(2):
            p_ref[hh, kb] = jnp.exp(s_ref[hh, kb] - row_max[hh]).astype(BF16)
        return carry

    lax.fori_loop(0, nkb, exp_block, 0)
    return [_weights_times_v(p_ref, hh, nkb, vaug_at(hh, klen)) for hh in range(2)]


def _pair_norm(o, gain, head_d):
    first = _lane_band(o.shape, 0, head_d)
    sq = o * o
    ss_a = jnp.sum(jnp.where(first, sq, 0.0), axis=-1, keepdims=True)
    ss_b = jnp.sum(sq, axis=-1, keepdims=True) - ss_a
    inv = jnp.where(first, lax.rsqrt(ss_a / head_d + NORM_EPS), lax.rsqrt(ss_b / head_d + NORM_EPS))
    return o * inv * gain


_SCORE_SCRATCH = [
    pltpu.VMEM((2, N_QB, ATT_T, ATT_T), F32),
    pltpu.VMEM((2, N_QB, ATT_T, ATT_T), BF16),
]


def _mla_body(qn_ref, qr_ref, kn_ref, v_ref, kr_ref, g_ref, o_ref, kcat_ref, vaug_ref, s_ref, p_ref):
    d = MLA_NOPE_D
    for hh in range(2):
        kcat_ref[hh, :, :d] = kn_ref[:, hh * d:(hh + 1) * d]
        kcat_ref[hh, :, d:] = kr_ref[...]
        vaug_ref[hh, :, :d] = v_ref[:, hh * d:(hh + 1) * d]
        vaug_ref[hh, :, d:] = jnp.ones((SEQ_LEN, LANES), BF16)

    def chunk_mask(rows, cols):
        return cols < ((rows >> 6) + 1) * CHUNK_LEN

    for i in range(N_QB):
        r = slice(i * ATT_T, (i + 1) * ATT_T)
        qr = qr_ref[r, :]
        zero = jnp.zeros_like(qr)
        qcats = [jnp.concatenate([
            qn_ref[r, hh * d:(hh + 1) * d],
            jnp.where(_lane_band(qr.shape, hh * MLA_ROPE_D, MLA_ROPE_D), qr, zero)], axis=1)
            for hh in range(2)]
        outs = _softmax_section(
            i, qcats, lambda hh, klen: kcat_ref[hh, :klen, :], lambda hh, klen: vaug_ref[hh, :klen, :],
            chunk_mask, s_ref, p_ref)
        for hh in range(2):
            o = outs[hh][:, :d] / outs[hh][:, d:]
            o_ref[r, hh * d:(hh + 1) * d] = _rms(o, g_ref[:, hh * d:(hh + 1) * d]).astype(BF16)


def _mla_attention(q_mla, kv_mla, krope, g_mla):
    n_pair = MLA_H // 2
    pair_w = 2 * MLA_NOPE_D
    return pl.pallas_call(
        _mla_body,
        grid=(N_BATCH, n_pair),
        in_specs=[
            pl.BlockSpec((SEQ_LEN, pair_w), lambda b, p: (b, p)),
            pl.BlockSpec((SEQ_LEN, LANES), lambda b, p: (b, MLA_H * MLA_NOPE_D // LANES + p)),
            pl.BlockSpec((SEQ_LEN, pair_w), lambda b, p: (b, p)),
            pl.BlockSpec((SEQ_LEN, pair_w), lambda b, p: (b, n_pair + p)),
            pl.BlockSpec((SEQ_LEN, LANES), lambda b, p: (b, 0)),
            pl.BlockSpec((1, pair_w), lambda b, p: (0, p)),
        ],
        out_specs=pl.BlockSpec((SEQ_LEN, pair_w), lambda b, p: (b, p)),
        out_shape=jax.ShapeDtypeStruct((N_TOK, MLA_H * MLA_V_D), BF16),
        scratch_shapes=[
            pltpu.VMEM((2, SEQ_LEN, 2 * LANES), BF16),
            pltpu.VMEM((2, SEQ_LEN, 2 * LANES), BF16),
        ] + _SCORE_SCRATCH,
        compiler_params=_params(("arbitrary", "arbitrary")),
        name="mla_attention",
    )(q_mla, q_mla, kv_mla, kv_mla, krope, g_mla)


def _fox_body(q_ref, k_ref, v_ref, qa_ref, ka_ref, g_ref, o_ref, kcat_ref, vaug_ref, s_ref, p_ref):
    kcat_ref[:, :LANES] = k_ref[...]
    kcat_ref[:, LANES:] = ka_ref[...]
    vaug_ref[:, :LANES] = v_ref[...]
    vaug_ref[:, LANES:] = jnp.ones((SEQ_LEN, LANES), BF16)

    def causal(rows, cols):
        return cols <= rows

    for i in range(N_QB):
        r = slice(i * ATT_T, (i + 1) * ATT_T)
        q = q_ref[r, :]
        qa = qa_ref[r, :]
        zero = jnp.zeros_like(q)
        qcats = [jnp.concatenate([
            jnp.where(_lane_band(q.shape, hh * FOX_D, FOX_D), q, zero),
            jnp.where(_lane_band(qa.shape, hh * SUBLANES, SUBLANES), qa, zero)], axis=1)
            for hh in range(2)]
        outs = _softmax_section(
            i, qcats, lambda hh, klen: kcat_ref[:klen, :], lambda hh, klen: vaug_ref[:klen, :],
            causal, s_ref, p_ref)
        heads = [outs[hh][:, :LANES] / outs[hh][:, LANES:] for hh in range(2)]
        o = jnp.where(_lane_band(heads[0].shape, 0, FOX_D), heads[0], heads[1])
        o_ref[r, :] = _pair_norm(o, g_ref[...], FOX_D).astype(BF16)


def _fox_attention(p_att, qaug, kaug, g_fx):
    n_pair = FOX_H // 2
    grp = HEAD_GRP // LANES
    return pl.pallas_call(
        _fox_body,
        grid=(N_BATCH, n_pair),
        in_specs=[
            pl.BlockSpec((SEQ_LEN, LANES), lambda b, p: (b, 3 * grp + p)),
            pl.BlockSpec((SEQ_LEN, LANES), lambda b, p: (b, 4 * grp + p)),
            pl.BlockSpec((SEQ_LEN, LANES), lambda b, p: (b, 5 * grp + p)),
            pl.BlockSpec((SEQ_LEN, LANES), lambda b, p: (b, p)),
            pl.BlockSpec((SEQ_LEN, LANES), lambda b, p: (b, p)),
            pl.BlockSpec((1, LANES), lambda b, p: (0, p)),
        ],
        out_specs=pl.BlockSpec((SEQ_LEN, LANES), lambda b, p: (b, p)),
        out_shape=jax.ShapeDtypeStruct((N_TOK, FOX_H * FOX_D), BF16),
        scratch_shapes=[
            pltpu.VMEM((SEQ_LEN, 2 * LANES), BF16),
            pltpu.VMEM((SEQ_LEN, 2 * LANES), BF16),
        ] + _SCORE_SCRATCH,
        compiler_params=_params(("arbitrary", "arbitrary")),
        name="fox_attention",
    )(p_att, p_att, p_att, qaug, kaug, g_fx)


def _sb_body(q_ref, k_ref, v_ref, tri_ref, g_ref, o_ref, z_ref, p_ref, c_ref):
    def sb_block(kb, masked):
        for hh in range(2):
            z = z_ref[hh, kb]
            log_keep = -_softplus(z)
            if masked:
                rows, cols = _rel_positions()
                strict = cols < rows
                log_keep = jnp.where(strict, log_keep, 0.0)
            hi = log_keep.astype(BF16)
            lo = (log_keep - hi.astype(F32)).astype(BF16)
            suffix = _dot(hi, tri_ref[...]) + _dot(lo, tri_ref[...])
            w = jnp.exp(z + suffix + c_ref[hh])
            if masked:
                w = jnp.where(strict, w, 0.0)
            p_ref[hh, kb] = w.astype(BF16)
            c_ref[hh] = c_ref[hh] + jnp.sum(log_keep, axis=-1, keepdims=True)

    for i in range(N_QB):
        nkb = i + 1
        klen = nkb * ATT_T
        r = slice(i * ATT_T, (i + 1) * ATT_T)
        q = q_ref[r, :]
        zero = jnp.zeros_like(q)
        for hh in range(2):
            z = _dot_nt(jnp.where(_lane_band(q.shape, hh * SB_D, SB_D), q, zero), k_ref[:klen, :])
            for kb in range(nkb):
                z_ref[hh, kb] = _col_block(z, kb)
        c_ref[...] = jnp.zeros(c_ref.shape, F32)
        sb_block(nkb - 1, True)

        def left_block(jj, carry):
            sb_block(nkb - 2 - jj, False)
            return carry

        lax.fori_loop(0, nkb - 1, left_block, 0)
        heads = [_weights_times_v(p_ref, hh, nkb, v_ref[:klen, :]) for hh in range(2)]
        o = jnp.where(_lane_band(heads[0].shape, 0, SB_D), heads[0], heads[1])
        o_ref[r, :] = _pair_norm(o, g_ref[...], SB_D).astype(BF16)


def _sb_attention(p_att, g_sb):
    n_pair = SB_H // 2
    grp = HEAD_GRP // LANES
    tri = jnp.asarray(np.tril(np.ones((ATT_T, ATT_T), np.float32)), BF16)
    return pl.pallas_call(
        _sb_body,
        grid=(N_BATCH, n_pair),
        in_specs=[
            pl.BlockSpec((SEQ_LEN, LANES), lambda b, p: (b, p)),
            pl.BlockSpec((SEQ_LEN, LANES), lambda b, p: (b, grp + p)),
            pl.BlockSpec((SEQ_LEN, LANES), lambda b, p: (b, 2 * grp + p)),
            _resident((ATT_T, ATT_T)),
            pl.BlockSpec((1, LANES), lambda b, p: (0, p)),
        ],
        out_specs=pl.BlockSpec((SEQ_LEN, LANES), lambda b, p: (b, p)),
        out_shape=jax.ShapeDtypeStruct((N_TOK, SB_H * SB_D), BF16),
        scratch_shapes=_SCORE_SCRATCH + [pltpu.VMEM((2, ATT_T, 1), F32)],
        compiler_params=_params(("arbitrary", "arbitrary")),
        name="sb_attention",
    )(p_att, p_att, p_att, tri, g_sb)


def _outproj_body(x_ref, mod_ref, om_ref, os_ref, of_ref, w_ref, o_ref):
    n_mla = MLA_H * MLA_V_D
    n_sb = SB_H * SB_D
    y = (_dot(om_ref[...], w_ref[:n_mla, :]) + _dot(os_ref[...], w_ref[n_mla:n_mla + n_sb, :])
         + _dot(of_ref[...], w_ref[n_mla + n_sb:, :]))
    o_ref[...] = x_ref[...] + mod_ref[0, 0][2:3] * y


def _out_project(x2, mod4, layer, o_mla, o_sb, o_fx, w_out):
    tm = PROJ_TM
    tiles_per_seq = SEQ_LEN // tm
    row = lambda i: (i, 0)
    return pl.pallas_call(
        _outproj_body,
        grid=(N_TOK // tm,),
        in_specs=[
            pl.BlockSpec((tm, D_MODEL), row),
            pl.BlockSpec((1, 1, 6, D_MODEL), lambda i: (layer, i // tiles_per_seq, 0, 0)),
            pl.BlockSpec((tm, MLA_H * MLA_V_D), row),
            pl.BlockSpec((tm, SB_H * SB_D), row),
            pl.BlockSpec((tm, FOX_H * FOX_D), row),
            _resident((D_MODEL, D_MODEL)),
        ],
        out_specs=pl.BlockSpec((tm, D_MODEL), row),
        out_shape=jax.ShapeDtypeStruct((N_TOK, D_MODEL), F32),
        compiler_params=_params(("arbitrary",)),
        name="out_project",
    )(x2, mod4, o_mla, o_sb, o_fx, w_out)


def _ffn_body(x_ref, mod_ref, gn_ref, w1_ref, w2_ref, gf_ref, o_ref, h_ref, acc_ref, *, final):
    j = pl.program_id(1)
    mod = mod_ref[0, 0]

    @pl.when(j == 0)
    def _():
        h = _rms(x_ref[...], gn_ref[...]) * (1.0 + mod[4:5]) + mod[3:4]
        h_ref[...] = h.astype(BF16)
        acc_ref[...] = jnp.zeros(acc_ref.shape, F32)

    a = jnp.maximum(_dot(h_ref[...], w1_ref[...]), 0.0)
    acc_ref[...] += _dot((a * a).astype(BF16), w2_ref[...])

    @pl.when(j == pl.num_programs(1) - 1)
    def _():
        y = x_ref[...] + mod[5:6] * acc_ref[...]
        o_ref[...] = _rms(y, gf_ref[...]) if final else y


def _ffn(x2, mod4, layer, gn, w1, w2, g_final, final):
    tm, tf = FFN_TM, FFN_TF
    tiles_per_seq = SEQ_LEN // tm
    return pl.pallas_call(
        functools.partial(_ffn_body, final=final),
        grid=(N_TOK // tm, FF_DIM // tf),
        in_specs=[
            pl.BlockSpec((tm, D_MODEL), lambda i, j: (i, 0)),
            pl.BlockSpec((1, 1, 6, D_MODEL), lambda i, j: (layer, i // tiles_per_seq, 0, 0)),
            pl.BlockSpec((1, D_MODEL), lambda i, j: (0, 0)),
            pl.BlockSpec((D_MODEL, tf), lambda i, j: (0, j)),
            pl.BlockSpec((tf, D_MODEL), lambda i, j: (j, 0)),
            pl.BlockSpec((1, D_MODEL), lambda i, j: (0, 0)),
        ],
        out_specs=pl.BlockSpec((tm, D_MODEL), lambda i, j: (i, 0)),
        out_shape=jax.ShapeDtypeStruct((N_TOK, D_MODEL), F32),
        scratch_shapes=[pltpu.VMEM((tm, D_MODEL), BF16), pltpu.VMEM((tm, D_MODEL), F32)],
        compiler_params=_params(("arbitrary", "arbitrary")),
        name="ffn",
    )(x2, mod4, gn, w1, w2, g_final)


def _swap_halves(w):
    half = w.shape[-1] // 2
    return jnp.concatenate([w[..., half:], w[..., :half]], axis=-1)


def _pack_w_in(w):
    o_kr = MLA_QL + MLA_KVL
    o_sb = o_kr + MLA_ROPE_D
    o_f = o_sb + ATT_W
    kr = w[:, o_kr:o_sb]
    kr_sw = _swap_halves(kr)
    f_pad = jnp.pad(w[:, o_f:], ((0, 0), (0, LANES - FOX_H)))
    return jnp.concatenate([w[:, :o_kr], kr, kr, kr_sw, kr_sw, f_pad, w[:, o_sb:o_f]], axis=1).astype(BF16)


def _pack_w_uq(w):
    w3 = w.reshape(MLA_QL, MLA_H, MLA_NOPE_D + MLA_ROPE_D)
    nope = w3[:, :, :MLA_NOPE_D].reshape(MLA_QL, -1)
    rope = w3[:, :, MLA_NOPE_D:]
    return jnp.concatenate(
        [nope, rope.reshape(MLA_QL, -1), _swap_halves(rope).reshape(MLA_QL, -1)], axis=1).astype(BF16)


def _pack_w_ukv(w):
    w3 = w.reshape(MLA_KVL, MLA_H, MLA_NOPE_D + MLA_V_D)
    return jnp.concatenate(
        [w3[:, :, :MLA_NOPE_D].reshape(MLA_KVL, -1), w3[:, :, MLA_NOPE_D:].reshape(MLA_KVL, -1)],
        axis=1).astype(BF16)


def _rope_tables():
    pos = jnp.arange(SEQ_LEN, dtype=F32)
    inv = ROPE_THETA ** (-jnp.arange(0, MLA_ROPE_D, 2, dtype=F32) / MLA_ROPE_D)
    ang = pos[:, None] * inv[None, :]
    cos, sin = jnp.cos(ang), jnp.sin(ang)
    cos_t = jnp.tile(jnp.concatenate([cos, cos], axis=1), (1, MLA_H))
    sin_t = jnp.tile(jnp.concatenate([-sin, sin], axis=1), (1, MLA_H))
    return cos_t, sin_t


def kernel(x, c, w_ada, b_ada, norm_mix, w_in, q_norm, w_uq, kv_norm, w_ukv, b_forget, out_norm,
           w_out, norm_ffn, w_ff1, w_ff2, final_norm):
    assert x.shape == (N_BATCH, SEQ_LEN, D_MODEL) and c.shape == (N_BATCH, D_MODEL)
    cos_t, sin_t = _rope_tables()
    c_pad = jnp.pad(c, ((0, SUBLANES - N_BATCH), (0, 0)))
    mod = _ada_mod(c_pad, w_ada, b_ada)
    mod4 = mod[:, :N_BATCH].reshape(N_LAYERS, N_BATCH, 6, D_MODEL)

    n_mla = MLA_H * MLA_V_D
    n_sb = SB_H * SB_D
    row = lambda v: v.reshape(1, -1)
    x2 = x.reshape(N_TOK, D_MODEL)
    for l in range(N_LAYERS):
        bfp = jnp.pad(b_forget[l], (0, LANES - FOX_H)).reshape(1, LANES)
        q_mla, kv_mla, krope, p_att, logf = _project(
            x2, mod4, l, row(norm_mix[l]), _pack_w_in(w_in[l]), row(q_norm[l]), _pack_w_uq(w_uq[l]),
            row(kv_norm[l]), _pack_w_ukv(w_ukv[l]), bfp, cos_t, sin_t)
        qaug, kaug = _fgate(logf)
        g = out_norm[l]
        o_mla = _mla_attention(q_mla, kv_mla, krope, row(g[:n_mla]))
        o_sb = _sb_attention(p_att, row(g[n_mla:n_mla + n_sb]))
        o_fx = _fox_attention(p_att, qaug, kaug, row(g[n_mla + n_sb:]))
        x2 = _out_project(x2, mod4, l, o_mla, o_sb, o_fx, w_out[l].astype(BF16))
        x2 = _ffn(x2, mod4, l, row(norm_ffn[l]), w_ff1[l].astype(BF16), w_ff2[l].astype(BF16),
                  row(final_norm), final=(l == N_LAYERS - 1))
    return x2.reshape(N_BATCH, SEQ_LEN, D_MODEL)
```

```python
import functools

import numpy as np
import jax
import jax.numpy as jnp
from jax import lax
from jax.experimental import pallas as pl
from jax.experimental.pallas import tpu as pltpu

F32 = jnp.float32
BF16 = jnp.bfloat16

D_MODEL = 2048
N_BATCH = 4
SEQ_LEN = 2048
N_LAYERS = 2
CHUNK_LEN = 64
NORM_EPS = 1e-6
LOG2_E = 1.4426950408889634
MLA_H = 8
MLA_QL = 512
MLA_KVL = 256
MLA_NOPE_D = 128
MLA_ROPE_D = 64
MLA_V_D = 128
ROPE_THETA = 10000.0
SB_H = 8
SB_D = 64
FOX_H = 8
FOX_D = 64
FF_DIM = 4 * D_MODEL
N_TOK = N_BATCH * SEQ_LEN

LANES = 128
SUBLANES = 8
VMEM_LIMIT = 56 * 1024 * 1024

C_CQ = 0
C_CKV = C_CQ + MLA_QL
C_KR = C_CKV + MLA_KVL
C_F = C_KR + 2 * LANES
C_ATT = C_F + LANES
ATT_W = 3 * SB_H * SB_D + 3 * FOX_H * FOX_D
IN_W = C_ATT + ATT_W
HEAD_GRP = SB_H * SB_D

PROJ_TM = 512
FFN_TM = 1024
FFN_TF = 512
ADA_TN = 1024
ATT_T = 256
N_QB = SEQ_LEN // ATT_T
CUM_T = 256


def _dot(a, b):
    return jnp.dot(a, b, preferred_element_type=F32)


def _dot_nt(a, b):
    return lax.dot_general(a, b, (((1,), (1,)), ((), ())), preferred_element_type=F32)


def _split3(x):
    hi = x.astype(BF16)
    r1 = x - hi.astype(F32)
    mid = r1.astype(BF16)
    lo = (r1 - mid.astype(F32)).astype(BF16)
    return hi, mid, lo


def _softplus(z):
    return jnp.maximum(z, 0.0) + jnp.log(1.0 + jnp.exp(-jnp.abs(z)))


def _rms(x, gain):
    return x * lax.rsqrt(jnp.mean(x * x, axis=-1, keepdims=True) + NORM_EPS) * gain


def _resident(shape):
    nd = len(shape)
    return pl.BlockSpec(shape, lambda *_: (0,) * nd, pipeline_mode=pl.Buffered(1))


def _layer_resident(shape, layer):
    nd = len(shape)
    return pl.BlockSpec((1,) + shape, lambda *_: (layer,) + (0,) * nd, pipeline_mode=pl.Buffered(1))


def _params(sem):
    return pltpu.CompilerParams(dimension_semantics=sem, vmem_limit_bytes=VMEM_LIMIT)


def _ada_body(c_ref, w_ref, b_ref, o_ref):
    c = c_ref[...]
    c_act = (c * jax.nn.sigmoid(c)).astype(BF16)
    o_ref[0] = _dot(c_act, w_ref[0].astype(BF16)) + b_ref[0]


def _ada_mod(c_pad, w_ada, b_ada):
    n_out = 6 * D_MODEL
    return pl.pallas_call(
        _ada_body,
        grid=(N_LAYERS, n_out // ADA_TN),
        in_specs=[
            pl.BlockSpec((SUBLANES, D_MODEL), lambda l, j: (0, 0)),
            pl.BlockSpec((1, D_MODEL, ADA_TN), lambda l, j: (l, 0, j)),
            pl.BlockSpec((1, 1, ADA_TN), lambda l, j: (l, 0, j)),
        ],
        out_specs=pl.BlockSpec((1, SUBLANES, ADA_TN), lambda l, j: (l, 0, j)),
        out_shape=jax.ShapeDtypeStruct((N_LAYERS, SUBLANES, n_out), F32),
        compiler_params=_params(("arbitrary", "arbitrary")),
        name="ada_mod",
    )(c_pad, w_ada, b_ada.reshape(N_LAYERS, 1, n_out))


def _proj_body(x_ref, mod_ref, gn_ref, win_ref, qn_ref, wuq_ref, kvn_ref, wukv_ref, bf_ref,
               cos_ref, sin_ref, q_out, kv_out, kr_out, att_out, lf_out, h_ref):
    mod = mod_ref[0, 0]
    x = x_ref[...]
    h = _rms(x, gn_ref[0]) * (1.0 + mod[1:2]) + mod[0:1]
    h_ref[...] = h.astype(BF16)

    def proj(c0, width):
        return _dot(h_ref[...], win_ref[0, :, c0:c0 + width])

    mla_scale = (MLA_NOPE_D + MLA_ROPE_D) ** -0.5
    n_nope = MLA_H * MLA_NOPE_D
    n_rope = MLA_H * MLA_ROPE_D

    cq = _rms(proj(C_CQ, MLA_QL), qn_ref[0]).astype(BF16)
    q = _dot(cq, wuq_ref[0])
    q_out[:, :n_nope] = (q[:, :n_nope] * mla_scale).astype(BF16)
    q_rot = q[:, n_nope:n_nope + n_rope] * cos_ref[...] + q[:, n_nope + n_rope:] * sin_ref[...]
    q_out[:, n_nope:] = (q_rot * mla_scale).astype(BF16)

    ckv = _rms(proj(C_CKV, MLA_KVL), kvn_ref[0]).astype(BF16)
    kv_out[...] = _dot(ckv, wukv_ref[0]).astype(BF16)

    kr = proj(C_KR, 2 * LANES)
    kr_out[...] = (kr[:, :LANES] * cos_ref[:, :LANES] + kr[:, LANES:] * sin_ref[:, :LANES]).astype(BF16)

    lf_out[...] = -_softplus(-(proj(C_F, LANES) + bf_ref[0]))

    sb_scale = SB_D ** -0.5 * LOG2_E
    fox_scale = FOX_D ** -0.5
    scales = (sb_scale, 1.0, 1.0, fox_scale, 1.0, 1.0)
    for g, sc in enumerate(scales):
        blk = proj(C_ATT + g * HEAD_GRP, HEAD_GRP)
        if sc != 1.0:
            blk = blk * sc
        att_out[:, g * HEAD_GRP:(g + 1) * HEAD_GRP] = blk.astype(BF16)


def _project(x2, mod4, layer, gn, win, qn, wuq, kvn, wukv, bfp, cos_t, sin_t):
    tm = PROJ_TM
    tiles_per_seq = SEQ_LEN // tm
    q_w = MLA_H * (MLA_NOPE_D + MLA_ROPE_D)
    kv_w = MLA_H * (MLA_NOPE_D + MLA_V_D)
    rope_w = MLA_H * MLA_ROPE_D
    row = lambda i: (i, 0)
    return pl.pallas_call(
        _proj_body,
        grid=(N_TOK // tm,),
        in_specs=[
            pl.BlockSpec((tm, D_MODEL), row),
            pl.BlockSpec((1, 1, 6, D_MODEL), lambda i: (layer, i // tiles_per_seq, 0, 0)),
            _layer_resident((1, D_MODEL), layer),
            _layer_resident((D_MODEL, IN_W), layer),
            _layer_resident((1, MLA_QL), layer),
            _layer_resident((MLA_QL, q_w + rope_w), layer),
            _layer_resident((1, MLA_KVL), layer),
            _layer_resident((MLA_KVL, kv_w), layer),
            _layer_resident((1, LANES), layer),
            pl.BlockSpec((tm, rope_w), lambda i: (i % tiles_per_seq, 0)),
            pl.BlockSpec((tm, rope_w), lambda i: (i % tiles_per_seq, 0)),
        ],
        out_specs=[
            pl.BlockSpec((tm, q_w), row),
            pl.BlockSpec((tm, kv_w), row),
            pl.BlockSpec((tm, LANES), row),
            pl.BlockSpec((tm, ATT_W), row),
            pl.BlockSpec((tm, LANES), row),
        ],
        out_shape=[
            jax.ShapeDtypeStruct((N_TOK, q_w), BF16),
            jax.ShapeDtypeStruct((N_TOK, kv_w), BF16),
            jax.ShapeDtypeStruct((N_TOK, LANES), BF16),
            jax.ShapeDtypeStruct((N_TOK, ATT_W), BF16),
            jax.ShapeDtypeStruct((N_TOK, LANES), F32),
        ],
        scratch_shapes=[pltpu.VMEM((tm, D_MODEL), BF16)],
        compiler_params=_params(("arbitrary",)),
        name="project",
    )(x2, mod4, gn, win, qn, wuq, kvn, wukv, bfp, cos_t, sin_t)


def _fgate_body(lf_ref, tri_ref, pq_ref, pk_ref, cq_ref, ck_ref, qa_out, ka_out):
    carry = jnp.zeros((1, LANES), F32)
    for blk in range(SEQ_LEN // CUM_T):
        rows = slice(blk * CUM_T, (blk + 1) * CUM_T)
        hi, mid, lo = _split3(lf_ref[rows, :])
        tri = tri_ref[...]
        f_blk = _dot(tri, hi) + _dot(tri, mid) + _dot(tri, lo) + carry
        carry = f_blk[CUM_T - 1:CUM_T, :]
        f_hi, f_mid, f_lo = _split3(f_blk)
        qa = _dot(f_hi, pq_ref[0]) + _dot(f_mid, pq_ref[1]) + _dot(f_lo, pq_ref[2]) + cq_ref[...]
        ka = ck_ref[...] - (_dot(f_hi, pk_ref[0]) + _dot(f_mid, pk_ref[1]) + _dot(f_lo, pk_ref[2]))
        qa_out[rows, :] = qa.astype(BF16)
        ka_out[rows, :] = ka.astype(BF16)


def _fgate_constants():
    tri = np.tril(np.ones((CUM_T, CUM_T), np.float32))
    n_pair = FOX_H // 2
    pq = np.zeros((3, LANES, n_pair * LANES), np.float32)
    pk = np.zeros((3, LANES, n_pair * LANES), np.float32)
    cq = np.zeros((1, n_pair * LANES), np.float32)
    ck = np.zeros((1, n_pair * LANES), np.float32)
    for h in range(FOX_H):
        base = (h // 2) * LANES + (h % 2) * SUBLANES
        for t in range(3):
            pq[t, h, base + t] = 1.0
            pk[t, h, base + 3 + t] = 1.0
            cq[0, base + 3 + t] = 1.0
            ck[0, base + t] = 1.0
    as_bf16 = lambda a: jnp.asarray(a, BF16)
    return as_bf16(tri), as_bf16(pq), as_bf16(pk), jnp.asarray(cq), jnp.asarray(ck)


def _fgate(logf):
    tri, pq, pk, cq, ck = _fgate_constants()
    aug_w = (FOX_H // 2) * LANES
    seq_blk = lambda b: (b, 0)
    return pl.pallas_call(
        _fgate_body,
        grid=(N_BATCH,),
        in_specs=[
            pl.BlockSpec((SEQ_LEN, LANES), seq_blk),
            _resident((CUM_T, CUM_T)),
            _resident((3, LANES, aug_w)),
            _resident((3, LANES, aug_w)),
            _resident((1, aug_w)),
            _resident((1, aug_w)),
        ],
        out_specs=[pl.BlockSpec((SEQ_LEN, aug_w), seq_blk), pl.BlockSpec((SEQ_LEN, aug_w), seq_blk)],
        out_shape=[jax.ShapeDtypeStruct((N_TOK, aug_w), BF16), jax.ShapeDtypeStruct((N_TOK, aug_w), BF16)],
        compiler_params=_params(("arbitrary",)),
        name="fgate",
    )(logf, tri, pq, pk, cq, ck)


def _lane_band(shape, lo, width):
    lane = lax.broadcasted_iota(jnp.int32, shape, len(shape) - 1)
    return (lane >= lo) & (lane < lo + width)


def _rel_positions():
    rows = lax.broadcasted_iota(jnp.int32, (ATT_T, ATT_T), 0)
    cols = lax.broadcasted_iota(jnp.int32, (ATT_T, ATT_T), 1)
    return rows, cols


def _col_block(a, kb):
    return a[:, kb * ATT_T:(kb + 1) * ATT_T]


def _weights_times_v(p_ref, hh, nkb, v):
    p = jnp.concatenate([p_ref[hh, kb] for kb in range(nkb)], axis=1) if nkb > 1 else p_ref[hh, 0]
    return _dot(p, v)


def _softmax_attention(qcat_at, kcat_at, vaug_at, mask_fn, finish, s_ref, p_ref):
    def scores(i):
        nkb = i + 1
        row_max = []
        for hh in range(2):
            s = _dot_nt(qcat_at(i, hh), kcat_at(hh, nkb * ATT_T))
            blocks = [_col_block(s, kb) for kb in range(nkb)]
            blocks[-1] = jnp.where(mask_fn(*_rel_positions()), blocks[-1], -jnp.inf)
            m = None
            for kb in range(nkb):
                s_ref[i % 2, hh, kb] = blocks[kb]
                bm = jnp.maximum(blocks[kb][:, :LANES], blocks[kb][:, LANES:])
                m = bm if m is None else jnp.maximum(m, bm)
            row_max.append(jnp.max(m, axis=-1, keepdims=True))
        return row_max

    row_max = scores(0)
    for i in range(N_QB):
        nkb = i + 1
        next_max = scores(i + 1) if i + 1 < N_QB else None
        for kb in range(nkb):
            for hh in range(2):
                p_ref[i % 2, hh, kb] = jnp.exp(s_ref[i % 2, hh, kb] - row_max[hh]).astype(BF16)
        finish(i, [_weights_times_v(p_ref.at[i % 2], hh, nkb, vaug_at(hh, nkb * ATT_T)) for hh in range(2)])
        row_max = next_max


def _pair_norm(o, gain, head_d):
    first = _lane_band(o.shape, 0, head_d)
    sq = o * o
    ss_a = jnp.sum(jnp.where(first, sq, 0.0), axis=-1, keepdims=True)
    ss_b = jnp.sum(sq, axis=-1, keepdims=True) - ss_a
    inv = jnp.where(first, lax.rsqrt(ss_a / head_d + NORM_EPS), lax.rsqrt(ss_b / head_d + NORM_EPS))
    return o * inv * gain


_SCORE_SCRATCH = [
    pltpu.VMEM((2, 2, N_QB, ATT_T, ATT_T), F32),
    pltpu.VMEM((2, 2, N_QB, ATT_T, ATT_T), BF16),
]


def _mla_body(qn_ref, qr_ref, kn_ref, v_ref, kr_ref, g_ref, o_ref, kcat_ref, vaug_ref, s_ref, p_ref):
    d = MLA_NOPE_D
    for hh in range(2):
        kcat_ref[hh, :, :d] = kn_ref[:, hh * d:(hh + 1) * d]
        kcat_ref[hh, :, d:] = kr_ref[...]
        vaug_ref[hh, :, :d] = v_ref[:, hh * d:(hh + 1) * d]
        vaug_ref[hh, :, d:] = jnp.ones((SEQ_LEN, LANES), BF16)

    def chunk_mask(rows, cols):
        return cols < ((rows >> 6) + 1) * CHUNK_LEN

    def qcat_at(i, hh):
        r = slice(i * ATT_T, (i + 1) * ATT_T)
        qr = qr_ref[r, :]
        qr = jnp.where(_lane_band(qr.shape, hh * MLA_ROPE_D, MLA_ROPE_D), qr, jnp.zeros_like(qr))
        return jnp.concatenate([qn_ref[r, hh * d:(hh + 1) * d], qr], axis=1)

    def finish(i, outs):
        r = slice(i * ATT_T, (i + 1) * ATT_T)
        for hh in range(2):
            o = outs[hh][:, :d] / outs[hh][:, d:]
            o_ref[r, hh * d:(hh + 1) * d] = _rms(o, g_ref[0, :, hh * d:(hh + 1) * d]).astype(BF16)

    _softmax_attention(
        qcat_at, lambda hh, klen: kcat_ref[hh, :klen, :], lambda hh, klen: vaug_ref[hh, :klen, :],
        chunk_mask, finish, s_ref, p_ref)


def _mla_attention(q_mla, kv_mla, krope, out_gain, layer):
    n_pair = MLA_H // 2
    pair_w = 2 * MLA_NOPE_D
    return pl.pallas_call(
        _mla_body,
        grid=(N_BATCH, n_pair),
        in_specs=[
            pl.BlockSpec((SEQ_LEN, pair_w), lambda b, p: (b, p)),
            pl.BlockSpec((SEQ_LEN, LANES), lambda b, p: (b, MLA_H * MLA_NOPE_D // LANES + p)),
            pl.BlockSpec((SEQ_LEN, pair_w), lambda b, p: (b, p)),
            pl.BlockSpec((SEQ_LEN, pair_w), lambda b, p: (b, n_pair + p)),
            pl.BlockSpec((SEQ_LEN, LANES), lambda b, p: (b, 0)),
            pl.BlockSpec((1, 1, pair_w), lambda b, p: (layer, 0, p)),
        ],
        out_specs=pl.BlockSpec((SEQ_LEN, pair_w), lambda b, p: (b, p)),
        out_shape=jax.ShapeDtypeStruct((N_TOK, MLA_H * MLA_V_D), BF16),
        scratch_shapes=[
            pltpu.VMEM((2, SEQ_LEN, 2 * LANES), BF16),
            pltpu.VMEM((2, SEQ_LEN, 2 * LANES), BF16),
        ] + _SCORE_SCRATCH,
        compiler_params=_params(("arbitrary", "arbitrary")),
        name="mla_attention",
    )(q_mla, q_mla, kv_mla, kv_mla, krope, out_gain)


def _fox_body(q_ref, k_ref, v_ref, qa_ref, ka_ref, g_ref, o_ref, kcat_ref, vaug_ref, s_ref, p_ref):
    kcat_ref[:, :LANES] = k_ref[...]
    kcat_ref[:, LANES:] = ka_ref[...]
    vaug_ref[:, :LANES] = v_ref[...]
    vaug_ref[:, LANES:] = jnp.ones((SEQ_LEN, LANES), BF16)

    def causal(rows, cols):
        return cols <= rows

    def qcat_at(i, hh):
        r = slice(i * ATT_T, (i + 1) * ATT_T)
        q = q_ref[r, :]
        qa = qa_ref[r, :]
        zero = jnp.zeros_like(q)
        return jnp.concatenate([
            jnp.where(_lane_band(q.shape, hh * FOX_D, FOX_D), q, zero),
            jnp.where(_lane_band(qa.shape, hh * SUBLANES, SUBLANES), qa, zero)], axis=1)

    def finish(i, outs):
        heads = [outs[hh][:, :LANES] / outs[hh][:, LANES:] for hh in range(2)]
        o = jnp.where(_lane_band(heads[0].shape, 0, FOX_D), heads[0], heads[1])
        o_ref[i * ATT_T:(i + 1) * ATT_T, :] = _pair_norm(o, g_ref[0], FOX_D).astype(BF16)

    _softmax_attention(
        qcat_at, lambda hh, klen: kcat_ref[:klen, :], lambda hh, klen: vaug_ref[:klen, :],
        causal, finish, s_ref, p_ref)


def _fox_attention(p_att, qaug, kaug, out_gain, layer):
    n_pair = FOX_H // 2
    grp = HEAD_GRP // LANES
    gain_blk = (MLA_H * MLA_V_D + SB_H * SB_D) // LANES
    return pl.pallas_call(
        _fox_body,
        grid=(N_BATCH, n_pair),
        in_specs=[
            pl.BlockSpec((SEQ_LEN, LANES), lambda b, p: (b, 3 * grp + p)),
            pl.BlockSpec((SEQ_LEN, LANES), lambda b, p: (b, 4 * grp + p)),
            pl.BlockSpec((SEQ_LEN, LANES), lambda b, p: (b, 5 * grp + p)),
            pl.BlockSpec((SEQ_LEN, LANES), lambda b, p: (b, p)),
            pl.BlockSpec((SEQ_LEN, LANES), lambda b, p: (b, p)),
            pl.BlockSpec((1, 1, LANES), lambda b, p: (layer, 0, gain_blk + p)),
        ],
        out_specs=pl.BlockSpec((SEQ_LEN, LANES), lambda b, p: (b, p)),
        out_shape=jax.ShapeDtypeStruct((N_TOK, FOX_H * FOX_D), BF16),
        scratch_shapes=[
            pltpu.VMEM((SEQ_LEN, 2 * LANES), BF16),
            pltpu.VMEM((SEQ_LEN, 2 * LANES), BF16),
        ] + _SCORE_SCRATCH,
        compiler_params=_params(("arbitrary", "arbitrary")),
        name="fox_attention",
    )(p_att, p_att, p_att, qaug, kaug, out_gain)


def _sb_body(q_ref, k_ref, v_ref, tri_ref, g_ref, o_ref, u_ref, p_ref):
    def scores(i):
        nkb = i + 1
        q = q_ref[i * ATT_T:(i + 1) * ATT_T, :]
        zero = jnp.zeros_like(q)
        for hh in range(2):
            z = _dot_nt(jnp.where(_lane_band(q.shape, hh * SB_D, SB_D), q, zero), k_ref[:nkb * ATT_T, :])
            for kb in range(nkb):
                u_ref[i % 2, hh, kb] = _col_block(z, kb)

    def weights(i):
        nkb = i + 1
        right = [None, None]
        for kb in reversed(range(nkb)):
            on_diagonal = kb == nkb - 1
            for hh in range(2):
                u = u_ref[i % 2, hh, kb]
                neg_u = -u
                keep = jnp.minimum(neg_u, 0.0) - jnp.log2(1.0 + jnp.exp2(jnp.minimum(u, neg_u)))
                if on_diagonal:
                    rows, cols = _rel_positions()
                    strict = cols < rows
                    keep = jnp.where(strict, keep, 0.0)
                hi = keep.astype(BF16)
                lo = (keep - hi.astype(F32)).astype(BF16)
                log_w = u + _dot(jnp.concatenate([hi, lo], axis=1), tri_ref[...])
                if right[hh] is not None:
                    log_w = log_w + right[hh]
                w = jnp.exp2(log_w)
                if on_diagonal:
                    w = jnp.where(strict, w, 0.0)
                p_ref[i % 2, hh, kb] = w.astype(BF16)
                total = jnp.sum(keep, axis=-1, keepdims=True)
                right[hh] = total if right[hh] is None else right[hh] + total

    scores(0)
    for i in range(N_QB):
        nkb = i + 1
        if i + 1 < N_QB:
            scores(i + 1)
        weights(i)
        heads = [_weights_times_v(p_ref.at[i % 2], hh, nkb, v_ref[:nkb * ATT_T, :]) for hh in range(2)]
        o = jnp.where(_lane_band(heads[0].shape, 0, SB_D), heads[0], heads[1])
        o_ref[i * ATT_T:(i + 1) * ATT_T, :] = _pair_norm(o, g_ref[0], SB_D).astype(BF16)


def _sb_attention(p_att, out_gain, layer):
    n_pair = SB_H // 2
    grp = HEAD_GRP // LANES
    gain_blk = MLA_H * MLA_V_D // LANES
    tri = np.tril(np.ones((ATT_T, ATT_T), np.float32))
    tri = jnp.asarray(np.concatenate([tri, tri], axis=0), BF16)
    return pl.pallas_call(
        _sb_body,
        grid=(N_BATCH, n_pair),
        in_specs=[
            pl.BlockSpec((SEQ_LEN, LANES), lambda b, p: (b, p)),
            pl.BlockSpec((SEQ_LEN, LANES), lambda b, p: (b, grp + p)),
            pl.BlockSpec((SEQ_LEN, LANES), lambda b, p: (b, 2 * grp + p)),
            _resident((2 * ATT_T, ATT_T)),
            pl.BlockSpec((1, 1, LANES), lambda b, p: (layer, 0, gain_blk + p)),
        ],
        out_specs=pl.BlockSpec((SEQ_LEN, LANES), lambda b, p: (b, p)),
        out_shape=jax.ShapeDtypeStruct((N_TOK, SB_H * SB_D), BF16),
        scratch_shapes=_SCORE_SCRATCH,
        compiler_params=_params(("arbitrary", "arbitrary")),
        name="sb_attention",
    )(p_att, p_att, p_att, tri, out_gain)


def _outproj_body(x_ref, mod_ref, om_ref, os_ref, of_ref, w_ref, o_ref):
    n_mla = MLA_H * MLA_V_D
    n_sb = SB_H * SB_D
    y = (_dot(om_ref[...], w_ref[0, :n_mla, :]) + _dot(os_ref[...], w_ref[0, n_mla:n_mla + n_sb, :])
         + _dot(of_ref[...], w_ref[0, n_mla + n_sb:, :]))
    o_ref[...] = x_ref[...] + mod_ref[0, 0][2:3] * y


def _out_project(x2, mod4, layer, o_mla, o_sb, o_fx, w_out):
    tm = PROJ_TM
    tiles_per_seq = SEQ_LEN // tm
    row = lambda i: (i, 0)
    return pl.pallas_call(
        _outproj_body,
        grid=(N_TOK // tm,),
        in_specs=[
            pl.BlockSpec((tm, D_MODEL), row),
            pl.BlockSpec((1, 1, 6, D_MODEL), lambda i: (layer, i // tiles_per_seq, 0, 0)),
            pl.BlockSpec((tm, MLA_H * MLA_V_D), row),
            pl.BlockSpec((tm, SB_H * SB_D), row),
            pl.BlockSpec((tm, FOX_H * FOX_D), row),
            _layer_resident((D_MODEL, D_MODEL), layer),
        ],
        out_specs=pl.BlockSpec((tm, D_MODEL), row),
        out_shape=jax.ShapeDtypeStruct((N_TOK, D_MODEL), F32),
        compiler_params=_params(("arbitrary",)),
        name="out_project",
    )(x2, mod4, o_mla, o_sb, o_fx, w_out)


def _ffn_body(x_ref, mod_ref, gn_ref, w1_ref, w2_ref, gf_ref, o_ref, h_ref, *, final):
    j = pl.program_id(1)
    mod = mod_ref[0, 0]

    @pl.when(j == 0)
    def _():
        h = _rms(x_ref[...], gn_ref[0]) * (1.0 + mod[4:5]) + mod[3:4]
        h_ref[...] = h.astype(BF16)
        o_ref[...] = jnp.zeros(o_ref.shape, F32)

    a = jnp.maximum(_dot(h_ref[...], w1_ref[0]), 0.0)
    o_ref[...] += _dot((a * a).astype(BF16), w2_ref[0])

    @pl.when(j == pl.num_programs(1) - 1)
    def _():
        y = x_ref[...] + mod[5:6] * o_ref[...]
        o_ref[...] = _rms(y, gf_ref[...]) if final else y


def _ffn(x2, mod4, layer, gn, w1, w2, g_final, final):
    tm, tf = FFN_TM, FFN_TF
    tiles_per_seq = SEQ_LEN // tm
    return pl.pallas_call(
        functools.partial(_ffn_body, final=final),
        grid=(N_TOK // tm, FF_DIM // tf),
        in_specs=[
            pl.BlockSpec((tm, D_MODEL), lambda i, j: (i, 0)),
            pl.BlockSpec((1, 1, 6, D_MODEL), lambda i, j: (layer, i // tiles_per_seq, 0, 0)),
            pl.BlockSpec((1, 1, D_MODEL), lambda i, j: (layer, 0, 0)),
            pl.BlockSpec((1, D_MODEL, tf), lambda i, j: (layer, 0, j)),
            pl.BlockSpec((1, tf, D_MODEL), lambda i, j: (layer, j, 0)),
            pl.BlockSpec((1, D_MODEL), lambda i, j: (0, 0)),
        ],
        out_specs=pl.BlockSpec((tm, D_MODEL), lambda i, j: (i, 0)),
        out_shape=jax.ShapeDtypeStruct((N_TOK, D_MODEL), F32),
        scratch_shapes=[pltpu.VMEM((tm, D_MODEL), BF16)],
        compiler_params=_params(("arbitrary", "arbitrary")),
        name="ffn",
    )(x2, mod4, gn, w1, w2, g_final)


def _swap_halves(w):
    half = w.shape[-1] // 2
    return jnp.concatenate([w[..., half:], w[..., :half]], axis=-1)


def _pack_w_in(w):
    o_kr = MLA_QL + MLA_KVL
    o_sb = o_kr + MLA_ROPE_D
    o_f = o_sb + ATT_W
    w = w.astype(BF16)
    kr = w[..., o_kr:o_sb]
    kr_sw = _swap_halves(kr)
    f_pad = jnp.pad(w[..., o_f:], ((0, 0), (0, 0), (0, LANES - FOX_H)))
    return jnp.concatenate([w[..., :o_kr], kr, kr, kr_sw, kr_sw, f_pad, w[..., o_sb:o_f]], axis=-1)


def _pack_w_uq(w):
    w4 = w.astype(BF16).reshape(N_LAYERS, MLA_QL, MLA_H, MLA_NOPE_D + MLA_ROPE_D)
    flat = lambda a: a.reshape(N_LAYERS, MLA_QL, -1)
    rope = w4[..., MLA_NOPE_D:]
    return jnp.concatenate([flat(w4[..., :MLA_NOPE_D]), flat(rope), flat(_swap_halves(rope))], axis=-1)


def _pack_w_ukv(w):
    w4 = w.astype(BF16).reshape(N_LAYERS, MLA_KVL, MLA_H, MLA_NOPE_D + MLA_V_D)
    flat = lambda a: a.reshape(N_LAYERS, MLA_KVL, -1)
    return jnp.concatenate([flat(w4[..., :MLA_NOPE_D]), flat(w4[..., MLA_NOPE_D:])], axis=-1)


def _rope_tables():
    pos = jnp.arange(SEQ_LEN, dtype=F32)
    inv = ROPE_THETA ** (-jnp.arange(0, MLA_ROPE_D, 2, dtype=F32) / MLA_ROPE_D)
    ang = pos[:, None] * inv[None, :]
    cos, sin = jnp.cos(ang), jnp.sin(ang)
    cos_t = jnp.tile(jnp.concatenate([cos, cos], axis=1), (1, MLA_H))
    sin_t = jnp.tile(jnp.concatenate([-sin, sin], axis=1), (1, MLA_H))
    return cos_t, sin_t


def kernel(x, c, w_ada, b_ada, norm_mix, w_in, q_norm, w_uq, kv_norm, w_ukv, b_forget, out_norm,
           w_out, norm_ffn, w_ff1, w_ff2, final_norm):
    assert x.shape == (N_BATCH, SEQ_LEN, D_MODEL) and c.shape == (N_BATCH, D_MODEL)
    cos_t, sin_t = _rope_tables()
    c_pad = jnp.pad(c, ((0, SUBLANES - N_BATCH), (0, 0)))
    mod = _ada_mod(c_pad, w_ada, b_ada)
    mod4 = mod[:, :N_BATCH].reshape(N_LAYERS, N_BATCH, 6, D_MODEL)

    rows = lambda v: v.reshape(N_LAYERS, 1, -1)
    win_p, wuq_p, wukv_p = _pack_w_in(w_in), _pack_w_uq(w_uq), _pack_w_ukv(w_ukv)
    wout_b, wff1_b, wff2_b = w_out.astype(BF16), w_ff1.astype(BF16), w_ff2.astype(BF16)
    bf_pad = jnp.pad(b_forget, ((0, 0), (0, LANES - FOX_H))).reshape(N_LAYERS, 1, LANES)
    out_gain = rows(out_norm)

    x2 = x.reshape(N_TOK, D_MODEL)
    for l in range(N_LAYERS):
        q_mla, kv_mla, krope, p_att, logf = _project(
            x2, mod4, l, rows(norm_mix), win_p, rows(q_norm), wuq_p, rows(kv_norm), wukv_p, bf_pad,
            cos_t, sin_t)
        qaug, kaug = _fgate(logf)
        o_mla = _mla_attention(q_mla, kv_mla, krope, out_gain, l)
        o_sb = _sb_attention(p_att, out_gain, l)
        o_fx = _fox_attention(p_att, qaug, kaug, out_gain, l)
        x2 = _out_project(x2, mod4, l, o_mla, o_sb, o_fx, wout_b)
        x2 = _ffn(x2, mod4, l, rows(norm_ffn), wff1_b, wff2_b, final_norm.reshape(1, -1),
                  final=(l == N_LAYERS - 1))
    return x2.reshape(N_BATCH, SEQ_LEN, D_MODEL)
```

```python
import functools

import numpy as np
import jax
import jax.numpy as jnp
from jax import lax
from jax.experimental import pallas as pl
from jax.experimental.pallas import tpu as pltpu

F32 = jnp.float32
BF16 = jnp.bfloat16

D_MODEL = 2048
N_BATCH = 4
SEQ_LEN = 2048
N_LAYERS = 2
CHUNK_LEN = 64
NORM_EPS = 1e-6
LOG2_E = 1.4426950408889634
MLA_H = 8
MLA_QL = 512
MLA_KVL = 256
MLA_NOPE_D = 128
MLA_ROPE_D = 64
MLA_V_D = 128
ROPE_THETA = 10000.0
SB_H = 8
SB_D = 64
FOX_H = 8
FOX_D = 64
FF_DIM = 4 * D_MODEL
N_TOK = N_BATCH * SEQ_LEN

LANES = 128
SUBLANES = 8
VMEM_LIMIT = 56 * 1024 * 1024

C_CQ = 0
C_CKV = C_CQ + MLA_QL
C_KR = C_CKV + MLA_KVL
C_F = C_KR + 2 * LANES
LOW_W = C_F + LANES
ATT_W = 3 * SB_H * SB_D + 3 * FOX_H * FOX_D
ATT_SRC = MLA_QL + MLA_KVL + MLA_ROPE_D
IN_SRC_W = ATT_SRC + ATT_W + FOX_H
HEAD_GRP = SB_H * SB_D
PACK_TR = 256

PROJ_TM = 512
FFN_TM = 1024
FFN_TF = 512
ADA_TN = 1024
ATT_T = 256
N_QB = SEQ_LEN // ATT_T
CUM_T = 256


def _dot(a, b):
    return jnp.dot(a, b, preferred_element_type=F32)


def _dot_nt(a, b):
    return lax.dot_general(a, b, (((1,), (1,)), ((), ())), preferred_element_type=F32)


def _split3(x):
    hi = x.astype(BF16)
    r1 = x - hi.astype(F32)
    mid = r1.astype(BF16)
    lo = (r1 - mid.astype(F32)).astype(BF16)
    return hi, mid, lo


def _softplus(z):
    return jnp.maximum(z, 0.0) + jnp.log(1.0 + jnp.exp(-jnp.abs(z)))


def _rms(x, gain):
    return x * lax.rsqrt(jnp.mean(x * x, axis=-1, keepdims=True) + NORM_EPS) * gain


def _resident(shape):
    nd = len(shape)
    return pl.BlockSpec(shape, lambda *_: (0,) * nd, pipeline_mode=pl.Buffered(1))


def _layer_resident(shape, layer):
    nd = len(shape)
    return pl.BlockSpec((1,) + shape, lambda *_: (layer,) + (0,) * nd, pipeline_mode=pl.Buffered(1))


def _params(sem):
    return pltpu.CompilerParams(dimension_semantics=sem, vmem_limit_bytes=VMEM_LIMIT)


def _ada_body(c_ref, w_ref, b_ref, o_ref):
    c = c_ref[...]
    c_act = (c * jax.nn.sigmoid(c)).astype(BF16)
    o_ref[0] = _dot(c_act, w_ref[0].astype(BF16)) + b_ref[0]


def _ada_mod(c_pad, w_ada, b_ada):
    n_out = 6 * D_MODEL
    return pl.pallas_call(
        _ada_body,
        grid=(N_LAYERS, n_out // ADA_TN),
        in_specs=[
            pl.BlockSpec((SUBLANES, D_MODEL), lambda l, j: (0, 0)),
            pl.BlockSpec((1, D_MODEL, ADA_TN), lambda l, j: (l, 0, j)),
            pl.BlockSpec((1, 1, ADA_TN), lambda l, j: (l, 0, j)),
        ],
        out_specs=pl.BlockSpec((1, SUBLANES, ADA_TN), lambda l, j: (l, 0, j)),
        out_shape=jax.ShapeDtypeStruct((N_LAYERS, SUBLANES, n_out), F32),
        compiler_params=_params(("arbitrary", "arbitrary")),
        name="ada_mod",
    )(c_pad, w_ada, b_ada.reshape(N_LAYERS, 1, n_out))


def _proj_body(x_ref, mod_ref, gn_ref, win_ref, watt_ref, qn_ref, wuq_ref, kvn_ref, wukv_ref, bf_ref,
               cos_ref, sin_ref, q_out, kv_out, kr_out, att_out, lf_out, h_ref):
    mod = mod_ref[0, 0]
    x = x_ref[...]
    h = _rms(x, gn_ref[0]) * (1.0 + mod[1:2]) + mod[0:1]
    h_ref[...] = h.astype(BF16)

    def proj(c0, width):
        return _dot(h_ref[...], win_ref[0, :, c0:c0 + width])

    mla_scale = (MLA_NOPE_D + MLA_ROPE_D) ** -0.5
    n_nope = MLA_H * MLA_NOPE_D
    n_rope = MLA_H * MLA_ROPE_D

    cq = _rms(proj(C_CQ, MLA_QL), qn_ref[0]).astype(BF16)
    q = _dot(cq, wuq_ref[0])
    q_out[:, :n_nope] = (q[:, :n_nope] * mla_scale).astype(BF16)
    q_rot = q[:, n_nope:n_nope + n_rope] * cos_ref[...] + q[:, n_nope + n_rope:] * sin_ref[...]
    q_out[:, n_nope:] = (q_rot * mla_scale).astype(BF16)

    ckv = _rms(proj(C_CKV, MLA_KVL), kvn_ref[0]).astype(BF16)
    kv_out[...] = _dot(ckv, wukv_ref[0]).astype(BF16)

    kr = proj(C_KR, 2 * LANES)
    kr_out[...] = (kr[:, :LANES] * cos_ref[:, :LANES] + kr[:, LANES:] * sin_ref[:, :LANES]).astype(BF16)

    lf_out[...] = -_softplus(-(proj(C_F, LANES) + bf_ref[0]))

    sb_scale = SB_D ** -0.5 * LOG2_E
    fox_scale = FOX_D ** -0.5
    scales = (sb_scale, 1.0, 1.0, fox_scale, 1.0, 1.0)
    for g, sc in enumerate(scales):
        blk = _dot(h_ref[...], watt_ref[0, :, g * HEAD_GRP:(g + 1) * HEAD_GRP])
        if sc != 1.0:
            blk = blk * sc
        att_out[:, g * HEAD_GRP:(g + 1) * HEAD_GRP] = blk.astype(BF16)


def _project(x2, mod4, layer, gn, win, watt, qn, wuq, kvn, wukv, bfp, cos_t, sin_t):
    tm = PROJ_TM
    tiles_per_seq = SEQ_LEN // tm
    q_w = MLA_H * (MLA_NOPE_D + MLA_ROPE_D)
    kv_w = MLA_H * (MLA_NOPE_D + MLA_V_D)
    rope_w = MLA_H * MLA_ROPE_D
    row = lambda i: (i, 0)
    return pl.pallas_call(
        _proj_body,
        grid=(N_TOK // tm,),
        in_specs=[
            pl.BlockSpec((tm, D_MODEL), row),
            pl.BlockSpec((1, 1, 6, D_MODEL), lambda i: (layer, i // tiles_per_seq, 0, 0)),
            _layer_resident((1, D_MODEL), layer),
            _layer_resident((D_MODEL, LOW_W), layer),
            _layer_resident((D_MODEL, ATT_W), layer),
            _layer_resident((1, MLA_QL), layer),
            _layer_resident((MLA_QL, q_w + rope_w), layer),
            _layer_resident((1, MLA_KVL), layer),
            _layer_resident((MLA_KVL, kv_w), layer),
            _layer_resident((1, LANES), layer),
            pl.BlockSpec((tm, rope_w), lambda i: (i % tiles_per_seq, 0)),
            pl.BlockSpec((tm, rope_w), lambda i: (i % tiles_per_seq, 0)),
        ],
        out_specs=[
            pl.BlockSpec((tm, q_w), row),
            pl.BlockSpec((tm, kv_w), row),
            pl.BlockSpec((tm, LANES), row),
            pl.BlockSpec((tm, ATT_W), row),
            pl.BlockSpec((tm, LANES), row),
        ],
        out_shape=[
            jax.ShapeDtypeStruct((N_TOK, q_w), BF16),
            jax.ShapeDtypeStruct((N_TOK, kv_w), BF16),
            jax.ShapeDtypeStruct((N_TOK, LANES), BF16),
            jax.ShapeDtypeStruct((N_TOK, ATT_W), BF16),
            jax.ShapeDtypeStruct((N_TOK, LANES), F32),
        ],
        scratch_shapes=[pltpu.VMEM((tm, D_MODEL), BF16)],
        compiler_params=_params(("arbitrary",)),
        name="project",
    )(x2, mod4, gn, win, watt, qn, wuq, kvn, wukv, bfp, cos_t, sin_t)


def _fgate_body(lf_ref, tri_ref, pq_ref, pk_ref, cq_ref, ck_ref, qa_out, ka_out):
    n_blk = SEQ_LEN // CUM_T
    local = []
    for blk in range(n_blk):
        hi, mid, lo = _split3(lf_ref[blk * CUM_T:(blk + 1) * CUM_T, :])
        tri = tri_ref[...]
        local.append(_dot(tri, hi) + _dot(tri, mid) + _dot(tri, lo))
    carry = jnp.zeros((1, LANES), F32)
    for blk in range(n_blk):
        rows = slice(blk * CUM_T, (blk + 1) * CUM_T)
        f_blk = local[blk] + carry
        carry = f_blk[CUM_T - 1:CUM_T, :]
        f_hi, f_mid, f_lo = _split3(f_blk)
        qa = _dot(f_hi, pq_ref[0]) + _dot(f_mid, pq_ref[1]) + _dot(f_lo, pq_ref[2]) + cq_ref[...]
        ka = ck_ref[...] - (_dot(f_hi, pk_ref[0]) + _dot(f_mid, pk_ref[1]) + _dot(f_lo, pk_ref[2]))
        qa_out[rows, :] = qa.astype(BF16)
        ka_out[rows, :] = ka.astype(BF16)


def _fgate_constants():
    tri = np.tril(np.ones((CUM_T, CUM_T), np.float32))
    n_pair = FOX_H // 2
    pq = np.zeros((3, LANES, n_pair * LANES), np.float32)
    pk = np.zeros((3, LANES, n_pair * LANES), np.float32)
    cq = np.zeros((1, n_pair * LANES), np.float32)
    ck = np.zeros((1, n_pair * LANES), np.float32)
    for h in range(FOX_H):
        base = (h // 2) * LANES + (h % 2) * SUBLANES
        for t in range(3):
            pq[t, h, base + t] = 1.0
            pk[t, h, base + 3 + t] = 1.0
            cq[0, base + 3 + t] = 1.0
            ck[0, base + t] = 1.0
    as_bf16 = lambda a: jnp.asarray(a, BF16)
    return as_bf16(tri), as_bf16(pq), as_bf16(pk), jnp.asarray(cq), jnp.asarray(ck)


def _fgate(logf):
    tri, pq, pk, cq, ck = _fgate_constants()
    aug_w = (FOX_H // 2) * LANES
    seq_blk = lambda b: (b, 0)
    return pl.pallas_call(
        _fgate_body,
        grid=(N_BATCH,),
        in_specs=[
            pl.BlockSpec((SEQ_LEN, LANES), seq_blk),
            _resident((CUM_T, CUM_T)),
            _resident((3, LANES, aug_w)),
            _resident((3, LANES, aug_w)),
            _resident((1, aug_w)),
            _resident((1, aug_w)),
        ],
        out_specs=[pl.BlockSpec((SEQ_LEN, aug_w), seq_blk), pl.BlockSpec((SEQ_LEN, aug_w), seq_blk)],
        out_shape=[jax.ShapeDtypeStruct((N_TOK, aug_w), BF16), jax.ShapeDtypeStruct((N_TOK, aug_w), BF16)],
        compiler_params=_params(("arbitrary",)),
        name="fgate",
    )(logf, tri, pq, pk, cq, ck)


def _lane_band(shape, lo, width):
    lane = lax.broadcasted_iota(jnp.int32, shape, len(shape) - 1)
    return (lane >= lo) & (lane < lo + width)


def _rel_positions():
    rows = lax.broadcasted_iota(jnp.int32, (ATT_T, ATT_T), 0)
    cols = lax.broadcasted_iota(jnp.int32, (ATT_T, ATT_T), 1)
    return rows, cols


def _col_block(a, kb):
    return a[:, kb * ATT_T:(kb + 1) * ATT_T]


def _weights_times_v(p_ref, hh, nkb, v):
    p = jnp.concatenate([p_ref[hh, kb] for kb in range(nkb)], axis=1) if nkb > 1 else p_ref[hh, 0]
    return _dot(p, v)


def _softmax_attention(qcat_at, kcat_at, vaug_at, mask_fn, finish, s_ref, p_ref):
    def scores(i):
        nkb = i + 1
        row_max = []
        for hh in range(2):
            s = _dot_nt(qcat_at(i, hh), kcat_at(hh, nkb * ATT_T))
            blocks = [_col_block(s, kb) for kb in range(nkb)]
            blocks[-1] = jnp.where(mask_fn(*_rel_positions()), blocks[-1], -jnp.inf)
            m = None
            for kb in range(nkb):
                s_ref[i % 2, hh, kb] = blocks[kb]
                bm = jnp.maximum(blocks[kb][:, :LANES], blocks[kb][:, LANES:])
                m = bm if m is None else jnp.maximum(m, bm)
            row_max.append(jnp.max(m, axis=-1, keepdims=True))
        return row_max

    row_max = scores(0)
    for i in range(N_QB):
        nkb = i + 1
        next_max = scores(i + 1) if i + 1 < N_QB else None
        for kb in range(nkb):
            for hh in range(2):
                p_ref[i % 2, hh, kb] = jnp.exp(s_ref[i % 2, hh, kb] - row_max[hh]).astype(BF16)
        finish(i, [_weights_times_v(p_ref.at[i % 2], hh, nkb, vaug_at(hh, nkb * ATT_T)) for hh in range(2)])
        row_max = next_max


def _pair_norm(o, gain, head_d):
    first = _lane_band(o.shape, 0, head_d)
    sq = o * o
    ss_a = jnp.sum(jnp.where(first, sq, 0.0), axis=-1, keepdims=True)
    ss_b = jnp.sum(sq, axis=-1, keepdims=True) - ss_a
    inv = jnp.where(first, lax.rsqrt(ss_a / head_d + NORM_EPS), lax.rsqrt(ss_b / head_d + NORM_EPS))
    return o * inv * gain


_SCORE_SCRATCH = [
    pltpu.VMEM((2, 2, N_QB, ATT_T, ATT_T), F32),
    pltpu.VMEM((2, 2, N_QB, ATT_T, ATT_T), BF16),
]


def _mla_body(qn_ref, qr_ref, kn_ref, v_ref, kr_ref, g_ref, o_ref, kcat_ref, vaug_ref, s_ref, p_ref):
    d = MLA_NOPE_D
    for hh in range(2):
        kcat_ref[hh, :, :d] = kn_ref[:, hh * d:(hh + 1) * d]
        kcat_ref[hh, :, d:] = kr_ref[...]
        vaug_ref[hh, :, :d] = v_ref[:, hh * d:(hh + 1) * d]
        vaug_ref[hh, :, d:] = jnp.ones((SEQ_LEN, LANES), BF16)

    def chunk_mask(rows, cols):
        return cols < ((rows >> 6) + 1) * CHUNK_LEN

    def qcat_at(i, hh):
        r = slice(i * ATT_T, (i + 1) * ATT_T)
        qr = qr_ref[r, :]
        qr = jnp.where(_lane_band(qr.shape, hh * MLA_ROPE_D, MLA_ROPE_D), qr, jnp.zeros_like(qr))
        return jnp.concatenate([qn_ref[r, hh * d:(hh + 1) * d], qr], axis=1)

    def finish(i, outs):
        r = slice(i * ATT_T, (i + 1) * ATT_T)
        for hh in range(2):
            o = outs[hh][:, :d] / outs[hh][:, d:]
            o_ref[r, hh * d:(hh + 1) * d] = _rms(o, g_ref[0, :, hh * d:(hh + 1) * d]).astype(BF16)

    _softmax_attention(
        qcat_at, lambda hh, klen: kcat_ref[hh, :klen, :], lambda hh, klen: vaug_ref[hh, :klen, :],
        chunk_mask, finish, s_ref, p_ref)


def _mla_attention(q_mla, kv_mla, krope, out_gain, layer):
    n_pair = MLA_H // 2
    pair_w = 2 * MLA_NOPE_D
    return pl.pallas_call(
        _mla_body,
        grid=(N_BATCH, n_pair),
        in_specs=[
            pl.BlockSpec((SEQ_LEN, pair_w), lambda b, p: (b, p)),
            pl.BlockSpec((SEQ_LEN, LANES), lambda b, p: (b, MLA_H * MLA_NOPE_D // LANES + p)),
            pl.BlockSpec((SEQ_LEN, pair_w), lambda b, p: (b, p)),
            pl.BlockSpec((SEQ_LEN, pair_w), lambda b, p: (b, n_pair + p)),
            pl.BlockSpec((SEQ_LEN, LANES), lambda b, p: (b, 0)),
            pl.BlockSpec((1, 1, pair_w), lambda b, p: (layer, 0, p)),
        ],
        out_specs=pl.BlockSpec((SEQ_LEN, pair_w), lambda b, p: (b, p)),
        out_shape=jax.ShapeDtypeStruct((N_TOK, MLA_H * MLA_V_D), BF16),
        scratch_shapes=[
            pltpu.VMEM((2, SEQ_LEN, 2 * LANES), BF16),
            pltpu.VMEM((2, SEQ_LEN, 2 * LANES), BF16),
        ] + _SCORE_SCRATCH,
        compiler_params=_params(("arbitrary", "arbitrary")),
        name="mla_attention",
    )(q_mla, q_mla, kv_mla, kv_mla, krope, out_gain)


def _fox_body(q_ref, k_ref, v_ref, qa_ref, ka_ref, g_ref, o_ref, kcat_ref, vaug_ref, s_ref, p_ref):
    kcat_ref[:, :LANES] = k_ref[...]
    kcat_ref[:, LANES:] = ka_ref[...]
    vaug_ref[:, :LANES] = v_ref[...]
    vaug_ref[:, LANES:] = jnp.ones((SEQ_LEN, LANES), BF16)

    def causal(rows, cols):
        return cols <= rows

    def qcat_at(i, hh):
        r = slice(i * ATT_T, (i + 1) * ATT_T)
        q = q_ref[r, :]
        qa = qa_ref[r, :]
        zero = jnp.zeros_like(q)
        return jnp.concatenate([
            jnp.where(_lane_band(q.shape, hh * FOX_D, FOX_D), q, zero),
            jnp.where(_lane_band(qa.shape, hh * SUBLANES, SUBLANES), qa, zero)], axis=1)

    def finish(i, outs):
        heads = [outs[hh][:, :LANES] / outs[hh][:, LANES:] for hh in range(2)]
        o = jnp.where(_lane_band(heads[0].shape, 0, FOX_D), heads[0], heads[1])
        o_ref[i * ATT_T:(i + 1) * ATT_T, :] = _pair_norm(o, g_ref[0], FOX_D).astype(BF16)

    _softmax_attention(
        qcat_at, lambda hh, klen: kcat_ref[:klen, :], lambda hh, klen: vaug_ref[:klen, :],
        causal, finish, s_ref, p_ref)


def _fox_attention(p_att, qaug, kaug, out_gain, layer):
    n_pair = FOX_H // 2
    grp = HEAD_GRP // LANES
    gain_blk = (MLA_H * MLA_V_D + SB_H * SB_D) // LANES
    return pl.pallas_call(
        _fox_body,
        grid=(N_BATCH, n_pair),
        in_specs=[
            pl.BlockSpec((SEQ_LEN, LANES), lambda b, p: (b, 3 * grp + p)),
            pl.BlockSpec((SEQ_LEN, LANES), lambda b, p: (b, 4 * grp + p)),
            pl.BlockSpec((SEQ_LEN, LANES), lambda b, p: (b, 5 * grp + p)),
            pl.BlockSpec((SEQ_LEN, LANES), lambda b, p: (b, p)),
            pl.BlockSpec((SEQ_LEN, LANES), lambda b, p: (b, p)),
            pl.BlockSpec((1, 1, LANES), lambda b, p: (layer, 0, gain_blk + p)),
        ],
        out_specs=pl.BlockSpec((SEQ_LEN, LANES), lambda b, p: (b, p)),
        out_shape=jax.ShapeDtypeStruct((N_TOK, FOX_H * FOX_D), BF16),
        scratch_shapes=[
            pltpu.VMEM((SEQ_LEN, 2 * LANES), BF16),
            pltpu.VMEM((SEQ_LEN, 2 * LANES), BF16),
        ] + _SCORE_SCRATCH,
        compiler_params=_params(("arbitrary", "arbitrary")),
        name="fox_attention",
    )(p_att, p_att, p_att, qaug, kaug, out_gain)


def _sb_body(q_ref, k_ref, v_ref, tri_ref, g_ref, o_ref, u_ref, p_ref):
    def scores(i):
        nkb = i + 1
        q = q_ref[i * ATT_T:(i + 1) * ATT_T, :]
        zero = jnp.zeros_like(q)
        for hh in range(2):
            z = _dot_nt(jnp.where(_lane_band(q.shape, hh * SB_D, SB_D), q, zero), k_ref[:nkb * ATT_T, :])
            for kb in range(nkb):
                u_ref[i % 2, hh, kb] = _col_block(z, kb)

    def weights(i):
        nkb = i + 1
        right = [None, None]
        for kb in reversed(range(nkb)):
            on_diagonal = kb == nkb - 1
            for hh in range(2):
                u = u_ref[i % 2, hh, kb]
                neg_u = -u
                keep = jnp.minimum(neg_u, 0.0) - jnp.log2(1.0 + jnp.exp2(jnp.minimum(u, neg_u)))
                if on_diagonal:
                    rows, cols = _rel_positions()
                    strict = cols < rows
                    keep = jnp.where(strict, keep, 0.0)
                hi = keep.astype(BF16)
                lo = (keep - hi.astype(F32)).astype(BF16)
                log_w = u + _dot(jnp.concatenate([hi, lo], axis=1), tri_ref[...])
                if right[hh] is not None:
                    log_w = log_w + right[hh]
                w = jnp.exp2(log_w)
                if on_diagonal:
                    w = jnp.where(strict, w, 0.0)
                p_ref[i % 2, hh, kb] = w.astype(BF16)
                total = jnp.sum(keep, axis=-1, keepdims=True)
                right[hh] = total if right[hh] is None else right[hh] + total

    scores(0)
    for i in range(N_QB):
        nkb = i + 1
        if i + 1 < N_QB:
            scores(i + 1)
        weights(i)
        heads = [_weights_times_v(p_ref.at[i % 2], hh, nkb, v_ref[:nkb * ATT_T, :]) for hh in range(2)]
        o = jnp.where(_lane_band(heads[0].shape, 0, SB_D), heads[0], heads[1])
        o_ref[i * ATT_T:(i + 1) * ATT_T, :] = _pair_norm(o, g_ref[0], SB_D).astype(BF16)


def _sb_attention(p_att, out_gain, layer):
    n_pair = SB_H // 2
    grp = HEAD_GRP // LANES
    gain_blk = MLA_H * MLA_V_D // LANES
    tri = np.tril(np.ones((ATT_T, ATT_T), np.float32))
    tri = jnp.asarray(np.concatenate([tri, tri], axis=0), BF16)
    return pl.pallas_call(
        _sb_body,
        grid=(N_BATCH, n_pair),
        in_specs=[
            pl.BlockSpec((SEQ_LEN, LANES), lambda b, p: (b, p)),
            pl.BlockSpec((SEQ_LEN, LANES), lambda b, p: (b, grp + p)),
            pl.BlockSpec((SEQ_LEN, LANES), lambda b, p: (b, 2 * grp + p)),
            _resident((2 * ATT_T, ATT_T)),
            pl.BlockSpec((1, 1, LANES), lambda b, p: (layer, 0, gain_blk + p)),
        ],
        out_specs=pl.BlockSpec((SEQ_LEN, LANES), lambda b, p: (b, p)),
        out_shape=jax.ShapeDtypeStruct((N_TOK, SB_H * SB_D), BF16),
        scratch_shapes=_SCORE_SCRATCH,
        compiler_params=_params(("arbitrary", "arbitrary")),
        name="sb_attention",
    )(p_att, p_att, p_att, tri, out_gain)


def _outproj_body(x_ref, mod_ref, om_ref, os_ref, of_ref, w_ref, o_ref):
    n_mla = MLA_H * MLA_V_D
    n_sb = SB_H * SB_D
    y = (_dot(om_ref[...], w_ref[0, :n_mla, :]) + _dot(os_ref[...], w_ref[0, n_mla:n_mla + n_sb, :])
         + _dot(of_ref[...], w_ref[0, n_mla + n_sb:, :]))
    o_ref[...] = x_ref[...] + mod_ref[0, 0][2:3] * y


def _out_project(x2, mod4, layer, o_mla, o_sb, o_fx, w_out):
    tm = PROJ_TM
    tiles_per_seq = SEQ_LEN // tm
    row = lambda i: (i, 0)
    return pl.pallas_call(
        _outproj_body,
        grid=(N_TOK // tm,),
        in_specs=[
            pl.BlockSpec((tm, D_MODEL), row),
            pl.BlockSpec((1, 1, 6, D_MODEL), lambda i: (layer, i // tiles_per_seq, 0, 0)),
            pl.BlockSpec((tm, MLA_H * MLA_V_D), row),
            pl.BlockSpec((tm, SB_H * SB_D), row),
            pl.BlockSpec((tm, FOX_H * FOX_D), row),
            _layer_resident((D_MODEL, D_MODEL), layer),
        ],
        out_specs=pl.BlockSpec((tm, D_MODEL), row),
        out_shape=jax.ShapeDtypeStruct((N_TOK, D_MODEL), F32),
        compiler_params=_params(("arbitrary",)),
        name="out_project",
    )(x2, mod4, o_mla, o_sb, o_fx, w_out)


def _ffn_body(x_ref, mod_ref, gn_ref, w1_ref, w2_ref, gf_ref, o_ref, h_ref, *, final):
    j = pl.program_id(1)
    mod = mod_ref[0, 0]

    @pl.when(j == 0)
    def _():
        h = _rms(x_ref[...], gn_ref[0]) * (1.0 + mod[4:5]) + mod[3:4]
        h_ref[...] = h.astype(BF16)
        o_ref[...] = jnp.zeros(o_ref.shape, F32)

    a = jnp.maximum(_dot(h_ref[...], w1_ref[0]), 0.0)
    o_ref[...] += _dot((a * a).astype(BF16), w2_ref[0])

    @pl.when(j == pl.num_programs(1) - 1)
    def _():
        y = x_ref[...] + mod[5:6] * o_ref[...]
        o_ref[...] = _rms(y, gf_ref[...]) if final else y


def _ffn(x2, mod4, layer, gn, w1, w2, g_final, final):
    tm, tf = FFN_TM, FFN_TF
    tiles_per_seq = SEQ_LEN // tm
    return pl.pallas_call(
        functools.partial(_ffn_body, final=final),
        grid=(N_TOK // tm, FF_DIM // tf),
        in_specs=[
            pl.BlockSpec((tm, D_MODEL), lambda i, j: (i, 0)),
            pl.BlockSpec((1, 1, 6, D_MODEL), lambda i, j: (layer, i // tiles_per_seq, 0, 0)),
            pl.BlockSpec((1, 1, D_MODEL), lambda i, j: (layer, 0, 0)),
            pl.BlockSpec((1, D_MODEL, tf), lambda i, j: (layer, 0, j)),
            pl.BlockSpec((1, tf, D_MODEL), lambda i, j: (layer, j, 0)),
            pl.BlockSpec((1, D_MODEL), lambda i, j: (0, 0)),
        ],
        out_specs=pl.BlockSpec((tm, D_MODEL), lambda i, j: (i, 0)),
        out_shape=jax.ShapeDtypeStruct((N_TOK, D_MODEL), F32),
        scratch_shapes=[pltpu.VMEM((tm, D_MODEL), BF16)],
        compiler_params=_params(("arbitrary", "arbitrary")),
        name="ffn",
    )(x2, mod4, gn, w1, w2, g_final)


def _swap_halves(w):
    half = w.shape[-1] // 2
    return jnp.concatenate([w[..., half:], w[..., :half]], axis=-1)


def _pack_w_low(w):
    o_kr = MLA_QL + MLA_KVL
    kr = w[..., o_kr:ATT_SRC]
    kr_sw = _swap_halves(kr)
    f_pad = jnp.pad(w[..., ATT_SRC + ATT_W:], ((0, 0), (0, 0), (0, LANES - FOX_H)))
    return jnp.concatenate([w[..., :o_kr], kr, kr, kr_sw, kr_sw, f_pad], axis=-1).astype(BF16)


def _pack_att_body(w_ref, o_ref):
    o_ref[0] = w_ref[0, :, ATT_SRC:ATT_SRC + ATT_W].astype(BF16)


def _pack_w_att(w):
    return pl.pallas_call(
        _pack_att_body,
        grid=(N_LAYERS, D_MODEL // PACK_TR),
        in_specs=[pl.BlockSpec((1, PACK_TR, IN_SRC_W), lambda l, i: (l, i, 0))],
        out_specs=pl.BlockSpec((1, PACK_TR, ATT_W), lambda l, i: (l, i, 0)),
        out_shape=jax.ShapeDtypeStruct((N_LAYERS, D_MODEL, ATT_W), BF16),
        compiler_params=_params(("arbitrary", "arbitrary")),
        name="pack_w_att",
    )(w)


def _pack_w_uq(w):
    w4 = w.astype(BF16).reshape(N_LAYERS, MLA_QL, MLA_H, MLA_NOPE_D + MLA_ROPE_D)
    flat = lambda a: a.reshape(N_LAYERS, MLA_QL, -1)
    rope = w4[..., MLA_NOPE_D:]
    return jnp.concatenate([flat(w4[..., :MLA_NOPE_D]), flat(rope), flat(_swap_halves(rope))], axis=-1)


def _pack_w_ukv(w):
    w4 = w.astype(BF16).reshape(N_LAYERS, MLA_KVL, MLA_H, MLA_NOPE_D + MLA_V_D)
    flat = lambda a: a.reshape(N_LAYERS, MLA_KVL, -1)
    return jnp.concatenate([flat(w4[..., :MLA_NOPE_D]), flat(w4[..., MLA_NOPE_D:])], axis=-1)


def _rope_tables():
    pos = jnp.arange(SEQ_LEN, dtype=F32)
    inv = ROPE_THETA ** (-jnp.arange(0, MLA_ROPE_D, 2, dtype=F32) / MLA_ROPE_D)
    ang = pos[:, None] * inv[None, :]
    cos, sin = jnp.cos(ang), jnp.sin(ang)
    cos_t = jnp.tile(jnp.concatenate([cos, cos], axis=1), (1, MLA_H))
    sin_t = jnp.tile(jnp.concatenate([-sin, sin], axis=1), (1, MLA_H))
    return cos_t, sin_t


def kernel(x, c, w_ada, b_ada, norm_mix, w_in, q_norm, w_uq, kv_norm, w_ukv, b_forget, out_norm,
           w_out, norm_ffn, w_ff1, w_ff2, final_norm):
    assert x.shape == (N_BATCH, SEQ_LEN, D_MODEL) and c.shape == (N_BATCH, D_MODEL)
    cos_t, sin_t = _rope_tables()
    c_pad = jnp.pad(c, ((0, SUBLANES - N_BATCH), (0, 0)))
    mod = _ada_mod(c_pad, w_ada, b_ada)
    mod4 = mod[:, :N_BATCH].reshape(N_LAYERS, N_BATCH, 6, D_MODEL)

    rows = lambda v: v.reshape(N_LAYERS, 1, -1)
    assert w_in.shape == (N_LAYERS, D_MODEL, IN_SRC_W)
    wlow_p, watt_p, wuq_p, wukv_p = _pack_w_low(w_in), _pack_w_att(w_in), _pack_w_uq(w_uq), _pack_w_ukv(w_ukv)
    wout_b, wff1_b, wff2_b = w_out.astype(BF16), w_ff1.astype(BF16), w_ff2.astype(BF16)
    bf_pad = jnp.pad(b_forget, ((0, 0), (0, LANES - FOX_H))).reshape(N_LAYERS, 1, LANES)
    out_gain = rows(out_norm)

    x2 = x.reshape(N_TOK, D_MODEL)
    for l in range(N_LAYERS):
        q_mla, kv_mla, krope, p_att, logf = _project(
            x2, mod4, l, rows(norm_mix), wlow_p, watt_p, rows(q_norm), wuq_p, rows(kv_norm), wukv_p, bf_pad,
            cos_t, sin_t)
        qaug, kaug = _fgate(logf)
        o_mla = _mla_attention(q_mla, kv_mla, krope, out_gain, l)
        o_sb = _sb_attention(p_att, out_gain, l)
        o_fx = _fox_attention(p_att, qaug, kaug, out_gain, l)
        x2 = _out_project(x2, mod4, l, o_mla, o_sb, o_fx, wout_b)
        x2 = _ffn(x2, mod4, l, rows(norm_ffn), wff1_b, wff2_b, final_norm.reshape(1, -1),
                  final=(l == N_LAYERS - 1))
    return x2.reshape(N_BATCH, SEQ_LEN, D_MODEL)
```

```python
import functools

import numpy as np
import jax
import jax.numpy as jnp
from jax import lax
from jax.experimental import pallas as pl
from jax.experimental.pallas import tpu as pltpu

F32 = jnp.float32
BF16 = jnp.bfloat16

D_MODEL = 2048
N_BATCH = 4
SEQ_LEN = 2048
N_LAYERS = 2
CHUNK_LEN = 64
NORM_EPS = 1e-6
LOG2_E = 1.4426950408889634
MLA_H = 8
MLA_QL = 512
MLA_KVL = 256
MLA_NOPE_D = 128
MLA_ROPE_D = 64
MLA_V_D = 128
ROPE_THETA = 10000.0
SB_H = 8
SB_D = 64
FOX_H = 8
FOX_D = 64
FF_DIM = 4 * D_MODEL
N_TOK = N_BATCH * SEQ_LEN

LANES = 128
SUBLANES = 8
VMEM_LIMIT = 56 * 1024 * 1024

R_CQ = 0
R_CKV = R_CQ + MLA_QL
R_KR = R_CKV + MLA_KVL
R_ATT = R_KR + MLA_ROPE_D
ATT_W = 3 * SB_H * SB_D + 3 * FOX_H * FOX_D
R_F = R_ATT + ATT_W
IN_W = R_F + FOX_H
HEAD_GRP = SB_H * SB_D

PROJ_TM = 512
FFN_TM = 1024
FFN_TF = 512
ADA_TN = 1024
ATT_T = 256
N_QB = SEQ_LEN // ATT_T
CUM_T = 256


def _dot(a, b):
    return jnp.dot(a, b, preferred_element_type=F32)


def _dot_nt(a, b):
    return lax.dot_general(a, b, (((1,), (1,)), ((), ())), preferred_element_type=F32)


def _split3(x):
    hi = x.astype(BF16)
    r1 = x - hi.astype(F32)
    mid = r1.astype(BF16)
    lo = (r1 - mid.astype(F32)).astype(BF16)
    return hi, mid, lo


def _softplus(z):
    return jnp.maximum(z, 0.0) + jnp.log(1.0 + jnp.exp(-jnp.abs(z)))


def _rms(x, gain):
    return x * lax.rsqrt(jnp.mean(x * x, axis=-1, keepdims=True) + NORM_EPS) * gain


def _resident(shape):
    nd = len(shape)
    return pl.BlockSpec(shape, lambda *_: (0,) * nd, pipeline_mode=pl.Buffered(1))


def _layer_resident(shape, layer):
    nd = len(shape)
    return pl.BlockSpec((1,) + shape, lambda *_: (layer,) + (0,) * nd, pipeline_mode=pl.Buffered(1))


def _params(sem):
    return pltpu.CompilerParams(dimension_semantics=sem, vmem_limit_bytes=VMEM_LIMIT)


def _ada_body(c_ref, w_ref, b_ref, o_ref):
    c = c_ref[...]
    c_act = (c * jax.nn.sigmoid(c)).astype(BF16)
    o_ref[0] = _dot(c_act, w_ref[0].astype(BF16)) + b_ref[0]


def _ada_mod(c_pad, w_ada, b_ada):
    n_out = 6 * D_MODEL
    return pl.pallas_call(
        _ada_body,
        grid=(N_LAYERS, n_out // ADA_TN),
        in_specs=[
            pl.BlockSpec((SUBLANES, D_MODEL), lambda l, j: (0, 0)),
            pl.BlockSpec((1, D_MODEL, ADA_TN), lambda l, j: (l, 0, j)),
            pl.BlockSpec((1, 1, ADA_TN), lambda l, j: (l, 0, j)),
        ],
        out_specs=pl.BlockSpec((1, SUBLANES, ADA_TN), lambda l, j: (l, 0, j)),
        out_shape=jax.ShapeDtypeStruct((N_LAYERS, SUBLANES, n_out), F32),
        compiler_params=_params(("arbitrary", "arbitrary")),
        name="ada_mod",
    )(c_pad, w_ada, b_ada.reshape(N_LAYERS, 1, n_out))


def _proj_body(x_ref, mod_ref, gn_ref, wt_ref, wkr_ref, wf_ref, qn_ref, wuq_ref, kvn_ref, wukv_ref, bf_ref,
               cos_ref, sin_ref, q_out, kv_out, kr_out, att_out, lf_out, h_ref):
    mod = mod_ref[0, 0]
    x = x_ref[...]
    h = _rms(x, gn_ref[0]) * (1.0 + mod[1:2]) + mod[0:1]
    h_ref[...] = h.astype(BF16)

    def proj(r0, width):
        return _dot_nt(h_ref[...], wt_ref[0, r0:r0 + width, :])

    mla_scale = (MLA_NOPE_D + MLA_ROPE_D) ** -0.5
    n_nope = MLA_H * MLA_NOPE_D
    n_rope = MLA_H * MLA_ROPE_D

    cq = _rms(proj(R_CQ, MLA_QL), qn_ref[0]).astype(BF16)
    q = _dot(cq, wuq_ref[0])
    q_out[:, :n_nope] = (q[:, :n_nope] * mla_scale).astype(BF16)
    q_rot = q[:, n_nope:n_nope + n_rope] * cos_ref[...] + q[:, n_nope + n_rope:] * sin_ref[...]
    q_out[:, n_nope:] = (q_rot * mla_scale).astype(BF16)

    ckv = _rms(proj(R_CKV, MLA_KVL), kvn_ref[0]).astype(BF16)
    kv_out[...] = _dot(ckv, wukv_ref[0]).astype(BF16)

    kr = _dot_nt(h_ref[...], wkr_ref[0])
    kr_out[...] = (kr[:, :LANES] * cos_ref[:, :LANES] + kr[:, LANES:] * sin_ref[:, :LANES]).astype(BF16)

    lf_out[...] = -_softplus(-(_dot_nt(h_ref[...], wf_ref[0]) + bf_ref[0]))

    sb_scale = SB_D ** -0.5 * LOG2_E
    fox_scale = FOX_D ** -0.5
    scales = (sb_scale, 1.0, 1.0, fox_scale, 1.0, 1.0)
    for g, sc in enumerate(scales):
        blk = proj(R_ATT + g * HEAD_GRP, HEAD_GRP)
        if sc != 1.0:
            blk = blk * sc
        att_out[:, g * HEAD_GRP:(g + 1) * HEAD_GRP] = blk.astype(BF16)


def _project(x2, mod4, layer, gn, wt_in, wt_kr, wt_f, qn, wuq, kvn, wukv, bfp, cos_t, sin_t):
    tm = PROJ_TM
    tiles_per_seq = SEQ_LEN // tm
    q_w = MLA_H * (MLA_NOPE_D + MLA_ROPE_D)
    kv_w = MLA_H * (MLA_NOPE_D + MLA_V_D)
    rope_w = MLA_H * MLA_ROPE_D
    row = lambda i: (i, 0)
    return pl.pallas_call(
        _proj_body,
        grid=(N_TOK // tm,),
        in_specs=[
            pl.BlockSpec((tm, D_MODEL), row),
            pl.BlockSpec((1, 1, 6, D_MODEL), lambda i: (layer, i // tiles_per_seq, 0, 0)),
            _layer_resident((1, D_MODEL), layer),
            _layer_resident((IN_W, D_MODEL), layer),
            _layer_resident((2 * LANES, D_MODEL), layer),
            _layer_resident((LANES, D_MODEL), layer),
            _layer_resident((1, MLA_QL), layer),
            _layer_resident((MLA_QL, q_w + rope_w), layer),
            _layer_resident((1, MLA_KVL), layer),
            _layer_resident((MLA_KVL, kv_w), layer),
            _layer_resident((1, LANES), layer),
            pl.BlockSpec((tm, rope_w), lambda i: (i % tiles_per_seq, 0)),
            pl.BlockSpec((tm, rope_w), lambda i: (i % tiles_per_seq, 0)),
        ],
        out_specs=[
            pl.BlockSpec((tm, q_w), row),
            pl.BlockSpec((tm, kv_w), row),
            pl.BlockSpec((tm, LANES), row),
            pl.BlockSpec((tm, ATT_W), row),
            pl.BlockSpec((tm, LANES), row),
        ],
        out_shape=[
            jax.ShapeDtypeStruct((N_TOK, q_w), BF16),
            jax.ShapeDtypeStruct((N_TOK, kv_w), BF16),
            jax.ShapeDtypeStruct((N_TOK, LANES), BF16),
            jax.ShapeDtypeStruct((N_TOK, ATT_W), BF16),
            jax.ShapeDtypeStruct((N_TOK, LANES), F32),
        ],
        scratch_shapes=[pltpu.VMEM((tm, D_MODEL), BF16)],
        compiler_params=_params(("arbitrary",)),
        name="project",
    )(x2, mod4, gn, wt_in, wt_kr, wt_f, qn, wuq, kvn, wukv, bfp, cos_t, sin_t)


def _fgate_body(lf_ref, tri_ref, pq_ref, pk_ref, cq_ref, ck_ref, qa_out, ka_out):
    n_blk = SEQ_LEN // CUM_T
    local = []
    for blk in range(n_blk):
        hi, mid, lo = _split3(lf_ref[blk * CUM_T:(blk + 1) * CUM_T, :])
        tri = tri_ref[...]
        local.append(_dot(tri, hi) + _dot(tri, mid) + _dot(tri, lo))
    carry = jnp.zeros((1, LANES), F32)
    for blk in range(n_blk):
        rows = slice(blk * CUM_T, (blk + 1) * CUM_T)
        f_blk = local[blk] + carry
        carry = f_blk[CUM_T - 1:CUM_T, :]
        f_hi, f_mid, f_lo = _split3(f_blk)
        qa = _dot(f_hi, pq_ref[0]) + _dot(f_mid, pq_ref[1]) + _dot(f_lo, pq_ref[2]) + cq_ref[...]
        ka = ck_ref[...] - (_dot(f_hi, pk_ref[0]) + _dot(f_mid, pk_ref[1]) + _dot(f_lo, pk_ref[2]))
        qa_out[rows, :] = qa.astype(BF16)
        ka_out[rows, :] = ka.astype(BF16)


def _fgate_constants():
    tri = np.tril(np.ones((CUM_T, CUM_T), np.float32))
    n_pair = FOX_H // 2
    pq = np.zeros((3, LANES, n_pair * LANES), np.float32)
    pk = np.zeros((3, LANES, n_pair * LANES), np.float32)
    cq = np.zeros((1, n_pair * LANES), np.float32)
    ck = np.zeros((1, n_pair * LANES), np.float32)
    for h in range(FOX_H):
        base = (h // 2) * LANES + (h % 2) * SUBLANES
        for t in range(3):
            pq[t, h, base + t] = 1.0
            pk[t, h, base + 3 + t] = 1.0
            cq[0, base + 3 + t] = 1.0
            ck[0, base + t] = 1.0
    as_bf16 = lambda a: jnp.asarray(a, BF16)
    return as_bf16(tri), as_bf16(pq), as_bf16(pk), jnp.asarray(cq), jnp.asarray(ck)


def _fgate(logf):
    tri, pq, pk, cq, ck = _fgate_constants()
    aug_w = (FOX_H // 2) * LANES
    seq_blk = lambda b: (b, 0)
    return pl.pallas_call(
        _fgate_body,
        grid=(N_BATCH,),
        in_specs=[
            pl.BlockSpec((SEQ_LEN, LANES), seq_blk),
            _resident((CUM_T, CUM_T)),
            _resident((3, LANES, aug_w)),
            _resident((3, LANES, aug_w)),
            _resident((1, aug_w)),
            _resident((1, aug_w)),
        ],
        out_specs=[pl.BlockSpec((SEQ_LEN, aug_w), seq_blk), pl.BlockSpec((SEQ_LEN, aug_w), seq_blk)],
        out_shape=[jax.ShapeDtypeStruct((N_TOK, aug_w), BF16), jax.ShapeDtypeStruct((N_TOK, aug_w), BF16)],
        compiler_params=_params(("arbitrary",)),
        name="fgate",
    )(logf, tri, pq, pk, cq, ck)


def _lane_band(shape, lo, width):
    lane = lax.broadcasted_iota(jnp.int32, shape, len(shape) - 1)
    return (lane >= lo) & (lane < lo + width)


def _rel_positions():
    rows = lax.broadcasted_iota(jnp.int32, (ATT_T, ATT_T), 0)
    cols = lax.broadcasted_iota(jnp.int32, (ATT_T, ATT_T), 1)
    return rows, cols


def _col_block(a, kb):
    return a[:, kb * ATT_T:(kb + 1) * ATT_T]


def _weights_times_v(p_ref, hh, nkb, v):
    p = jnp.concatenate([p_ref[hh, kb] for kb in range(nkb)], axis=1) if nkb > 1 else p_ref[hh, 0]
    return _dot(p, v)


def _softmax_attention(qcat_at, kcat_at, vaug_at, mask_fn, finish, s_ref, p_ref):
    def scores(i):
        nkb = i + 1
        row_max = []
        for hh in range(2):
            s = _dot_nt(qcat_at(i, hh), kcat_at(hh, nkb * ATT_T))
            blocks = [_col_block(s, kb) for kb in range(nkb)]
            blocks[-1] = jnp.where(mask_fn(*_rel_positions()), blocks[-1], -jnp.inf)
            m = None
            for kb in range(nkb):
                s_ref[i % 2, hh, kb] = blocks[kb]
                bm = jnp.maximum(blocks[kb][:, :LANES], blocks[kb][:, LANES:])
                m = bm if m is None else jnp.maximum(m, bm)
            row_max.append(jnp.max(m, axis=-1, keepdims=True))
        return row_max

    row_max = scores(0)
    for i in range(N_QB):
        nkb = i + 1
        next_max = scores(i + 1) if i + 1 < N_QB else None
        for kb in range(nkb):
            for hh in range(2):
                p_ref[i % 2, hh, kb] = jnp.exp(s_ref[i % 2, hh, kb] - row_max[hh]).astype(BF16)
        finish(i, [_weights_times_v(p_ref.at[i % 2], hh, nkb, vaug_at(hh, nkb * ATT_T)) for hh in range(2)])
        row_max = next_max


def _pair_norm(o, gain, head_d):
    first = _lane_band(o.shape, 0, head_d)
    sq = o * o
    ss_a = jnp.sum(jnp.where(first, sq, 0.0), axis=-1, keepdims=True)
    ss_b = jnp.sum(sq, axis=-1, keepdims=True) - ss_a
    inv = jnp.where(first, lax.rsqrt(ss_a / head_d + NORM_EPS), lax.rsqrt(ss_b / head_d + NORM_EPS))
    return o * inv * gain


_SCORE_SCRATCH = [
    pltpu.VMEM((2, 2, N_QB, ATT_T, ATT_T), F32),
    pltpu.VMEM((2, 2, N_QB, ATT_T, ATT_T), BF16),
]
ATT_PAIRS = 4


def _cast_along(w, layer):
    _, n_rows, n_cols = w.shape
    slab = n_rows // (N_BATCH * ATT_PAIRS)
    assert slab * N_BATCH * ATT_PAIRS == n_rows and slab % (2 * SUBLANES) == 0
    in_spec = pl.BlockSpec((1, slab, n_cols), lambda b, p: (layer, b * ATT_PAIRS + p, 0))
    out_spec = pl.BlockSpec((slab, n_cols), lambda b, p: (b * ATT_PAIRS + p, 0))
    return in_spec, out_spec, jax.ShapeDtypeStruct((n_rows, n_cols), BF16)


def _mla_body(qn_ref, qr_ref, kn_ref, v_ref, kr_ref, g_ref, w_ref, o_ref, wb_ref, kcat_ref, vaug_ref,
              s_ref, p_ref):
    wb_ref[...] = w_ref[0].astype(BF16)
    d = MLA_NOPE_D
    for hh in range(2):
        kcat_ref[hh, :, :d] = kn_ref[:, hh * d:(hh + 1) * d]
        kcat_ref[hh, :, d:] = kr_ref[...]
        vaug_ref[hh, :, :d] = v_ref[:, hh * d:(hh + 1) * d]
        vaug_ref[hh, :, d:] = jnp.ones((SEQ_LEN, LANES), BF16)

    def chunk_mask(rows, cols):
        return cols < ((rows >> 6) + 1) * CHUNK_LEN

    def qcat_at(i, hh):
        r = slice(i * ATT_T, (i + 1) * ATT_T)
        qr = qr_ref[r, :]
        qr = jnp.where(_lane_band(qr.shape, hh * MLA_ROPE_D, MLA_ROPE_D), qr, jnp.zeros_like(qr))
        return jnp.concatenate([qn_ref[r, hh * d:(hh + 1) * d], qr], axis=1)

    def finish(i, outs):
        r = slice(i * ATT_T, (i + 1) * ATT_T)
        for hh in range(2):
            o = outs[hh][:, :d] / outs[hh][:, d:]
            o_ref[r, hh * d:(hh + 1) * d] = _rms(o, g_ref[0, :, hh * d:(hh + 1) * d]).astype(BF16)

    _softmax_attention(
        qcat_at, lambda hh, klen: kcat_ref[hh, :klen, :], lambda hh, klen: vaug_ref[hh, :klen, :],
        chunk_mask, finish, s_ref, p_ref)


def _mla_attention(q_mla, kv_mla, krope, out_gain, layer, w_cast):
    n_pair = MLA_H // 2
    assert n_pair == ATT_PAIRS
    pair_w = 2 * MLA_NOPE_D
    w_in_spec, w_out_spec, w_out_shape = _cast_along(w_cast, layer)
    return pl.pallas_call(
        _mla_body,
        grid=(N_BATCH, n_pair),
        in_specs=[
            pl.BlockSpec((SEQ_LEN, pair_w), lambda b, p: (b, p)),
            pl.BlockSpec((SEQ_LEN, LANES), lambda b, p: (b, MLA_H * MLA_NOPE_D // LANES + p)),
            pl.BlockSpec((SEQ_LEN, pair_w), lambda b, p: (b, p)),
            pl.BlockSpec((SEQ_LEN, pair_w), lambda b, p: (b, n_pair + p)),
            pl.BlockSpec((SEQ_LEN, LANES), lambda b, p: (b, 0)),
            pl.BlockSpec((1, 1, pair_w), lambda b, p: (layer, 0, p)),
            w_in_spec,
        ],
        out_specs=[pl.BlockSpec((SEQ_LEN, pair_w), lambda b, p: (b, p)), w_out_spec],
        out_shape=[jax.ShapeDtypeStruct((N_TOK, MLA_H * MLA_V_D), BF16), w_out_shape],
        scratch_shapes=[
            pltpu.VMEM((2, SEQ_LEN, 2 * LANES), BF16),
            pltpu.VMEM((2, SEQ_LEN, 2 * LANES), BF16),
        ] + _SCORE_SCRATCH,
        compiler_params=_params(("arbitrary", "arbitrary")),
        name="mla_attention",
    )(q_mla, q_mla, kv_mla, kv_mla, krope, out_gain, w_cast)


def _fox_body(q_ref, k_ref, v_ref, qa_ref, ka_ref, g_ref, w_ref, o_ref, wb_ref, kcat_ref, vaug_ref,
              s_ref, p_ref):
    wb_ref[...] = w_ref[0].astype(BF16)
    kcat_ref[:, :LANES] = k_ref[...]
    kcat_ref[:, LANES:] = ka_ref[...]
    vaug_ref[:, :LANES] = v_ref[...]
    vaug_ref[:, LANES:] = jnp.ones((SEQ_LEN, LANES), BF16)

    def causal(rows, cols):
        return cols <= rows

    def qcat_at(i, hh):
        r = slice(i * ATT_T, (i + 1) * ATT_T)
        q = q_ref[r, :]
        qa = qa_ref[r, :]
        zero = jnp.zeros_like(q)
        return jnp.concatenate([
            jnp.where(_lane_band(q.shape, hh * FOX_D, FOX_D), q, zero),
            jnp.where(_lane_band(qa.shape, hh * SUBLANES, SUBLANES), qa, zero)], axis=1)

    def finish(i, outs):
        heads = [outs[hh][:, :LANES] / outs[hh][:, LANES:] for hh in range(2)]
        o = jnp.where(_lane_band(heads[0].shape, 0, FOX_D), heads[0], heads[1])
        o_ref[i * ATT_T:(i + 1) * ATT_T, :] = _pair_norm(o, g_ref[0], FOX_D).astype(BF16)

    _softmax_attention(
        qcat_at, lambda hh, klen: kcat_ref[:klen, :], lambda hh, klen: vaug_ref[:klen, :],
        causal, finish, s_ref, p_ref)


def _fox_attention(p_att, qaug, kaug, out_gain, layer, w_cast):
    n_pair = FOX_H // 2
    assert n_pair == ATT_PAIRS
    grp = HEAD_GRP // LANES
    gain_blk = (MLA_H * MLA_V_D + SB_H * SB_D) // LANES
    w_in_spec, w_out_spec, w_out_shape = _cast_along(w_cast, layer)
    return pl.pallas_call(
        _fox_body,
        grid=(N_BATCH, n_pair),
        in_specs=[
            pl.BlockSpec((SEQ_LEN, LANES), lambda b, p: (b, 3 * grp + p)),
            pl.BlockSpec((SEQ_LEN, LANES), lambda b, p: (b, 4 * grp + p)),
            pl.BlockSpec((SEQ_LEN, LANES), lambda b, p: (b, 5 * grp + p)),
            pl.BlockSpec((SEQ_LEN, LANES), lambda b, p: (b, p)),
            pl.BlockSpec((SEQ_LEN, LANES), lambda b, p: (b, p)),
            pl.BlockSpec((1, 1, LANES), lambda b, p: (layer, 0, gain_blk + p)),
            w_in_spec,
        ],
        out_specs=[pl.BlockSpec((SEQ_LEN, LANES), lambda b, p: (b, p)), w_out_spec],
        out_shape=[jax.ShapeDtypeStruct((N_TOK, FOX_H * FOX_D), BF16), w_out_shape],
        scratch_shapes=[
            pltpu.VMEM((SEQ_LEN, 2 * LANES), BF16),
            pltpu.VMEM((SEQ_LEN, 2 * LANES), BF16),
        ] + _SCORE_SCRATCH,
        compiler_params=_params(("arbitrary", "arbitrary")),
        name="fox_attention",
    )(p_att, p_att, p_att, qaug, kaug, out_gain, w_cast)


def _sb_body(q_ref, k_ref, v_ref, tri_ref, g_ref, w_ref, o_ref, wb_ref, u_ref, p_ref):
    wb_ref[...] = w_ref[0].astype(BF16)

    def scores(i):
        nkb = i + 1
        q = q_ref[i * ATT_T:(i + 1) * ATT_T, :]
        zero = jnp.zeros_like(q)
        for hh in range(2):
            z = _dot_nt(jnp.where(_lane_band(q.shape, hh * SB_D, SB_D), q, zero), k_ref[:nkb * ATT_T, :])
            for kb in range(nkb):
                u_ref[i % 2, hh, kb] = _col_block(z, kb)

    def weights(i):
        nkb = i + 1
        right = [None, None]
        for kb in reversed(range(nkb)):
            on_diagonal = kb == nkb - 1
            for hh in range(2):
                u = u_ref[i % 2, hh, kb]
                neg_u = -u
                keep = jnp.minimum(neg_u, 0.0) - jnp.log2(1.0 + jnp.exp2(jnp.minimum(u, neg_u)))
                if on_diagonal:
                    rows, cols = _rel_positions()
                    strict = cols < rows
                    keep = jnp.where(strict, keep, 0.0)
                hi = keep.astype(BF16)
                lo = (keep - hi.astype(F32)).astype(BF16)
                log_w = u + _dot(jnp.concatenate([hi, lo], axis=1), tri_ref[...])
                if right[hh] is not None:
                    log_w = log_w + right[hh]
                w = jnp.exp2(log_w)
                if on_diagonal:
                    w = jnp.where(strict, w, 0.0)
                p_ref[i % 2, hh, kb] = w.astype(BF16)
                total = jnp.sum(keep, axis=-1, keepdims=True)
                right[hh] = total if right[hh] is None else right[hh] + total

    scores(0)
    for i in range(N_QB):
        nkb = i + 1
        if i + 1 < N_QB:
            scores(i + 1)
        weights(i)
        heads = [_weights_times_v(p_ref.at[i % 2], hh, nkb, v_ref[:nkb * ATT_T, :]) for hh in range(2)]
        o = jnp.where(_lane_band(heads[0].shape, 0, SB_D), heads[0], heads[1])
        o_ref[i * ATT_T:(i + 1) * ATT_T, :] = _pair_norm(o, g_ref[0], SB_D).astype(BF16)


def _sb_attention(p_att, out_gain, layer, w_cast):
    n_pair = SB_H // 2
    assert n_pair == ATT_PAIRS
    grp = HEAD_GRP // LANES
    gain_blk = MLA_H * MLA_V_D // LANES
    w_in_spec, w_out_spec, w_out_shape = _cast_along(w_cast, layer)
    tri = np.tril(np.ones((ATT_T, ATT_T), np.float32))
    tri = jnp.asarray(np.concatenate([tri, tri], axis=0), BF16)
    return pl.pallas_call(
        _sb_body,
        grid=(N_BATCH, n_pair),
        in_specs=[
            pl.BlockSpec((SEQ_LEN, LANES), lambda b, p: (b, p)),
            pl.BlockSpec((SEQ_LEN, LANES), lambda b, p: (b, grp + p)),
            pl.BlockSpec((SEQ_LEN, LANES), lambda b, p: (b, 2 * grp + p)),
            _resident((2 * ATT_T, ATT_T)),
            pl.BlockSpec((1, 1, LANES), lambda b, p: (layer, 0, gain_blk + p)),
            w_in_spec,
        ],
        out_specs=[pl.BlockSpec((SEQ_LEN, LANES), lambda b, p: (b, p)), w_out_spec],
        out_shape=[jax.ShapeDtypeStruct((N_TOK, SB_H * SB_D), BF16), w_out_shape],
        scratch_shapes=_SCORE_SCRATCH,
        compiler_params=_params(("arbitrary", "arbitrary")),
        name="sb_attention",
    )(p_att, p_att, p_att, tri, out_gain, w_cast)


def _outproj_body(x_ref, mod_ref, om_ref, os_ref, of_ref, w_ref, o_ref):
    n_mla = MLA_H * MLA_V_D
    n_sb = SB_H * SB_D
    y = (_dot(om_ref[...], w_ref[:n_mla, :]) + _dot(os_ref[...], w_ref[n_mla:n_mla + n_sb, :])
         + _dot(of_ref[...], w_ref[n_mla + n_sb:, :]))
    o_ref[...] = x_ref[...] + mod_ref[0, 0][2:3] * y


def _out_project(x2, mod4, layer, o_mla, o_sb, o_fx, w_out):
    tm = PROJ_TM
    tiles_per_seq = SEQ_LEN // tm
    row = lambda i: (i, 0)
    return pl.pallas_call(
        _outproj_body,
        grid=(N_TOK // tm,),
        in_specs=[
            pl.BlockSpec((tm, D_MODEL), row),
            pl.BlockSpec((1, 1, 6, D_MODEL), lambda i: (layer, i // tiles_per_seq, 0, 0)),
            pl.BlockSpec((tm, MLA_H * MLA_V_D), row),
            pl.BlockSpec((tm, SB_H * SB_D), row),
            pl.BlockSpec((tm, FOX_H * FOX_D), row),
            _resident((D_MODEL, D_MODEL)),
        ],
        out_specs=pl.BlockSpec((tm, D_MODEL), row),
        out_shape=jax.ShapeDtypeStruct((N_TOK, D_MODEL), F32),
        compiler_params=_params(("arbitrary",)),
        name="out_project",
    )(x2, mod4, o_mla, o_sb, o_fx, w_out)


def _ffn_body(x_ref, mod_ref, gn_ref, w1_ref, w2_ref, gf_ref, o_ref, h_ref, *, final):
    j = pl.program_id(1)
    mod = mod_ref[0, 0]

    @pl.when(j == 0)
    def _():
        h = _rms(x_ref[...], gn_ref[0]) * (1.0 + mod[4:5]) + mod[3:4]
        h_ref[...] = h.astype(BF16)
        o_ref[...] = jnp.zeros(o_ref.shape, F32)

    a = jnp.maximum(_dot(h_ref[...], w1_ref[...]), 0.0)
    o_ref[...] += _dot((a * a).astype(BF16), w2_ref[...])

    @pl.when(j == pl.num_programs(1) - 1)
    def _():
        y = x_ref[...] + mod[5:6] * o_ref[...]
        o_ref[...] = _rms(y, gf_ref[...]) if final else y


def _ffn(x2, mod4, layer, gn, w1, w2, g_final, final):
    tm, tf = FFN_TM, FFN_TF
    tiles_per_seq = SEQ_LEN // tm
    return pl.pallas_call(
        functools.partial(_ffn_body, final=final),
        grid=(N_TOK // tm, FF_DIM // tf),
        in_specs=[
            pl.BlockSpec((tm, D_MODEL), lambda i, j: (i, 0)),
            pl.BlockSpec((1, 1, 6, D_MODEL), lambda i, j: (layer, i // tiles_per_seq, 0, 0)),
            pl.BlockSpec((1, 1, D_MODEL), lambda i, j: (layer, 0, 0)),
            pl.BlockSpec((D_MODEL, tf), lambda i, j: (0, j)),
            pl.BlockSpec((tf, D_MODEL), lambda i, j: (j, 0)),
            pl.BlockSpec((1, D_MODEL), lambda i, j: (0, 0)),
        ],
        out_specs=pl.BlockSpec((tm, D_MODEL), lambda i, j: (i, 0)),
        out_shape=jax.ShapeDtypeStruct((N_TOK, D_MODEL), F32),
        scratch_shapes=[pltpu.VMEM((tm, D_MODEL), BF16)],
        compiler_params=_params(("arbitrary", "arbitrary")),
        name="ffn",
    )(x2, mod4, gn, w1, w2, g_final)


def _swap_halves(w):
    half = w.shape[-1] // 2
    return jnp.concatenate([w[..., half:], w[..., :half]], axis=-1)


def _pack_w_in(w):
    wt = jnp.swapaxes(w, 1, 2).astype(BF16)
    kr = wt[:, R_KR:R_ATT]
    kr_sw = jnp.concatenate([kr[:, MLA_ROPE_D // 2:], kr[:, :MLA_ROPE_D // 2]], axis=1)
    wt_kr = jnp.concatenate([kr, kr, kr_sw, kr_sw], axis=1)
    wt_f = jnp.pad(wt[:, R_F:], ((0, 0), (0, LANES - FOX_H), (0, 0)))
    return wt, wt_kr, wt_f


def _pack_w_uq(w):
    w4 = w.astype(BF16).reshape(N_LAYERS, MLA_QL, MLA_H, MLA_NOPE_D + MLA_ROPE_D)
    flat = lambda a: a.reshape(N_LAYERS, MLA_QL, -1)
    rope = w4[..., MLA_NOPE_D:]
    return jnp.concatenate([flat(w4[..., :MLA_NOPE_D]), flat(rope), flat(_swap_halves(rope))], axis=-1)


def _pack_w_ukv(w):
    w4 = w.astype(BF16).reshape(N_LAYERS, MLA_KVL, MLA_H, MLA_NOPE_D + MLA_V_D)
    flat = lambda a: a.reshape(N_LAYERS, MLA_KVL, -1)
    return jnp.concatenate([flat(w4[..., :MLA_NOPE_D]), flat(w4[..., MLA_NOPE_D:])], axis=-1)


def _rope_tables():
    pos = jnp.arange(SEQ_LEN, dtype=F32)
    inv = ROPE_THETA ** (-jnp.arange(0, MLA_ROPE_D, 2, dtype=F32) / MLA_ROPE_D)
    ang = pos[:, None] * inv[None, :]
    cos, sin = jnp.cos(ang), jnp.sin(ang)
    cos_t = jnp.tile(jnp.concatenate([cos, cos], axis=1), (1, MLA_H))
    sin_t = jnp.tile(jnp.concatenate([-sin, sin], axis=1), (1, MLA_H))
    return cos_t, sin_t


def kernel(x, c, w_ada, b_ada, norm_mix, w_in, q_norm, w_uq, kv_norm, w_ukv, b_forget, out_norm,
           w_out, norm_ffn, w_ff1, w_ff2, final_norm):
    assert x.shape == (N_BATCH, SEQ_LEN, D_MODEL) and c.shape == (N_BATCH, D_MODEL)
    cos_t, sin_t = _rope_tables()
    c_pad = jnp.pad(c, ((0, SUBLANES - N_BATCH), (0, 0)))
    mod = _ada_mod(c_pad, w_ada, b_ada)
    mod4 = mod[:, :N_BATCH].reshape(N_LAYERS, N_BATCH, 6, D_MODEL)

    rows = lambda v: v.reshape(N_LAYERS, 1, -1)
    assert w_in.shape == (N_LAYERS, D_MODEL, IN_W)
    wt_in, wt_kr, wt_f = _pack_w_in(w_in)
    wuq_p, wukv_p = _pack_w_uq(w_uq), _pack_w_ukv(w_ukv)
    bf_pad = jnp.pad(b_forget, ((0, 0), (0, LANES - FOX_H))).reshape(N_LAYERS, 1, LANES)
    out_gain = rows(out_norm)

    x2 = x.reshape(N_TOK, D_MODEL)
    for l in range(N_LAYERS):
        q_mla, kv_mla, krope, p_att, logf = _project(
            x2, mod4, l, rows(norm_mix), wt_in, wt_kr, wt_f, rows(q_norm), wuq_p, rows(kv_norm), wukv_p, bf_pad,
            cos_t, sin_t)
        qaug, kaug = _fgate(logf)
        o_mla, wff1_b = _mla_attention(q_mla, kv_mla, krope, out_gain, l, w_ff1)
        o_sb, wout_b = _sb_attention(p_att, out_gain, l, w_out)
        o_fx, wff2_b = _fox_attention(p_att, qaug, kaug, out_gain, l, w_ff2)
        x2 = _out_project(x2, mod4, l, o_mla, o_sb, o_fx, wout_b)
        x2 = _ffn(x2, mod4, l, rows(norm_ffn), wff1_b, wff2_b, final_norm.reshape(1, -1),
                  final=(l == N_LAYERS - 1))
    return x2.reshape(N_BATCH, SEQ_LEN, D_MODEL)
```

```python
import functools

import numpy as np
import jax
import jax.numpy as jnp
from jax import lax
from jax.experimental import pallas as pl
from jax.experimental.pallas import tpu as pltpu

F32 = jnp.float32
BF16 = jnp.bfloat16

D_MODEL = 2048
N_BATCH = 4
SEQ_LEN = 2048
N_LAYERS = 2
CHUNK_LEN = 64
NORM_EPS = 1e-6
LOG2_E = 1.4426950408889634
MLA_H = 8
MLA_QL = 512
MLA_KVL = 256
MLA_NOPE_D = 128
MLA_ROPE_D = 64
MLA_V_D = 128
ROPE_THETA = 10000.0
SB_H = 8
SB_D = 64
FOX_H = 8
FOX_D = 64
FF_DIM = 4 * D_MODEL
N_TOK = N_BATCH * SEQ_LEN

LANES = 128
SUBLANES = 8
VMEM_LIMIT = 56 * 1024 * 1024

R_CQ = 0
R_CKV = R_CQ + MLA_QL
R_KR = R_CKV + MLA_KVL
R_ATT = R_KR + MLA_ROPE_D
ATT_W = 3 * SB_H * SB_D + 3 * FOX_H * FOX_D
R_F = R_ATT + ATT_W
IN_W = R_F + FOX_H
HEAD_GRP = SB_H * SB_D

PROJ_TM = 512
FFN_TM = 1024
FFN_TF = 512
ADA_TN = 1024
ATT_T = 256
N_QB = SEQ_LEN // ATT_T
QB_ORDER = tuple(reversed(range(N_QB)))
CUM_T = 256


def _dot(a, b):
    return jnp.dot(a, b, preferred_element_type=F32)


def _dot_nt(a, b):
    return lax.dot_general(a, b, (((1,), (1,)), ((), ())), preferred_element_type=F32)


def _split3(x):
    hi = x.astype(BF16)
    r1 = x - hi.astype(F32)
    mid = r1.astype(BF16)
    lo = (r1 - mid.astype(F32)).astype(BF16)
    return hi, mid, lo


def _softplus(z):
    return jnp.maximum(z, 0.0) + jnp.log(1.0 + jnp.exp(-jnp.abs(z)))


def _rms(x, gain):
    return x * lax.rsqrt(jnp.mean(x * x, axis=-1, keepdims=True) + NORM_EPS) * gain


def _resident(shape):
    nd = len(shape)
    return pl.BlockSpec(shape, lambda *_: (0,) * nd, pipeline_mode=pl.Buffered(1))


def _layer_resident(shape, layer):
    nd = len(shape)
    return pl.BlockSpec((1,) + shape, lambda *_: (layer,) + (0,) * nd, pipeline_mode=pl.Buffered(1))


def _params(sem):
    return pltpu.CompilerParams(dimension_semantics=sem, vmem_limit_bytes=VMEM_LIMIT)


def _ada_body(c_ref, w_ref, b_ref, o_ref):
    c = c_ref[...]
    c_act = (c * jax.nn.sigmoid(c)).astype(BF16)
    o_ref[0] = _dot(c_act, w_ref[0].astype(BF16)) + b_ref[0]


def _ada_mod(c_pad, w_ada, b_ada):
    n_out = 6 * D_MODEL
    return pl.pallas_call(
        _ada_body,
        grid=(N_LAYERS, n_out // ADA_TN),
        in_specs=[
            pl.BlockSpec((SUBLANES, D_MODEL), lambda l, j: (0, 0)),
            pl.BlockSpec((1, D_MODEL, ADA_TN), lambda l, j: (l, 0, j)),
            pl.BlockSpec((1, 1, ADA_TN), lambda l, j: (l, 0, j)),
        ],
        out_specs=pl.BlockSpec((1, SUBLANES, ADA_TN), lambda l, j: (l, 0, j)),
        out_shape=jax.ShapeDtypeStruct((N_LAYERS, SUBLANES, n_out), F32),
        compiler_params=_params(("arbitrary", "arbitrary")),
        name="ada_mod",
    )(c_pad, w_ada, b_ada.reshape(N_LAYERS, 1, n_out))


def _proj_body(x_ref, mod_ref, gn_ref, wt_ref, wkr_ref, wf_ref, qn_ref, wuq_ref, kvn_ref, wukv_ref, bf_ref,
               cos_ref, sin_ref, q_out, kv_out, kr_out, att_out, lf_out, h_ref):
    mod = mod_ref[0, 0]
    x = x_ref[...]
    h = _rms(x, gn_ref[0]) * (1.0 + mod[1:2]) + mod[0:1]
    h_ref[...] = h.astype(BF16)

    def proj(r0, width):
        return _dot_nt(h_ref[...], wt_ref[0, r0:r0 + width, :])

    mla_scale = (MLA_NOPE_D + MLA_ROPE_D) ** -0.5 * LOG2_E
    n_nope = MLA_H * MLA_NOPE_D
    n_rope = MLA_H * MLA_ROPE_D

    cq = _rms(proj(R_CQ, MLA_QL), qn_ref[0]).astype(BF16)
    q = _dot(cq, wuq_ref[0])
    q_out[:, :n_nope] = (q[:, :n_nope] * mla_scale).astype(BF16)
    q_rot = q[:, n_nope:n_nope + n_rope] * cos_ref[...] + q[:, n_nope + n_rope:] * sin_ref[...]
    q_out[:, n_nope:] = (q_rot * mla_scale).astype(BF16)

    ckv = _rms(proj(R_CKV, MLA_KVL), kvn_ref[0]).astype(BF16)
    kv_out[...] = _dot(ckv, wukv_ref[0]).astype(BF16)

    kr = _dot_nt(h_ref[...], wkr_ref[0])
    kr_out[...] = (kr[:, :LANES] * cos_ref[:, :LANES] + kr[:, LANES:] * sin_ref[:, :LANES]).astype(BF16)

    lf_out[...] = -_softplus(-(_dot_nt(h_ref[...], wf_ref[0]) + bf_ref[0]))

    sb_scale = SB_D ** -0.5 * LOG2_E
    fox_scale = FOX_D ** -0.5 * LOG2_E
    scales = (sb_scale, 1.0, 1.0, fox_scale, 1.0, 1.0)
    for g, sc in enumerate(scales):
        blk = proj(R_ATT + g * HEAD_GRP, HEAD_GRP)
        if sc != 1.0:
            blk = blk * sc
        att_out[:, g * HEAD_GRP:(g + 1) * HEAD_GRP] = blk.astype(BF16)


def _project(x2, mod4, layer, gn, wt_in, wt_kr, wt_f, qn, wuq, kvn, wukv, bfp, cos_t, sin_t):
    tm = PROJ_TM
    tiles_per_seq = SEQ_LEN // tm
    q_w = MLA_H * (MLA_NOPE_D + MLA_ROPE_D)
    kv_w = MLA_H * (MLA_NOPE_D + MLA_V_D)
    rope_w = MLA_H * MLA_ROPE_D
    row = lambda i: (i, 0)
    return pl.pallas_call(
        _proj_body,
        grid=(N_TOK // tm,),
        in_specs=[
            pl.BlockSpec((tm, D_MODEL), row),
            pl.BlockSpec((1, 1, 6, D_MODEL), lambda i: (layer, i // tiles_per_seq, 0, 0)),
            _layer_resident((1, D_MODEL), layer),
            _layer_resident((IN_W, D_MODEL), layer),
            _layer_resident((2 * LANES, D_MODEL), layer),
            _layer_resident((LANES, D_MODEL), layer),
            _layer_resident((1, MLA_QL), layer),
            _layer_resident((MLA_QL, q_w + rope_w), layer),
            _layer_resident((1, MLA_KVL), layer),
            _layer_resident((MLA_KVL, kv_w), layer),
            _layer_resident((1, LANES), layer),
            pl.BlockSpec((tm, rope_w), lambda i: (i % tiles_per_seq, 0)),
            pl.BlockSpec((tm, rope_w), lambda i: (i % tiles_per_seq, 0)),
        ],
        out_specs=[
            pl.BlockSpec((tm, q_w), row),
            pl.BlockSpec((tm, kv_w), row),
            pl.BlockSpec((tm, LANES), row),
            pl.BlockSpec((tm, ATT_W), row),
            pl.BlockSpec((tm, LANES), row),
        ],
        out_shape=[
            jax.ShapeDtypeStruct((N_TOK, q_w), BF16),
            jax.ShapeDtypeStruct((N_TOK, kv_w), BF16),
            jax.ShapeDtypeStruct((N_TOK, LANES), BF16),
            jax.ShapeDtypeStruct((N_TOK, ATT_W), BF16),
            jax.ShapeDtypeStruct((N_TOK, LANES), F32),
        ],
        scratch_shapes=[pltpu.VMEM((tm, D_MODEL), BF16)],
        compiler_params=_params(("arbitrary",)),
        name="project",
    )(x2, mod4, gn, wt_in, wt_kr, wt_f, qn, wuq, kvn, wukv, bfp, cos_t, sin_t)


def _fgate_body(lf_ref, tri_ref, pq_ref, pk_ref, cq_ref, ck_ref, qa_out, ka_out):
    n_blk = SEQ_LEN // CUM_T
    local = []
    for blk in range(n_blk):
        hi, mid, lo = _split3(lf_ref[blk * CUM_T:(blk + 1) * CUM_T, :] * LOG2_E)
        tri = tri_ref[...]
        local.append(_dot(tri, hi) + _dot(tri, mid) + _dot(tri, lo))
    carry = jnp.zeros((1, LANES), F32)
    for blk in range(n_blk):
        rows = slice(blk * CUM_T, (blk + 1) * CUM_T)
        f_blk = local[blk] + carry
        carry = f_blk[CUM_T - 1:CUM_T, :]
        f_hi, f_mid, f_lo = _split3(f_blk)
        qa = _dot(f_hi, pq_ref[0]) + _dot(f_mid, pq_ref[1]) + _dot(f_lo, pq_ref[2]) + cq_ref[...]
        ka = ck_ref[...] - (_dot(f_hi, pk_ref[0]) + _dot(f_mid, pk_ref[1]) + _dot(f_lo, pk_ref[2]))
        qa_out[rows, :] = qa.astype(BF16)
        ka_out[rows, :] = ka.astype(BF16)


def _fgate_constants():
    tri = np.tril(np.ones((CUM_T, CUM_T), np.float32))
    n_pair = FOX_H // 2
    pq = np.zeros((3, LANES, n_pair * LANES), np.float32)
    pk = np.zeros((3, LANES, n_pair * LANES), np.float32)
    cq = np.zeros((1, n_pair * LANES), np.float32)
    ck = np.zeros((1, n_pair * LANES), np.float32)
    for h in range(FOX_H):
        base = (h // 2) * LANES + (h % 2) * SUBLANES
        for t in range(3):
            pq[t, h, base + t] = 1.0
            pk[t, h, base + 3 + t] = 1.0
            cq[0, base + 3 + t] = 1.0
            ck[0, base + t] = 1.0
    as_bf16 = lambda a: jnp.asarray(a, BF16)
    return as_bf16(tri), as_bf16(pq), as_bf16(pk), jnp.asarray(cq), jnp.asarray(ck)


def _fgate(logf):
    tri, pq, pk, cq, ck = _fgate_constants()
    aug_w = (FOX_H // 2) * LANES
    seq_blk = lambda b: (b, 0)
    return pl.pallas_call(
        _fgate_body,
        grid=(N_BATCH,),
        in_specs=[
            pl.BlockSpec((SEQ_LEN, LANES), seq_blk),
            _resident((CUM_T, CUM_T)),
            _resident((3, LANES, aug_w)),
            _resident((3, LANES, aug_w)),
            _resident((1, aug_w)),
            _resident((1, aug_w)),
        ],
        out_specs=[pl.BlockSpec((SEQ_LEN, aug_w), seq_blk), pl.BlockSpec((SEQ_LEN, aug_w), seq_blk)],
        out_shape=[jax.ShapeDtypeStruct((N_TOK, aug_w), BF16), jax.ShapeDtypeStruct((N_TOK, aug_w), BF16)],
        compiler_params=_params(("arbitrary",)),
        name="fgate",
    )(logf, tri, pq, pk, cq, ck)


def _lane_band(shape, lo, width):
    lane = lax.broadcasted_iota(jnp.int32, shape, len(shape) - 1)
    return (lane >= lo) & (lane < lo + width)


def _rel_positions():
    rows = lax.broadcasted_iota(jnp.int32, (ATT_T, ATT_T), 0)
    cols = lax.broadcasted_iota(jnp.int32, (ATT_T, ATT_T), 1)
    return rows, cols


def _col_block(a, kb):
    return a[:, kb * ATT_T:(kb + 1) * ATT_T]


def _weights_times_v(p_ref, hh, nkb, v):
    p = jnp.concatenate([p_ref[hh, kb] for kb in range(nkb)], axis=1) if nkb > 1 else p_ref[hh, 0]
    return _dot(p, v)


def _softmax_attention(qcat_at, kcat_at, vaug_at, mask_fn, finish, s_ref, p_ref):
    def scores(i):
        nkb = i + 1
        row_max = []
        for hh in range(2):
            s = _dot_nt(qcat_at(i, hh), kcat_at(hh, nkb * ATT_T))
            blocks = [_col_block(s, kb) for kb in range(nkb)]
            blocks[-1] = jnp.where(mask_fn(*_rel_positions()), blocks[-1], -jnp.inf)
            m = None
            for kb in range(nkb):
                s_ref[i % 2, hh, kb] = blocks[kb]
                bm = jnp.maximum(blocks[kb][:, :LANES], blocks[kb][:, LANES:])
                m = bm if m is None else jnp.maximum(m, bm)
            row_max.append(jnp.max(m, axis=-1, keepdims=True))
        return row_max

    row_max = scores(QB_ORDER[0])
    for n, i in enumerate(QB_ORDER):
        nkb = i + 1
        next_max = scores(QB_ORDER[n + 1]) if n + 1 < N_QB else None
        for kb in range(nkb):
            for hh in range(2):
                p_ref[i % 2, hh, kb] = jnp.exp2(s_ref[i % 2, hh, kb] - row_max[hh]).astype(BF16)
        finish(i, [_weights_times_v(p_ref.at[i % 2], hh, nkb, vaug_at(hh, nkb * ATT_T)) for hh in range(2)])
        row_max = next_max


def _pair_norm(o, gain, head_d):
    first = _lane_band(o.shape, 0, head_d)
    sq = o * o
    ss_a = jnp.sum(jnp.where(first, sq, 0.0), axis=-1, keepdims=True)
    ss_b = jnp.sum(sq, axis=-1, keepdims=True) - ss_a
    inv = jnp.where(first, lax.rsqrt(ss_a / head_d + NORM_EPS), lax.rsqrt(ss_b / head_d + NORM_EPS))
    return o * inv * gain


_SCORE_SCRATCH = [
    pltpu.VMEM((2, 2, N_QB, ATT_T, ATT_T), F32),
    pltpu.VMEM((2, 2, N_QB, ATT_T, ATT_T), BF16),
]
ATT_PAIRS = 4


def _cast_along(w, layer):
    _, n_rows, n_cols = w.shape
    slab = n_rows // (N_BATCH * ATT_PAIRS)
    assert slab * N_BATCH * ATT_PAIRS == n_rows and slab % (2 * SUBLANES) == 0
    in_spec = pl.BlockSpec((1, slab, n_cols), lambda b, p: (layer, b * ATT_PAIRS + p, 0))
    out_spec = pl.BlockSpec((slab, n_cols), lambda b, p: (b * ATT_PAIRS + p, 0))
    return in_spec, out_spec, jax.ShapeDtypeStruct((n_rows, n_cols), BF16)


def _mla_body(qn_ref, qr_ref, kn_ref, v_ref, kr_ref, g_ref, w_ref, o_ref, wb_ref, kcat_ref, vaug_ref,
              s_ref, p_ref):
    wb_ref[...] = w_ref[0].astype(BF16)
    d = MLA_NOPE_D
    for hh in range(2):
        kcat_ref[hh, :, :d] = kn_ref[:, hh * d:(hh + 1) * d]
        kcat_ref[hh, :, d:] = kr_ref[...]
        vaug_ref[hh, :, :d] = v_ref[:, hh * d:(hh + 1) * d]
        vaug_ref[hh, :, d:] = jnp.ones((SEQ_LEN, LANES), BF16)

    def chunk_mask(rows, cols):
        return cols < ((rows >> 6) + 1) * CHUNK_LEN

    def qcat_at(i, hh):
        r = slice(i * ATT_T, (i + 1) * ATT_T)
        qr = qr_ref[r, :]
        qr = jnp.where(_lane_band(qr.shape, hh * MLA_ROPE_D, MLA_ROPE_D), qr, jnp.zeros_like(qr))
        return jnp.concatenate([qn_ref[r, hh * d:(hh + 1) * d], qr], axis=1)

    def finish(i, outs):
        r = slice(i * ATT_T, (i + 1) * ATT_T)
        for hh in range(2):
            o = outs[hh][:, :d] / outs[hh][:, d:]
            o_ref[r, hh * d:(hh + 1) * d] = _rms(o, g_ref[0, :, hh * d:(hh + 1) * d]).astype(BF16)

    _softmax_attention(
        qcat_at, lambda hh, klen: kcat_ref[hh, :klen, :], lambda hh, klen: vaug_ref[hh, :klen, :],
        chunk_mask, finish, s_ref, p_ref)


def _mla_attention(q_mla, kv_mla, krope, out_gain, layer, w_cast):
    n_pair = MLA_H // 2
    assert n_pair == ATT_PAIRS
    pair_w = 2 * MLA_NOPE_D
    w_in_spec, w_out_spec, w_out_shape = _cast_along(w_cast, layer)
    return pl.pallas_call(
        _mla_body,
        grid=(N_BATCH, n_pair),
        in_specs=[
            pl.BlockSpec((SEQ_LEN, pair_w), lambda b, p: (b, p)),
            pl.BlockSpec((SEQ_LEN, LANES), lambda b, p: (b, MLA_H * MLA_NOPE_D // LANES + p)),
            pl.BlockSpec((SEQ_LEN, pair_w), lambda b, p: (b, p)),
            pl.BlockSpec((SEQ_LEN, pair_w), lambda b, p: (b, n_pair + p)),
            pl.BlockSpec((SEQ_LEN, LANES), lambda b, p: (b, 0)),
            pl.BlockSpec((1, 1, pair_w), lambda b, p: (layer, 0, p)),
            w_in_spec,
        ],
        out_specs=[pl.BlockSpec((SEQ_LEN, pair_w), lambda b, p: (b, p)), w_out_spec],
        out_shape=[jax.ShapeDtypeStruct((N_TOK, MLA_H * MLA_V_D), BF16), w_out_shape],
        scratch_shapes=[
            pltpu.VMEM((2, SEQ_LEN, 2 * LANES), BF16),
            pltpu.VMEM((2, SEQ_LEN, 2 * LANES), BF16),
        ] + _SCORE_SCRATCH,
        compiler_params=_params(("arbitrary", "arbitrary")),
        name="mla_attention",
    )(q_mla, q_mla, kv_mla, kv_mla, krope, out_gain, w_cast)


def _fox_body(q_ref, k_ref, v_ref, qa_ref, ka_ref, g_ref, w_ref, o_ref, wb_ref, kcat_ref, vaug_ref,
              s_ref, p_ref):
    wb_ref[...] = w_ref[0].astype(BF16)
    kcat_ref[:, :LANES] = k_ref[...]
    kcat_ref[:, LANES:] = ka_ref[...]
    vaug_ref[:, :LANES] = v_ref[...]
    vaug_ref[:, LANES:] = jnp.ones((SEQ_LEN, LANES), BF16)

    def causal(rows, cols):
        return cols <= rows

    def qcat_at(i, hh):
        r = slice(i * ATT_T, (i + 1) * ATT_T)
        q = q_ref[r, :]
        qa = qa_ref[r, :]
        zero = jnp.zeros_like(q)
        return jnp.concatenate([
            jnp.where(_lane_band(q.shape, hh * FOX_D, FOX_D), q, zero),
            jnp.where(_lane_band(qa.shape, hh * SUBLANES, SUBLANES), qa, zero)], axis=1)

    def finish(i, outs):
        heads = [outs[hh][:, :LANES] / outs[hh][:, LANES:] for hh in range(2)]
        o = jnp.where(_lane_band(heads[0].shape, 0, FOX_D), heads[0], heads[1])
        o_ref[i * ATT_T:(i + 1) * ATT_T, :] = _pair_norm(o, g_ref[0], FOX_D).astype(BF16)

    _softmax_attention(
        qcat_at, lambda hh, klen: kcat_ref[:klen, :], lambda hh, klen: vaug_ref[:klen, :],
        causal, finish, s_ref, p_ref)


def _fox_attention(p_att, qaug, kaug, out_gain, layer, w_cast):
    n_pair = FOX_H // 2
    assert n_pair == ATT_PAIRS
    grp = HEAD_GRP // LANES
    gain_blk = (MLA_H * MLA_V_D + SB_H * SB_D) // LANES
    w_in_spec, w_out_spec, w_out_shape = _cast_along(w_cast, layer)
    return pl.pallas_call(
        _fox_body,
        grid=(N_BATCH, n_pair),
        in_specs=[
            pl.BlockSpec((SEQ_LEN, LANES), lambda b, p: (b, 3 * grp + p)),
            pl.BlockSpec((SEQ_LEN, LANES), lambda b, p: (b, 4 * grp + p)),
            pl.BlockSpec((SEQ_LEN, LANES), lambda b, p: (b, 5 * grp + p)),
            pl.BlockSpec((SEQ_LEN, LANES), lambda b, p: (b, p)),
            pl.BlockSpec((SEQ_LEN, LANES), lambda b, p: (b, p)),
            pl.BlockSpec((1, 1, LANES), lambda b, p: (layer, 0, gain_blk + p)),
            w_in_spec,
        ],
        out_specs=[pl.BlockSpec((SEQ_LEN, LANES), lambda b, p: (b, p)), w_out_spec],
        out_shape=[jax.ShapeDtypeStruct((N_TOK, FOX_H * FOX_D), BF16), w_out_shape],
        scratch_shapes=[
            pltpu.VMEM((SEQ_LEN, 2 * LANES), BF16),
            pltpu.VMEM((SEQ_LEN, 2 * LANES), BF16),
        ] + _SCORE_SCRATCH,
        compiler_params=_params(("arbitrary", "arbitrary")),
        name="fox_attention",
    )(p_att, p_att, p_att, qaug, kaug, out_gain, w_cast)


def _sb_body(q_ref, k_ref, v_ref, tri_ref, g_ref, w_ref, o_ref, wb_ref, u_ref, p_ref):
    wb_ref[...] = w_ref[0].astype(BF16)

    def score_tiles(i):
        q = q_ref[i * ATT_T:(i + 1) * ATT_T, :]
        zero = jnp.zeros_like(q)
        q_heads = [jnp.where(_lane_band(q.shape, hh * SB_D, SB_D), q, zero) for hh in range(2)]

        def tile(hh, kb):
            u_ref[i % 2, hh, kb] = _dot_nt(q_heads[hh], k_ref[kb * ATT_T:(kb + 1) * ATT_T, :])

        return [functools.partial(tile, hh, kb) for kb in range(i + 1) for hh in range(2)]

    def weights(i, pending):
        nkb = i + 1
        right = [None, None]
        for kb in reversed(range(nkb)):
            on_diagonal = kb == nkb - 1
            for hh in range(2):
                if pending:
                    pending.pop()()
                u = u_ref[i % 2, hh, kb]
                drop = jnp.maximum(u, 0.0) + jnp.log2(1.0 + jnp.exp2(-jnp.abs(u)))
                if on_diagonal:
                    rows, cols = _rel_positions()
                    strict = cols < rows
                    drop = jnp.where(strict, drop, 0.0)
                log_w = (u - drop) + _dot(drop.astype(BF16), tri_ref[...])
                if right[hh] is not None:
                    log_w = log_w - right[hh]
                w = jnp.exp2(log_w)
                if on_diagonal:
                    w = jnp.where(strict, w, 0.0)
                p_ref[i % 2, hh, kb] = w.astype(BF16)
                total = jnp.sum(drop, axis=-1, keepdims=True)
                right[hh] = total if right[hh] is None else right[hh] + total

    for tile in score_tiles(QB_ORDER[0]):
        tile()
    for n, i in enumerate(QB_ORDER):
        nkb = i + 1
        pending = score_tiles(QB_ORDER[n + 1]) if n + 1 < N_QB else []
        weights(i, pending)
        while pending:
            pending.pop()()
        heads = [_weights_times_v(p_ref.at[i % 2], hh, nkb, v_ref[:nkb * ATT_T, :]) for hh in range(2)]
        o = jnp.where(_lane_band(heads[0].shape, 0, SB_D), heads[0], heads[1])
        o_ref[i * ATT_T:(i + 1) * ATT_T, :] = _pair_norm(o, g_ref[0], SB_D).astype(BF16)


def _sb_attention(p_att, out_gain, layer, w_cast):
    n_pair = SB_H // 2
    assert n_pair == ATT_PAIRS
    grp = HEAD_GRP // LANES
    gain_blk = MLA_H * MLA_V_D // LANES
    w_in_spec, w_out_spec, w_out_shape = _cast_along(w_cast, layer)
    tri = jnp.asarray(-np.tril(np.ones((ATT_T, ATT_T), np.float32), -1), BF16)
    return pl.pallas_call(
        _sb_body,
        grid=(N_BATCH, n_pair),
        in_specs=[
            pl.BlockSpec((SEQ_LEN, LANES), lambda b, p: (b, p)),
            pl.BlockSpec((SEQ_LEN, LANES), lambda b, p: (b, grp + p)),
            pl.BlockSpec((SEQ_LEN, LANES), lambda b, p: (b, 2 * grp + p)),
            _resident((ATT_T, ATT_T)),
            pl.BlockSpec((1, 1, LANES), lambda b, p: (layer, 0, gain_blk + p)),
            w_in_spec,
        ],
        out_specs=[pl.BlockSpec((SEQ_LEN, LANES), lambda b, p: (b, p)), w_out_spec],
        out_shape=[jax.ShapeDtypeStruct((N_TOK, SB_H * SB_D), BF16), w_out_shape],
        scratch_shapes=_SCORE_SCRATCH,
        compiler_params=_params(("arbitrary", "arbitrary")),
        name="sb_attention",
    )(p_att, p_att, p_att, tri, out_gain, w_cast)


def _outproj_body(x_ref, mod_ref, om_ref, os_ref, of_ref, w_ref, o_ref):
    n_mla = MLA_H * MLA_V_D
    n_sb = SB_H * SB_D
    y = (_dot(om_ref[...], w_ref[:n_mla, :]) + _dot(os_ref[...], w_ref[n_mla:n_mla + n_sb, :])
         + _dot(of_ref[...], w_ref[n_mla + n_sb:, :]))
    o_ref[...] = x_ref[...] + mod_ref[0, 0][2:3] * y


def _out_project(x2, mod4, layer, o_mla, o_sb, o_fx, w_out):
    tm = PROJ_TM
    tiles_per_seq = SEQ_LEN // tm
    row = lambda i: (i, 0)
    return pl.pallas_call(
        _outproj_body,
        grid=(N_TOK // tm,),
        in_specs=[
            pl.BlockSpec((tm, D_MODEL), row),
            pl.BlockSpec((1, 1, 6, D_MODEL), lambda i: (layer, i // tiles_per_seq, 0, 0)),
            pl.BlockSpec((tm, MLA_H * MLA_V_D), row),
            pl.BlockSpec((tm, SB_H * SB_D), row),
            pl.BlockSpec((tm, FOX_H * FOX_D), row),
            _resident((D_MODEL, D_MODEL)),
        ],
        out_specs=pl.BlockSpec((tm, D_MODEL), row),
        out_shape=jax.ShapeDtypeStruct((N_TOK, D_MODEL), F32),
        compiler_params=_params(("arbitrary",)),
        name="out_project",
    )(x2, mod4, o_mla, o_sb, o_fx, w_out)


def _ffn_body(x_ref, mod_ref, gn_ref, w1_ref, w2_ref, gf_ref, o_ref, h_ref, *, final):
    j = pl.program_id(1)
    mod = mod_ref[0, 0]

    @pl.when(j == 0)
    def _():
        h = _rms(x_ref[...], gn_ref[0]) * (1.0 + mod[4:5]) + mod[3:4]
        h_ref[...] = h.astype(BF16)
        o_ref[...] = jnp.zeros(o_ref.shape, F32)

    a = jnp.maximum(_dot(h_ref[...], w1_ref[...]), 0.0)
    o_ref[...] += _dot((a * a).astype(BF16), w2_ref[...])

    @pl.when(j == pl.num_programs(1) - 1)
    def _():
        y = x_ref[...] + mod[5:6] * o_ref[...]
        o_ref[...] = _rms(y, gf_ref[...]) if final else y


def _ffn(x2, mod4, layer, gn, w1, w2, g_final, final):
    tm, tf = FFN_TM, FFN_TF
    tiles_per_seq = SEQ_LEN // tm
    return pl.pallas_call(
        functools.partial(_ffn_body, final=final),
        grid=(N_TOK // tm, FF_DIM // tf),
        in_specs=[
            pl.BlockSpec((tm, D_MODEL), lambda i, j: (i, 0)),
            pl.BlockSpec((1, 1, 6, D_MODEL), lambda i, j: (layer, i // tiles_per_seq, 0, 0)),
            pl.BlockSpec((1, 1, D_MODEL), lambda i, j: (layer, 0, 0)),
            pl.BlockSpec((D_MODEL, tf), lambda i, j: (0, j)),
            pl.BlockSpec((tf, D_MODEL), lambda i, j: (j, 0)),
            pl.BlockSpec((1, D_MODEL), lambda i, j: (0, 0)),
        ],
        out_specs=pl.BlockSpec((tm, D_MODEL), lambda i, j: (i, 0)),
        out_shape=jax.ShapeDtypeStruct((N_TOK, D_MODEL), F32),
        scratch_shapes=[pltpu.VMEM((tm, D_MODEL), BF16)],
        compiler_params=_params(("arbitrary", "arbitrary")),
        name="ffn",
    )(x2, mod4, gn, w1, w2, g_final)


def _swap_halves(w):
    half = w.shape[-1] // 2
    return jnp.concatenate([w[..., half:], w[..., :half]], axis=-1)


def _pack_w_in(w):
    wt = jnp.swapaxes(w, 1, 2).astype(BF16)
    kr = wt[:, R_KR:R_ATT]
    kr_sw = jnp.concatenate([kr[:, MLA_ROPE_D // 2:], kr[:, :MLA_ROPE_D // 2]], axis=1)
    wt_kr = jnp.concatenate([kr, kr, kr_sw, kr_sw], axis=1)
    wt_f = jnp.pad(wt[:, R_F:], ((0, 0), (0, LANES - FOX_H), (0, 0)))
    return wt, wt_kr, wt_f


def _pack_w_uq(w):
    w4 = w.astype(BF16).reshape(N_LAYERS, MLA_QL, MLA_H, MLA_NOPE_D + MLA_ROPE_D)
    flat = lambda a: a.reshape(N_LAYERS, MLA_QL, -1)
    rope = w4[..., MLA_NOPE_D:]
    return jnp.concatenate([flat(w4[..., :MLA_NOPE_D]), flat(rope), flat(_swap_halves(rope))], axis=-1)


def _pack_w_ukv(w):
    w4 = w.astype(BF16).reshape(N_LAYERS, MLA_KVL, MLA_H, MLA_NOPE_D + MLA_V_D)
    flat = lambda a: a.reshape(N_LAYERS, MLA_KVL, -1)
    return jnp.concatenate([flat(w4[..., :MLA_NOPE_D]), flat(w4[..., MLA_NOPE_D:])], axis=-1)


def _rope_tables():
    pos = jnp.arange(SEQ_LEN, dtype=F32)
    inv = ROPE_THETA ** (-jnp.arange(0, MLA_ROPE_D, 2, dtype=F32) / MLA_ROPE_D)
    ang = pos[:, None] * inv[None, :]
    cos, sin = jnp.cos(ang), jnp.sin(ang)
    cos_t = jnp.tile(jnp.concatenate([cos, cos], axis=1), (1, MLA_H))
    sin_t = jnp.tile(jnp.concatenate([-sin, sin], axis=1), (1, MLA_H))
    return cos_t, sin_t


def kernel(x, c, w_ada, b_ada, norm_mix, w_in, q_norm, w_uq, kv_norm, w_ukv, b_forget, out_norm,
           w_out, norm_ffn, w_ff1, w_ff2, final_norm):
    assert x.shape == (N_BATCH, SEQ_LEN, D_MODEL) and c.shape == (N_BATCH, D_MODEL)
    cos_t, sin_t = _rope_tables()
    c_pad = jnp.pad(c, ((0, SUBLANES - N_BATCH), (0, 0)))
    mod = _ada_mod(c_pad, w_ada, b_ada)
    mod4 = mod[:, :N_BATCH].reshape(N_LAYERS, N_BATCH, 6, D_MODEL)

    rows = lambda v: v.reshape(N_LAYERS, 1, -1)
    assert w_in.shape == (N_LAYERS, D_MODEL, IN_W)
    wt_in, wt_kr, wt_f = _pack_w_in(w_in)
    wuq_p, wukv_p = _pack_w_uq(w_uq), _pack_w_ukv(w_ukv)
    bf_pad = jnp.pad(b_forget, ((0, 0), (0, LANES - FOX_H))).reshape(N_LAYERS, 1, LANES)
    out_gain = rows(out_norm)

    x2 = x.reshape(N_TOK, D_MODEL)
    for l in range(N_LAYERS):
        q_mla, kv_mla, krope, p_att, logf = _project(
            x2, mod4, l, rows(norm_mix), wt_in, wt_kr, wt_f, rows(q_norm), wuq_p, rows(kv_norm), wukv_p, bf_pad,
            cos_t, sin_t)
        qaug, kaug = _fgate(logf)
        o_mla, wff1_b = _mla_attention(q_mla, kv_mla, krope, out_gain, l, w_ff1)
        o_sb, wout_b = _sb_attention(p_att, out_gain, l, w_out)
        o_fx, wff2_b = _fox_attention(p_att, qaug, kaug, out_gain, l, w_ff2)
        x2 = _out_project(x2, mod4, l, o_mla, o_sb, o_fx, wout_b)
        x2 = _ffn(x2, mod4, l, rows(norm_ffn), wff1_b, wff2_b, final_norm.reshape(1, -1),
                  final=(l == N_LAYERS - 1))
    return x2.reshape(N_BATCH, SEQ_LEN, D_MODEL)
```

```python
import functools

import numpy as np
import jax
import jax.numpy as jnp
from jax import lax
from jax.experimental import pallas as pl
from jax.experimental.pallas import tpu as pltpu

F32 = jnp.float32
BF16 = jnp.bfloat16

D_MODEL = 2048
N_BATCH = 4
SEQ_LEN = 2048
N_LAYERS = 2
CHUNK_LEN = 64
NORM_EPS = 1e-6
LOG2_E = 1.4426950408889634
MLA_H = 8
MLA_QL = 512
MLA_KVL = 256
MLA_NOPE_D = 128
MLA_ROPE_D = 64
MLA_V_D = 128
ROPE_THETA = 10000.0
SB_H = 8
SB_D = 64
FOX_H = 8
FOX_D = 64
FF_DIM = 4 * D_MODEL
N_TOK = N_BATCH * SEQ_LEN

LANES = 128
SUBLANES = 8
VMEM_LIMIT = 56 * 1024 * 1024

R_CQ = 0
R_CKV = R_CQ + MLA_QL
R_KR = R_CKV + MLA_KVL
R_ATT = R_KR + MLA_ROPE_D
ATT_W = 3 * SB_H * SB_D + 3 * FOX_H * FOX_D
R_F = R_ATT + ATT_W
IN_W = R_F + FOX_H
HEAD_GRP = SB_H * SB_D

PROJ_TM = 512
FFN_TM = 1024
FFN_TF = 512
ADA_TN = 1024
ATT_T = 256
N_QB = SEQ_LEN // ATT_T
QB_ORDER = tuple(reversed(range(N_QB)))
CUM_T = 256


def _dot(a, b):
    return jnp.dot(a, b, preferred_element_type=F32)


def _dot_nt(a, b):
    return lax.dot_general(a, b, (((1,), (1,)), ((), ())), preferred_element_type=F32)


def _split3(x):
    hi = x.astype(BF16)
    r1 = x - hi.astype(F32)
    mid = r1.astype(BF16)
    lo = (r1 - mid.astype(F32)).astype(BF16)
    return hi, mid, lo


def _softplus(z):
    return jnp.maximum(z, 0.0) + jnp.log(1.0 + jnp.exp(-jnp.abs(z)))


def _rms(x, gain):
    return x * lax.rsqrt(jnp.mean(x * x, axis=-1, keepdims=True) + NORM_EPS) * gain


def _inv_rms(x_ref):
    x = x_ref[...]
    return lax.rsqrt(jnp.mean(x * x, axis=-1, keepdims=True) + NORM_EPS)


def _modulated_norm(x_ref, norm_gain, mod_scale, mod_shift, h_ref):
    inv = _inv_rms(x_ref)
    gain = norm_gain * (1.0 + mod_scale)
    h_ref[...] = (x_ref[...] * inv * gain + mod_shift).astype(BF16)


def _resident(shape):
    nd = len(shape)
    return pl.BlockSpec(shape, lambda *_: (0,) * nd, pipeline_mode=pl.Buffered(1))


def _layer_resident(shape, layer):
    nd = len(shape)
    return pl.BlockSpec((1,) + shape, lambda *_: (layer,) + (0,) * nd, pipeline_mode=pl.Buffered(1))


def _params(sem):
    return pltpu.CompilerParams(dimension_semantics=sem, vmem_limit_bytes=VMEM_LIMIT)


def _ada_body(c_ref, w_ref, b_ref, o_ref):
    c = c_ref[...]
    c_act = (c * jax.nn.sigmoid(c)).astype(BF16)
    o_ref[0] = _dot(c_act, w_ref[0].astype(BF16)) + b_ref[0]


def _ada_mod(c_pad, w_ada, b_ada):
    n_out = 6 * D_MODEL
    return pl.pallas_call(
        _ada_body,
        grid=(N_LAYERS, n_out // ADA_TN),
        in_specs=[
            pl.BlockSpec((SUBLANES, D_MODEL), lambda l, j: (0, 0)),
            pl.BlockSpec((1, D_MODEL, ADA_TN), lambda l, j: (l, 0, j)),
            pl.BlockSpec((1, 1, ADA_TN), lambda l, j: (l, 0, j)),
        ],
        out_specs=pl.BlockSpec((1, SUBLANES, ADA_TN), lambda l, j: (l, 0, j)),
        out_shape=jax.ShapeDtypeStruct((N_LAYERS, SUBLANES, n_out), F32),
        compiler_params=_params(("arbitrary", "arbitrary")),
        name="ada_mod",
    )(c_pad, w_ada, b_ada.reshape(N_LAYERS, 1, n_out))


def _proj_body(x_ref, mod_ref, gn_ref, wt_ref, wkr_ref, wf_ref, qn_ref, wuq_ref, kvn_ref, wukv_ref, bf_ref,
               cos_ref, sin_ref, q_out, kv_out, kr_out, att_out, lf_out, h_ref):
    mod = mod_ref[0, 0]
    _modulated_norm(x_ref, gn_ref[0], mod[1:2], mod[0:1], h_ref)

    def proj(r0, width):
        return _dot_nt(h_ref[...], wt_ref[0, r0:r0 + width, :])

    mla_scale = (MLA_NOPE_D + MLA_ROPE_D) ** -0.5 * LOG2_E
    n_nope = MLA_H * MLA_NOPE_D
    n_rope = MLA_H * MLA_ROPE_D

    cq = _rms(proj(R_CQ, MLA_QL), qn_ref[0]).astype(BF16)
    q = _dot(cq, wuq_ref[0])
    q_out[:, :n_nope] = (q[:, :n_nope] * mla_scale).astype(BF16)
    q_rot = q[:, n_nope:n_nope + n_rope] * cos_ref[...] + q[:, n_nope + n_rope:] * sin_ref[...]
    q_out[:, n_nope:] = (q_rot * mla_scale).astype(BF16)

    ckv = _rms(proj(R_CKV, MLA_KVL), kvn_ref[0]).astype(BF16)
    kv_out[...] = _dot(ckv, wukv_ref[0]).astype(BF16)

    kr = _dot_nt(h_ref[...], wkr_ref[0])
    kr_out[...] = (kr[:, :LANES] * cos_ref[:, :LANES] + kr[:, LANES:] * sin_ref[:, :LANES]).astype(BF16)

    lf_out[...] = -_softplus(-(_dot_nt(h_ref[...], wf_ref[0]) + bf_ref[0]))

    sb_scale = SB_D ** -0.5 * LOG2_E
    fox_scale = FOX_D ** -0.5 * LOG2_E
    scales = (sb_scale, 1.0, 1.0, fox_scale, 1.0, 1.0)
    for g, sc in enumerate(scales):
        blk = proj(R_ATT + g * HEAD_GRP, HEAD_GRP)
        if sc != 1.0:
            blk = blk * sc
        att_out[:, g * HEAD_GRP:(g + 1) * HEAD_GRP] = blk.astype(BF16)


def _project(x2, mod4, layer, gn, wt_in, wt_kr, wt_f, qn, wuq, kvn, wukv, bfp, cos_t, sin_t):
    tm = PROJ_TM
    tiles_per_seq = SEQ_LEN // tm
    q_w = MLA_H * (MLA_NOPE_D + MLA_ROPE_D)
    kv_w = MLA_H * (MLA_NOPE_D + MLA_V_D)
    rope_w = MLA_H * MLA_ROPE_D
    row = lambda i: (i, 0)
    return pl.pallas_call(
        _proj_body,
        grid=(N_TOK // tm,),
        in_specs=[
            pl.BlockSpec((tm, D_MODEL), row),
            pl.BlockSpec((1, 1, 6, D_MODEL), lambda i: (layer, i // tiles_per_seq, 0, 0)),
            _layer_resident((1, D_MODEL), layer),
            _layer_resident((IN_W, D_MODEL), layer),
            _layer_resident((2 * LANES, D_MODEL), layer),
            _layer_resident((LANES, D_MODEL), layer),
            _layer_resident((1, MLA_QL), layer),
            _layer_resident((MLA_QL, q_w + rope_w), layer),
            _layer_resident((1, MLA_KVL), layer),
            _layer_resident((MLA_KVL, kv_w), layer),
            _layer_resident((1, LANES), layer),
            pl.BlockSpec((tm, rope_w), lambda i: (i % tiles_per_seq, 0)),
            pl.BlockSpec((tm, rope_w), lambda i: (i % tiles_per_seq, 0)),
        ],
        out_specs=[
            pl.BlockSpec((tm, q_w), row),
            pl.BlockSpec((tm, kv_w), row),
            pl.BlockSpec((tm, LANES), row),
            pl.BlockSpec((tm, ATT_W), row),
            pl.BlockSpec((tm, LANES), row),
        ],
        out_shape=[
            jax.ShapeDtypeStruct((N_TOK, q_w), BF16),
            jax.ShapeDtypeStruct((N_TOK, kv_w), BF16),
            jax.ShapeDtypeStruct((N_TOK, LANES), BF16),
            jax.ShapeDtypeStruct((N_TOK, ATT_W), BF16),
            jax.ShapeDtypeStruct((N_TOK, LANES), F32),
        ],
        scratch_shapes=[pltpu.VMEM((tm, D_MODEL), BF16)],
        compiler_params=_params(("arbitrary",)),
        name="project",
    )(x2, mod4, gn, wt_in, wt_kr, wt_f, qn, wuq, kvn, wukv, bfp, cos_t, sin_t)


def _fgate_body(lf_ref, tri_ref, pq_ref, pk_ref, cq_ref, ck_ref, qa_out, ka_out):
    n_blk = SEQ_LEN // CUM_T
    local = []
    for blk in range(n_blk):
        hi, mid, lo = _split3(lf_ref[blk * CUM_T:(blk + 1) * CUM_T, :] * LOG2_E)
        tri = tri_ref[...]
        local.append(_dot(tri, hi) + _dot(tri, mid) + _dot(tri, lo))
    carry = jnp.zeros((1, LANES), F32)
    for blk in range(n_blk):
        rows = slice(blk * CUM_T, (blk + 1) * CUM_T)
        f_blk = local[blk] + carry
        carry = f_blk[CUM_T - 1:CUM_T, :]
        f_hi, f_mid, f_lo = _split3(f_blk)
        qa = _dot(f_hi, pq_ref[0]) + _dot(f_mid, pq_ref[1]) + _dot(f_lo, pq_ref[2]) + cq_ref[...]
        ka = ck_ref[...] - (_dot(f_hi, pk_ref[0]) + _dot(f_mid, pk_ref[1]) + _dot(f_lo, pk_ref[2]))
        qa_out[rows, :] = qa.astype(BF16)
        ka_out[rows, :] = ka.astype(BF16)


def _fgate_constants():
    tri = np.tril(np.ones((CUM_T, CUM_T), np.float32))
    n_pair = FOX_H // 2
    pq = np.zeros((3, LANES, n_pair * LANES), np.float32)
    pk = np.zeros((3, LANES, n_pair * LANES), np.float32)
    cq = np.zeros((1, n_pair * LANES), np.float32)
    ck = np.zeros((1, n_pair * LANES), np.float32)
    for h in range(FOX_H):
        base = (h // 2) * LANES + (h % 2) * SUBLANES
        for t in range(3):
            pq[t, h, base + t] = 1.0
            pk[t, h, base + 3 + t] = 1.0
            cq[0, base + 3 + t] = 1.0
            ck[0, base + t] = 1.0
    as_bf16 = lambda a: jnp.asarray(a, BF16)
    return as_bf16(tri), as_bf16(pq), as_bf16(pk), jnp.asarray(cq), jnp.asarray(ck)


def _fgate(logf):
    tri, pq, pk, cq, ck = _fgate_constants()
    aug_w = (FOX_H // 2) * LANES
    seq_blk = lambda b: (b, 0)
    return pl.pallas_call(
        _fgate_body,
        grid=(N_BATCH,),
        in_specs=[
            pl.BlockSpec((SEQ_LEN, LANES), seq_blk),
            _resident((CUM_T, CUM_T)),
            _resident((3, LANES, aug_w)),
            _resident((3, LANES, aug_w)),
            _resident((1, aug_w)),
            _resident((1, aug_w)),
        ],
        out_specs=[pl.BlockSpec((SEQ_LEN, aug_w), seq_blk), pl.BlockSpec((SEQ_LEN, aug_w), seq_blk)],
        out_shape=[jax.ShapeDtypeStruct((N_TOK, aug_w), BF16), jax.ShapeDtypeStruct((N_TOK, aug_w), BF16)],
        compiler_params=_params(("arbitrary",)),
        name="fgate",
    )(logf, tri, pq, pk, cq, ck)


def _lane_band(shape, lo, width):
    lane = lax.broadcasted_iota(jnp.int32, shape, len(shape) - 1)
    return (lane >= lo) & (lane < lo + width)


def _rel_positions():
    rows = lax.broadcasted_iota(jnp.int32, (ATT_T, ATT_T), 0)
    cols = lax.broadcasted_iota(jnp.int32, (ATT_T, ATT_T), 1)
    return rows, cols


def _col_block(a, kb):
    return a[:, kb * ATT_T:(kb + 1) * ATT_T]


def _weights_times_v(p_ref, hh, nkb, v):
    p = jnp.concatenate([p_ref[hh, kb] for kb in range(nkb)], axis=1) if nkb > 1 else p_ref[hh, 0]
    return _dot(p, v)


def _softmax_attention(qcat_at, kcat_at, vaug_at, mask_fn, finish, s_ref, p_ref):
    def scores(i):
        nkb = i + 1
        row_max = []
        for hh in range(2):
            s = _dot_nt(qcat_at(i, hh), kcat_at(hh, nkb * ATT_T))
            blocks = [_col_block(s, kb) for kb in range(nkb)]
            blocks[-1] = jnp.where(mask_fn(*_rel_positions()), blocks[-1], -jnp.inf)
            m = None
            for kb in range(nkb):
                s_ref[i % 2, hh, kb] = blocks[kb]
                bm = jnp.maximum(blocks[kb][:, :LANES], blocks[kb][:, LANES:])
                m = bm if m is None else jnp.maximum(m, bm)
            row_max.append(jnp.max(m, axis=-1, keepdims=True))
        return row_max

    row_max = scores(QB_ORDER[0])
    for n, i in enumerate(QB_ORDER):
        nkb = i + 1
        next_max = scores(QB_ORDER[n + 1]) if n + 1 < N_QB else None
        for kb in range(nkb):
            for hh in range(2):
                p_ref[i % 2, hh, kb] = jnp.exp2(s_ref[i % 2, hh, kb] - row_max[hh]).astype(BF16)
        finish(i, [_weights_times_v(p_ref.at[i % 2], hh, nkb, vaug_at(hh, nkb * ATT_T)) for hh in range(2)])
        row_max = next_max


def _pair_norm(o, gain, head_d):
    first = _lane_band(o.shape, 0, head_d)
    sq = o * o
    ss_a = jnp.sum(jnp.where(first, sq, 0.0), axis=-1, keepdims=True)
    ss_b = jnp.sum(sq, axis=-1, keepdims=True) - ss_a
    inv = jnp.where(first, lax.rsqrt(ss_a / head_d + NORM_EPS), lax.rsqrt(ss_b / head_d + NORM_EPS))
    return o * inv * gain


_SCORE_SCRATCH = [
    pltpu.VMEM((2, 2, N_QB, ATT_T, ATT_T), F32),
    pltpu.VMEM((2, 2, N_QB, ATT_T, ATT_T), BF16),
]
ATT_PAIRS = 4


def _cast_along(w, layer):
    _, n_rows, n_cols = w.shape
    slab = n_rows // (N_BATCH * ATT_PAIRS)
    assert slab * N_BATCH * ATT_PAIRS == n_rows and slab % (2 * SUBLANES) == 0
    in_spec = pl.BlockSpec((1, slab, n_cols), lambda b, p: (layer, b * ATT_PAIRS + p, 0))
    out_spec = pl.BlockSpec((slab, n_cols), lambda b, p: (b * ATT_PAIRS + p, 0))
    return in_spec, out_spec, jax.ShapeDtypeStruct((n_rows, n_cols), BF16)


def _mla_body(qn_ref, qr_ref, kn_ref, v_ref, kr_ref, g_ref, w_ref, o_ref, wb_ref, kcat_ref, vaug_ref,
              s_ref, p_ref):
    wb_ref[...] = w_ref[0].astype(BF16)
    d = MLA_NOPE_D
    for hh in range(2):
        kcat_ref[hh, :, :d] = kn_ref[:, hh * d:(hh + 1) * d]
        kcat_ref[hh, :, d:] = kr_ref[...]
        vaug_ref[hh, :, :d] = v_ref[:, hh * d:(hh + 1) * d]
        vaug_ref[hh, :, d:] = jnp.ones((SEQ_LEN, LANES), BF16)

    def chunk_mask(rows, cols):
        return cols < ((rows >> 6) + 1) * CHUNK_LEN

    def qcat_at(i, hh):
        r = slice(i * ATT_T, (i + 1) * ATT_T)
        qr = qr_ref[r, :]
        qr = jnp.where(_lane_band(qr.shape, hh * MLA_ROPE_D, MLA_ROPE_D), qr, jnp.zeros_like(qr))
        return jnp.concatenate([qn_ref[r, hh * d:(hh + 1) * d], qr], axis=1)

    def finish(i, outs):
        r = slice(i * ATT_T, (i + 1) * ATT_T)
        for hh in range(2):
            o = outs[hh][:, :d] / outs[hh][:, d:]
            o_ref[r, hh * d:(hh + 1) * d] = _rms(o, g_ref[0, :, hh * d:(hh + 1) * d]).astype(BF16)

    _softmax_attention(
        qcat_at, lambda hh, klen: kcat_ref[hh, :klen, :], lambda hh, klen: vaug_ref[hh, :klen, :],
        chunk_mask, finish, s_ref, p_ref)


def _mla_attention(q_mla, kv_mla, krope, out_gain, layer, w_cast):
    n_pair = MLA_H // 2
    assert n_pair == ATT_PAIRS
    pair_w = 2 * MLA_NOPE_D
    w_in_spec, w_out_spec, w_out_shape = _cast_along(w_cast, layer)
    return pl.pallas_call(
        _mla_body,
        grid=(N_BATCH, n_pair),
        in_specs=[
            pl.BlockSpec((SEQ_LEN, pair_w), lambda b, p: (b, p)),
            pl.BlockSpec((SEQ_LEN, LANES), lambda b, p: (b, MLA_H * MLA_NOPE_D // LANES + p)),
            pl.BlockSpec((SEQ_LEN, pair_w), lambda b, p: (b, p)),
            pl.BlockSpec((SEQ_LEN, pair_w), lambda b, p: (b, n_pair + p)),
            pl.BlockSpec((SEQ_LEN, LANES), lambda b, p: (b, 0)),
            pl.BlockSpec((1, 1, pair_w), lambda b, p: (layer, 0, p)),
            w_in_spec,
        ],
        out_specs=[pl.BlockSpec((SEQ_LEN, pair_w), lambda b, p: (b, p)), w_out_spec],
        out_shape=[jax.ShapeDtypeStruct((N_TOK, MLA_H * MLA_V_D), BF16), w_out_shape],
        scratch_shapes=[
            pltpu.VMEM((2, SEQ_LEN, 2 * LANES), BF16),
            pltpu.VMEM((2, SEQ_LEN, 2 * LANES), BF16),
        ] + _SCORE_SCRATCH,
        compiler_params=_params(("arbitrary", "arbitrary")),
        name="mla_attention",
    )(q_mla, q_mla, kv_mla, kv_mla, krope, out_gain, w_cast)


def _fox_body(q_ref, k_ref, v_ref, qa_ref, ka_ref, g_ref, w_ref, o_ref, wb_ref, kcat_ref, vaug_ref,
              s_ref, p_ref):
    wb_ref[...] = w_ref[0].astype(BF16)
    kcat_ref[:, :LANES] = k_ref[...]
    kcat_ref[:, LANES:] = ka_ref[...]
    vaug_ref[:, :LANES] = v_ref[...]
    vaug_ref[:, LANES:] = jnp.ones((SEQ_LEN, LANES), BF16)

    def causal(rows, cols):
        return cols <= rows

    def qcat_at(i, hh):
        r = slice(i * ATT_T, (i + 1) * ATT_T)
        q = q_ref[r, :]
        qa = qa_ref[r, :]
        zero = jnp.zeros_like(q)
        return jnp.concatenate([
            jnp.where(_lane_band(q.shape, hh * FOX_D, FOX_D), q, zero),
            jnp.where(_lane_band(qa.shape, hh * SUBLANES, SUBLANES), qa, zero)], axis=1)

    def finish(i, outs):
        heads = [outs[hh][:, :LANES] / outs[hh][:, LANES:] for hh in range(2)]
        o = jnp.where(_lane_band(heads[0].shape, 0, FOX_D), heads[0], heads[1])
        o_ref[i * ATT_T:(i + 1) * ATT_T, :] = _pair_norm(o, g_ref[0], FOX_D).astype(BF16)

    _softmax_attention(
        qcat_at, lambda hh, klen: kcat_ref[:klen, :], lambda hh, klen: vaug_ref[:klen, :],
        causal, finish, s_ref, p_ref)


def _fox_attention(p_att, qaug, kaug, out_gain, layer, w_cast):
    n_pair = FOX_H // 2
    assert n_pair == ATT_PAIRS
    grp = HEAD_GRP // LANES
    gain_blk = (MLA_H * MLA_V_D + SB_H * SB_D) // LANES
    w_in_spec, w_out_spec, w_out_shape = _cast_along(w_cast, layer)
    return pl.pallas_call(
        _fox_body,
        grid=(N_BATCH, n_pair),
        in_specs=[
            pl.BlockSpec((SEQ_LEN, LANES), lambda b, p: (b, 3 * grp + p)),
            pl.BlockSpec((SEQ_LEN, LANES), lambda b, p: (b, 4 * grp + p)),
            pl.BlockSpec((SEQ_LEN, LANES), lambda b, p: (b, 5 * grp + p)),
            pl.BlockSpec((SEQ_LEN, LANES), lambda b, p: (b, p)),
            pl.BlockSpec((SEQ_LEN, LANES), lambda b, p: (b, p)),
            pl.BlockSpec((1, 1, LANES), lambda b, p: (layer, 0, gain_blk + p)),
            w_in_spec,
        ],
        out_specs=[pl.BlockSpec((SEQ_LEN, LANES), lambda b, p: (b, p)), w_out_spec],
        out_shape=[jax.ShapeDtypeStruct((N_TOK, FOX_H * FOX_D), BF16), w_out_shape],
        scratch_shapes=[
            pltpu.VMEM((SEQ_LEN, 2 * LANES), BF16),
            pltpu.VMEM((SEQ_LEN, 2 * LANES), BF16),
        ] + _SCORE_SCRATCH,
        compiler_params=_params(("arbitrary", "arbitrary")),
        name="fox_attention",
    )(p_att, p_att, p_att, qaug, kaug, out_gain, w_cast)


def _sb_body(q_ref, k_ref, v_ref, tri_ref, g_ref, w_ref, o_ref, wb_ref, u_ref, p_ref):
    wb_ref[...] = w_ref[0].astype(BF16)

    def score_tiles(i):
        q = q_ref[i * ATT_T:(i + 1) * ATT_T, :]
        zero = jnp.zeros_like(q)
        q_heads = [jnp.where(_lane_band(q.shape, hh * SB_D, SB_D), q, zero) for hh in range(2)]

        def tile(hh, kb):
            u_ref[i % 2, hh, kb] = _dot_nt(q_heads[hh], k_ref[kb * ATT_T:(kb + 1) * ATT_T, :])

        return [functools.partial(tile, hh, kb) for kb in range(i + 1) for hh in range(2)]

    def weights(i, pending):
        nkb = i + 1
        right = [None, None]
        for kb in reversed(range(nkb)):
            on_diagonal = kb == nkb - 1
            for hh in range(2):
                if pending:
                    pending.pop()()
                u = u_ref[i % 2, hh, kb]
                drop = jnp.maximum(u, 0.0) + jnp.log2(1.0 + jnp.exp2(-jnp.abs(u)))
                if on_diagonal:
                    rows, cols = _rel_positions()
                    strict = cols < rows
                    drop = jnp.where(strict, drop, 0.0)
                log_w = (u - drop) + _dot(drop.astype(BF16), tri_ref[...])
                if right[hh] is not None:
                    log_w = log_w - right[hh]
                w = jnp.exp2(log_w)
                if on_diagonal:
                    w = jnp.where(strict, w, 0.0)
                p_ref[i % 2, hh, kb] = w.astype(BF16)
                total = jnp.sum(drop, axis=-1, keepdims=True)
                right[hh] = total if right[hh] is None else right[hh] + total

    for tile in score_tiles(QB_ORDER[0]):
        tile()
    for n, i in enumerate(QB_ORDER):
        nkb = i + 1
        pending = score_tiles(QB_ORDER[n + 1]) if n + 1 < N_QB else []
        weights(i, pending)
        while pending:
            pending.pop()()
        heads = [_weights_times_v(p_ref.at[i % 2], hh, nkb, v_ref[:nkb * ATT_T, :]) for hh in range(2)]
        o = jnp.where(_lane_band(heads[0].shape, 0, SB_D), heads[0], heads[1])
        o_ref[i * ATT_T:(i + 1) * ATT_T, :] = _pair_norm(o, g_ref[0], SB_D).astype(BF16)


def _sb_attention(p_att, out_gain, layer, w_cast):
    n_pair = SB_H // 2
    assert n_pair == ATT_PAIRS
    grp = HEAD_GRP // LANES
    gain_blk = MLA_H * MLA_V_D // LANES
    w_in_spec, w_out_spec, w_out_shape = _cast_along(w_cast, layer)
    tri = jnp.asarray(-np.tril(np.ones((ATT_T, ATT_T), np.float32), -1), BF16)
    return pl.pallas_call(
        _sb_body,
        grid=(N_BATCH, n_pair),
        in_specs=[
            pl.BlockSpec((SEQ_LEN, LANES), lambda b, p: (b, p)),
            pl.BlockSpec((SEQ_LEN, LANES), lambda b, p: (b, grp + p)),
            pl.BlockSpec((SEQ_LEN, LANES), lambda b, p: (b, 2 * grp + p)),
            _resident((ATT_T, ATT_T)),
            pl.BlockSpec((1, 1, LANES), lambda b, p: (layer, 0, gain_blk + p)),
            w_in_spec,
        ],
        out_specs=[pl.BlockSpec((SEQ_LEN, LANES), lambda b, p: (b, p)), w_out_spec],
        out_shape=[jax.ShapeDtypeStruct((N_TOK, SB_H * SB_D), BF16), w_out_shape],
        scratch_shapes=_SCORE_SCRATCH,
        compiler_params=_params(("arbitrary", "arbitrary")),
        name="sb_attention",
    )(p_att, p_att, p_att, tri, out_gain, w_cast)


def _outproj_body(x_ref, mod_ref, om_ref, os_ref, of_ref, w_ref, gn_ref, o_ref, h_ref):
    n_mla = MLA_H * MLA_V_D
    n_sb = SB_H * SB_D
    mod = mod_ref[0, 0]
    gain = gn_ref[0] * (1.0 + mod[4:5])
    half = x_ref.shape[0] // 2
    for part in range(2):
        rows = slice(part * half, (part + 1) * half)
        y = (_dot(om_ref[rows, :], w_ref[:n_mla, :]) + _dot(os_ref[rows, :], w_ref[n_mla:n_mla + n_sb, :])
             + _dot(of_ref[rows, :], w_ref[n_mla + n_sb:, :]))
        y = x_ref[rows, :] + mod[2:3] * y
        o_ref[rows, :] = y
        h_ref[rows, :] = (_rms(y, gain) + mod[3:4]).astype(BF16)


def _out_project(x2, mod4, layer, o_mla, o_sb, o_fx, w_out, gn_ffn):
    tm = PROJ_TM
    tiles_per_seq = SEQ_LEN // tm
    row = lambda i: (i, 0)
    return pl.pallas_call(
        _outproj_body,
        grid=(N_TOK // tm,),
        in_specs=[
            pl.BlockSpec((tm, D_MODEL), row),
            pl.BlockSpec((1, 1, 6, D_MODEL), lambda i: (layer, i // tiles_per_seq, 0, 0)),
            pl.BlockSpec((tm, MLA_H * MLA_V_D), row),
            pl.BlockSpec((tm, SB_H * SB_D), row),
            pl.BlockSpec((tm, FOX_H * FOX_D), row),
            _resident((D_MODEL, D_MODEL)),
            _layer_resident((1, D_MODEL), layer),
        ],
        out_specs=[pl.BlockSpec((tm, D_MODEL), row), pl.BlockSpec((tm, D_MODEL), row)],
        out_shape=[jax.ShapeDtypeStruct((N_TOK, D_MODEL), F32), jax.ShapeDtypeStruct((N_TOK, D_MODEL), BF16)],
        compiler_params=_params(("arbitrary",)),
        name="out_project",
    )(x2, mod4, o_mla, o_sb, o_fx, w_out, gn_ffn)


def _ffn_body(x_ref, h_ref, mod_ref, w1_ref, w2_ref, gf_ref, o_ref, *, final):
    j = pl.program_id(1)
    mod = mod_ref[0, 0]

    @pl.when(j == 0)
    def _():
        o_ref[...] = x_ref[...]

    a = jnp.maximum(_dot(h_ref[...], w1_ref[...]), 0.0)
    o_ref[...] += mod[5:6] * _dot((a * a).astype(BF16), w2_ref[...])

    if final:
        @pl.when(j == pl.num_programs(1) - 1)
        def _():
            inv = _inv_rms(o_ref)
            o_ref[...] = o_ref[...] * inv * gf_ref[...]


def _ffn(x2, h2, mod4, layer, w1, w2, g_final, final):
    tm, tf = FFN_TM, FFN_TF
    tiles_per_seq = SEQ_LEN // tm
    return pl.pallas_call(
        functools.partial(_ffn_body, final=final),
        grid=(N_TOK // tm, FF_DIM // tf),
        in_specs=[
            pl.BlockSpec((tm, D_MODEL), lambda i, j: (i, 0)),
            pl.BlockSpec((tm, D_MODEL), lambda i, j: (i, 0)),
            pl.BlockSpec((1, 1, 6, D_MODEL), lambda i, j: (layer, i // tiles_per_seq, 0, 0)),
            pl.BlockSpec((D_MODEL, tf), lambda i, j: (0, j)),
            pl.BlockSpec((tf, D_MODEL), lambda i, j: (j, 0)),
            pl.BlockSpec((1, D_MODEL), lambda i, j: (0, 0)),
        ],
        out_specs=pl.BlockSpec((tm, D_MODEL), lambda i, j: (i, 0)),
        out_shape=jax.ShapeDtypeStruct((N_TOK, D_MODEL), F32),
        compiler_params=_params(("arbitrary", "arbitrary")),
        name="ffn",
    )(x2, h2, mod4, w1, w2, g_final)


def _swap_halves(w):
    half = w.shape[-1] // 2
    return jnp.concatenate([w[..., half:], w[..., :half]], axis=-1)


def _pack_w_in(w):
    wt = jnp.swapaxes(w, 1, 2).astype(BF16)
    kr = wt[:, R_KR:R_ATT]
    kr_sw = jnp.concatenate([kr[:, MLA_ROPE_D // 2:], kr[:, :MLA_ROPE_D // 2]], axis=1)
    wt_kr = jnp.concatenate([kr, kr, kr_sw, kr_sw], axis=1)
    wt_f = jnp.pad(wt[:, R_F:], ((0, 0), (0, LANES - FOX_H), (0, 0)))
    return wt, wt_kr, wt_f


def _pack_w_uq(w):
    w4 = w.astype(BF16).reshape(N_LAYERS, MLA_QL, MLA_H, MLA_NOPE_D + MLA_ROPE_D)
    flat = lambda a: a.reshape(N_LAYERS, MLA_QL, -1)
    rope = w4[..., MLA_NOPE_D:]
    return jnp.concatenate([flat(w4[..., :MLA_NOPE_D]), flat(rope), flat(_swap_halves(rope))], axis=-1)


def _pack_w_ukv(w):
    w4 = w.astype(BF16).reshape(N_LAYERS, MLA_KVL, MLA_H, MLA_NOPE_D + MLA_V_D)
    flat = lambda a: a.reshape(N_LAYERS, MLA_KVL, -1)
    return jnp.concatenate([flat(w4[..., :MLA_NOPE_D]), flat(w4[..., MLA_NOPE_D:])], axis=-1)


def _rope_tables():
    pos = jnp.arange(SEQ_LEN, dtype=F32)
    inv = ROPE_THETA ** (-jnp.arange(0, MLA_ROPE_D, 2, dtype=F32) / MLA_ROPE_D)
    ang = pos[:, None] * inv[None, :]
    cos, sin = jnp.cos(ang), jnp.sin(ang)
    cos_t = jnp.tile(jnp.concatenate([cos, cos], axis=1), (1, MLA_H))
    sin_t = jnp.tile(jnp.concatenate([-sin, sin], axis=1), (1, MLA_H))
    return cos_t, sin_t


def kernel(x, c, w_ada, b_ada, norm_mix, w_in, q_norm, w_uq, kv_norm, w_ukv, b_forget, out_norm,
           w_out, norm_ffn, w_ff1, w_ff2, final_norm):
    assert x.shape == (N_BATCH, SEQ_LEN, D_MODEL) and c.shape == (N_BATCH, D_MODEL)
    cos_t, sin_t = _rope_tables()
    c_pad = jnp.pad(c, ((0, SUBLANES - N_BATCH), (0, 0)))
    mod = _ada_mod(c_pad, w_ada, b_ada)
    mod4 = mod[:, :N_BATCH].reshape(N_LAYERS, N_BATCH, 6, D_MODEL)

    rows = lambda v: v.reshape(N_LAYERS, 1, -1)
    assert w_in.shape == (N_LAYERS, D_MODEL, IN_W)
    wt_in, wt_kr, wt_f = _pack_w_in(w_in)
    wuq_p, wukv_p = _pack_w_uq(w_uq), _pack_w_ukv(w_ukv)
    bf_pad = jnp.pad(b_forget, ((0, 0), (0, LANES - FOX_H))).reshape(N_LAYERS, 1, LANES)
    out_gain = rows(out_norm)

    x2 = x.reshape(N_TOK, D_MODEL)
    for l in range(N_LAYERS):
        q_mla, kv_mla, krope, p_att, logf = _project(
            x2, mod4, l, rows(norm_mix), wt_in, wt_kr, wt_f, rows(q_norm), wuq_p, rows(kv_norm), wukv_p, bf_pad,
            cos_t, sin_t)
        qaug, kaug = _fgate(logf)
        o_mla, wff1_b = _mla_attention(q_mla, kv_mla, krope, out_gain, l, w_ff1)
        o_sb, wout_b = _sb_attention(p_att, out_gain, l, w_out)
        o_fx, wff2_b = _fox_attention(p_att, qaug, kaug, out_gain, l, w_ff2)
        x2, h_ffn = _out_project(x2, mod4, l, o_mla, o_sb, o_fx, wout_b, rows(norm_ffn))
        x2 = _ffn(x2, h_ffn, mod4, l, wff1_b, wff2_b, final_norm.reshape(1, -1),
                  final=(l == N_LAYERS - 1))
    return x2.reshape(N_BATCH, SEQ_LEN, D_MODEL)
```

```python
import functools

import numpy as np
import jax
import jax.numpy as jnp
from jax import lax
from jax.experimental import pallas as pl
from jax.experimental.pallas import tpu as pltpu

F32 = jnp.float32
BF16 = jnp.bfloat16

D_MODEL = 2048
N_BATCH = 4
SEQ_LEN = 2048
N_LAYERS = 2
CHUNK_LEN = 64
NORM_EPS = 1e-6
LOG2_E = 1.4426950408889634
MLA_H = 8
MLA_QL = 512
MLA_KVL = 256
MLA_NOPE_D = 128
MLA_ROPE_D = 64
MLA_V_D = 128
ROPE_THETA = 10000.0
SB_H = 8
SB_D = 64
FOX_H = 8
FOX_D = 64
FF_DIM = 4 * D_MODEL
N_TOK = N_BATCH * SEQ_LEN

LANES = 128
SUBLANES = 8
VMEM_LIMIT = 56 * 1024 * 1024

R_CQ = 0
R_CKV = R_CQ + MLA_QL
R_KR = R_CKV + MLA_KVL
R_ATT = R_KR + MLA_ROPE_D
ATT_W = 3 * SB_H * SB_D + 3 * FOX_H * FOX_D
R_F = R_ATT + ATT_W
IN_W = R_F + FOX_H
HEAD_GRP = SB_H * SB_D

PROJ_TM = 512
FFN_TM = 1024
FFN_TF = 512
ADA_TN = 1024
ATT_T = 256
N_QB = SEQ_LEN // ATT_T
QB_ORDER = tuple(reversed(range(N_QB)))
CUM_T = 256


def _dot(a, b):
    return jnp.dot(a, b, preferred_element_type=F32)


def _dot_nt(a, b):
    return lax.dot_general(a, b, (((1,), (1,)), ((), ())), preferred_element_type=F32)


def _split3(x):
    hi = x.astype(BF16)
    r1 = x - hi.astype(F32)
    mid = r1.astype(BF16)
    lo = (r1 - mid.astype(F32)).astype(BF16)
    return hi, mid, lo


def _softplus(z):
    return jnp.maximum(z, 0.0) + jnp.log(1.0 + jnp.exp(-jnp.abs(z)))


def _rms(x, gain):
    return x * lax.rsqrt(jnp.mean(x * x, axis=-1, keepdims=True) + NORM_EPS) * gain


def _inv_rms(x_ref):
    x = x_ref[...]
    return lax.rsqrt(jnp.mean(x * x, axis=-1, keepdims=True) + NORM_EPS)


def _modulated_norm(x_ref, norm_gain, mod_scale, mod_shift, h_ref):
    inv = _inv_rms(x_ref)
    gain = norm_gain * (1.0 + mod_scale)
    h_ref[...] = (x_ref[...] * inv * gain + mod_shift).astype(BF16)


def _resident(shape):
    nd = len(shape)
    return pl.BlockSpec(shape, lambda *_: (0,) * nd, pipeline_mode=pl.Buffered(1))


def _layer_resident(shape, layer):
    nd = len(shape)
    return pl.BlockSpec((1,) + shape, lambda *_: (layer,) + (0,) * nd, pipeline_mode=pl.Buffered(1))


def _params(sem):
    return pltpu.CompilerParams(dimension_semantics=sem, vmem_limit_bytes=VMEM_LIMIT)


def _ada_columns(c_ref, w_ref, b_ref, o_ref):
    c = c_ref[...]
    c_act = (c * jax.nn.sigmoid(c)).astype(BF16)
    o_ref[...] = _dot(c_act, w_ref[0].astype(BF16)) + b_ref[0]


def _ada_mod(c_pad, w_ada, b_ada3, layer):
    n_out = 6 * D_MODEL
    return pl.pallas_call(
        _ada_columns,
        grid=(n_out // ADA_TN,),
        in_specs=[
            pl.BlockSpec((SUBLANES, D_MODEL), lambda j: (0, 0)),
            pl.BlockSpec((1, D_MODEL, ADA_TN), lambda j: (layer, 0, j)),
            pl.BlockSpec((1, 1, ADA_TN), lambda j: (layer, 0, j)),
        ],
        out_specs=pl.BlockSpec((SUBLANES, ADA_TN), lambda j: (0, j)),
        out_shape=jax.ShapeDtypeStruct((SUBLANES, n_out), F32),
        compiler_params=_params(("arbitrary",)),
        name="ada_mod",
    )(c_pad, w_ada, b_ada3)


def _ada_along(layer):
    n_out = 6 * D_MODEL
    slab = n_out // (N_BATCH * ATT_PAIRS)
    assert slab * N_BATCH * ATT_PAIRS == n_out and slab % LANES == 0
    step = lambda b, p: b * ATT_PAIRS + p
    in_specs = [
        pl.BlockSpec((SUBLANES, D_MODEL), lambda b, p: (0, 0)),
        pl.BlockSpec((1, D_MODEL, slab), lambda b, p: (layer, 0, step(b, p))),
        pl.BlockSpec((1, 1, slab), lambda b, p: (layer, 0, step(b, p))),
    ]
    out_spec = pl.BlockSpec((SUBLANES, slab), lambda b, p: (0, step(b, p)))
    return in_specs, out_spec, jax.ShapeDtypeStruct((SUBLANES, n_out), F32)


def _proj_body(x_ref, mod_ref, gn_ref, wt_ref, wkr_ref, wf_ref, qn_ref, wuq_ref, kvn_ref, wukv_ref, bf_ref,
               cos_ref, sin_ref, q_out, kv_out, kr_out, att_out, lf_out, h_ref):
    mod = mod_ref[0]
    _modulated_norm(x_ref, gn_ref[0], mod[1:2], mod[0:1], h_ref)

    def proj(r0, width):
        return _dot_nt(h_ref[...], wt_ref[0, r0:r0 + width, :])

    mla_scale = (MLA_NOPE_D + MLA_ROPE_D) ** -0.5 * LOG2_E
    n_nope = MLA_H * MLA_NOPE_D
    n_rope = MLA_H * MLA_ROPE_D

    cq = _rms(proj(R_CQ, MLA_QL), qn_ref[0]).astype(BF16)
    q = _dot(cq, wuq_ref[0])
    q_out[:, :n_nope] = (q[:, :n_nope] * mla_scale).astype(BF16)
    q_rot = q[:, n_nope:n_nope + n_rope] * cos_ref[...] + q[:, n_nope + n_rope:] * sin_ref[...]
    q_out[:, n_nope:] = (q_rot * mla_scale).astype(BF16)

    ckv = _rms(proj(R_CKV, MLA_KVL), kvn_ref[0]).astype(BF16)
    kv_out[...] = _dot(ckv, wukv_ref[0]).astype(BF16)

    kr = _dot_nt(h_ref[...], wkr_ref[0])
    kr_out[...] = (kr[:, :LANES] * cos_ref[:, :LANES] + kr[:, LANES:] * sin_ref[:, :LANES]).astype(BF16)

    lf_out[...] = -_softplus(-(_dot_nt(h_ref[...], wf_ref[0]) + bf_ref[0]))

    sb_scale = SB_D ** -0.5 * LOG2_E
    fox_scale = FOX_D ** -0.5 * LOG2_E
    scales = (sb_scale, 1.0, 1.0, fox_scale, 1.0, 1.0)
    for g, sc in enumerate(scales):
        blk = proj(R_ATT + g * HEAD_GRP, HEAD_GRP)
        if sc != 1.0:
            blk = blk * sc
        att_out[:, g * HEAD_GRP:(g + 1) * HEAD_GRP] = blk.astype(BF16)


def _project(x2, mod_l, layer, gn, wt_in, wt_kr, wt_f, qn, wuq, kvn, wukv, bfp, cos_t, sin_t):
    tm = PROJ_TM
    tiles_per_seq = SEQ_LEN // tm
    q_w = MLA_H * (MLA_NOPE_D + MLA_ROPE_D)
    kv_w = MLA_H * (MLA_NOPE_D + MLA_V_D)
    rope_w = MLA_H * MLA_ROPE_D
    row = lambda i: (i, 0)
    return pl.pallas_call(
        _proj_body,
        grid=(N_TOK // tm,),
        in_specs=[
            pl.BlockSpec((tm, D_MODEL), row),
            pl.BlockSpec((1, 6, D_MODEL), lambda i: (i // tiles_per_seq, 0, 0)),
            _layer_resident((1, D_MODEL), layer),
            _layer_resident((IN_W, D_MODEL), layer),
            _layer_resident((2 * LANES, D_MODEL), layer),
            _layer_resident((LANES, D_MODEL), layer),
            _layer_resident((1, MLA_QL), layer),
            _layer_resident((MLA_QL, q_w + rope_w), layer),
            _layer_resident((1, MLA_KVL), layer),
            _layer_resident((MLA_KVL, kv_w), layer),
            _layer_resident((1, LANES), layer),
            pl.BlockSpec((tm, rope_w), lambda i: (i % tiles_per_seq, 0)),
            pl.BlockSpec((tm, rope_w), lambda i: (i % tiles_per_seq, 0)),
        ],
        out_specs=[
            pl.BlockSpec((tm, q_w), row),
            pl.BlockSpec((tm, kv_w), row),
            pl.BlockSpec((tm, LANES), row),
            pl.BlockSpec((tm, ATT_W), row),
            pl.BlockSpec((tm, LANES), row),
        ],
        out_shape=[
            jax.ShapeDtypeStruct((N_TOK, q_w), BF16),
            jax.ShapeDtypeStruct((N_TOK, kv_w), BF16),
            jax.ShapeDtypeStruct((N_TOK, LANES), BF16),
            jax.ShapeDtypeStruct((N_TOK, ATT_W), BF16),
            jax.ShapeDtypeStruct((N_TOK, LANES), F32),
        ],
        scratch_shapes=[pltpu.VMEM((tm, D_MODEL), BF16)],
        compiler_params=_params(("arbitrary",)),
        name="project",
    )(x2, mod_l, gn, wt_in, wt_kr, wt_f, qn, wuq, kvn, wukv, bfp, cos_t, sin_t)


def _fgate_body(lf_ref, tri_ref, pq_ref, pk_ref, cq_ref, ck_ref, qa_out, ka_out):
    n_blk = SEQ_LEN // CUM_T
    local = []
    for blk in range(n_blk):
        hi, mid, lo = _split3(lf_ref[blk * CUM_T:(blk + 1) * CUM_T, :] * LOG2_E)
        tri = tri_ref[...]
        local.append(_dot(tri, hi) + _dot(tri, mid) + _dot(tri, lo))
    carry = jnp.zeros((1, LANES), F32)
    for blk in range(n_blk):
        rows = slice(blk * CUM_T, (blk + 1) * CUM_T)
        f_blk = local[blk] + carry
        carry = f_blk[CUM_T - 1:CUM_T, :]
        f_hi, f_mid, f_lo = _split3(f_blk)
        qa = _dot(f_hi, pq_ref[0]) + _dot(f_mid, pq_ref[1]) + _dot(f_lo, pq_ref[2]) + cq_ref[...]
        ka = ck_ref[...] - (_dot(f_hi, pk_ref[0]) + _dot(f_mid, pk_ref[1]) + _dot(f_lo, pk_ref[2]))
        qa_out[rows, :] = qa.astype(BF16)
        ka_out[rows, :] = ka.astype(BF16)


def _fgate_constants():
    tri = np.tril(np.ones((CUM_T, CUM_T), np.float32))
    n_pair = FOX_H // 2
    pq = np.zeros((3, LANES, n_pair * LANES), np.float32)
    pk = np.zeros((3, LANES, n_pair * LANES), np.float32)
    cq = np.zeros((1, n_pair * LANES), np.float32)
    ck = np.zeros((1, n_pair * LANES), np.float32)
    for h in range(FOX_H):
        base = (h // 2) * LANES + (h % 2) * SUBLANES
        for t in range(3):
            pq[t, h, base + t] = 1.0
            pk[t, h, base + 3 + t] = 1.0
            cq[0, base + 3 + t] = 1.0
            ck[0, base + t] = 1.0
    as_bf16 = lambda a: jnp.asarray(a, BF16)
    return as_bf16(tri), as_bf16(pq), as_bf16(pk), jnp.asarray(cq), jnp.asarray(ck)


def _fgate(logf):
    tri, pq, pk, cq, ck = _fgate_constants()
    aug_w = (FOX_H // 2) * LANES
    seq_blk = lambda b: (b, 0)
    return pl.pallas_call(
        _fgate_body,
        grid=(N_BATCH,),
        in_specs=[
            pl.BlockSpec((SEQ_LEN, LANES), seq_blk),
            _resident((CUM_T, CUM_T)),
            _resident((3, LANES, aug_w)),
            _resident((3, LANES, aug_w)),
            _resident((1, aug_w)),
            _resident((1, aug_w)),
        ],
        out_specs=[pl.BlockSpec((SEQ_LEN, aug_w), seq_blk), pl.BlockSpec((SEQ_LEN, aug_w), seq_blk)],
        out_shape=[jax.ShapeDtypeStruct((N_TOK, aug_w), BF16), jax.ShapeDtypeStruct((N_TOK, aug_w), BF16)],
        compiler_params=_params(("arbitrary",)),
        name="fgate",
    )(logf, tri, pq, pk, cq, ck)


def _lane_band(shape, lo, width):
    lane = lax.broadcasted_iota(jnp.int32, shape, len(shape) - 1)
    return (lane >= lo) & (lane < lo + width)


def _rel_positions():
    rows = lax.broadcasted_iota(jnp.int32, (ATT_T, ATT_T), 0)
    cols = lax.broadcasted_iota(jnp.int32, (ATT_T, ATT_T), 1)
    return rows, cols


def _col_block(a, kb):
    return a[:, kb * ATT_T:(kb + 1) * ATT_T]


def _weights_times_v(p_ref, hh, nkb, v):
    p = jnp.concatenate([p_ref[hh, kb] for kb in range(nkb)], axis=1) if nkb > 1 else p_ref[hh, 0]
    return _dot(p, v)


def _softmax_attention(qcat_at, kcat_at, vaug_at, mask_fn, finish, s_ref, p_ref):
    def scores(i):
        nkb = i + 1
        row_max = []
        for hh in range(2):
            s = _dot_nt(qcat_at(i, hh), kcat_at(hh, nkb * ATT_T))
            blocks = [_col_block(s, kb) for kb in range(nkb)]
            blocks[-1] = jnp.where(mask_fn(*_rel_positions()), blocks[-1], -jnp.inf)
            m = None
            for kb in range(nkb):
                s_ref[i % 2, hh, kb] = blocks[kb]
                bm = jnp.maximum(blocks[kb][:, :LANES], blocks[kb][:, LANES:])
                m = bm if m is None else jnp.maximum(m, bm)
            row_max.append(jnp.max(m, axis=-1, keepdims=True))
        return row_max

    row_max = scores(QB_ORDER[0])
    for n, i in enumerate(QB_ORDER):
        nkb = i + 1
        next_max = scores(QB_ORDER[n + 1]) if n + 1 < N_QB else None
        for kb in range(nkb):
            for hh in range(2):
                p_ref[i % 2, hh, kb] = jnp.exp2(s_ref[i % 2, hh, kb] - row_max[hh]).astype(BF16)
        finish(i, [_weights_times_v(p_ref.at[i % 2], hh, nkb, vaug_at(hh, nkb * ATT_T)) for hh in range(2)])
        row_max = next_max


def _pair_norm(o, gain, head_d):
    first = _lane_band(o.shape, 0, head_d)
    sq = o * o
    ss_a = jnp.sum(jnp.where(first, sq, 0.0), axis=-1, keepdims=True)
    ss_b = jnp.sum(sq, axis=-1, keepdims=True) - ss_a
    inv = jnp.where(first, lax.rsqrt(ss_a / head_d + NORM_EPS), lax.rsqrt(ss_b / head_d + NORM_EPS))
    return o * inv * gain


_SCORE_SCRATCH = [
    pltpu.VMEM((2, 2, N_QB, ATT_T, ATT_T), F32),
    pltpu.VMEM((2, 2, N_QB, ATT_T, ATT_T), BF16),
]
ATT_PAIRS = 4


def _cast_along(w, layer):
    _, n_rows, n_cols = w.shape
    slab = n_rows // (N_BATCH * ATT_PAIRS)
    assert slab * N_BATCH * ATT_PAIRS == n_rows and slab % (2 * SUBLANES) == 0
    in_spec = pl.BlockSpec((1, slab, n_cols), lambda b, p: (layer, b * ATT_PAIRS + p, 0))
    out_spec = pl.BlockSpec((slab, n_cols), lambda b, p: (b * ATT_PAIRS + p, 0))
    return in_spec, out_spec, jax.ShapeDtypeStruct((n_rows, n_cols), BF16)


def _mla_body(qn_ref, qr_ref, kn_ref, v_ref, kr_ref, g_ref, w_ref, o_ref, wb_ref, kcat_ref, vaug_ref,
              s_ref, p_ref):
    wb_ref[...] = w_ref[0].astype(BF16)
    d = MLA_NOPE_D
    for hh in range(2):
        kcat_ref[hh, :, :d] = kn_ref[:, hh * d:(hh + 1) * d]
        kcat_ref[hh, :, d:] = kr_ref[...]
        vaug_ref[hh, :, :d] = v_ref[:, hh * d:(hh + 1) * d]
        vaug_ref[hh, :, d:] = jnp.ones((SEQ_LEN, LANES), BF16)

    def chunk_mask(rows, cols):
        return cols < ((rows >> 6) + 1) * CHUNK_LEN

    def qcat_at(i, hh):
        r = slice(i * ATT_T, (i + 1) * ATT_T)
        qr = qr_ref[r, :]
        qr = jnp.where(_lane_band(qr.shape, hh * MLA_ROPE_D, MLA_ROPE_D), qr, jnp.zeros_like(qr))
        return jnp.concatenate([qn_ref[r, hh * d:(hh + 1) * d], qr], axis=1)

    def finish(i, outs):
        r = slice(i * ATT_T, (i + 1) * ATT_T)
        for hh in range(2):
            o = outs[hh][:, :d] / outs[hh][:, d:]
            o_ref[r, hh * d:(hh + 1) * d] = _rms(o, g_ref[0, :, hh * d:(hh + 1) * d]).astype(BF16)

    _softmax_attention(
        qcat_at, lambda hh, klen: kcat_ref[hh, :klen, :], lambda hh, klen: vaug_ref[hh, :klen, :],
        chunk_mask, finish, s_ref, p_ref)


def _mla_attention(q_mla, kv_mla, krope, out_gain, layer, w_cast):
    n_pair = MLA_H // 2
    assert n_pair == ATT_PAIRS
    pair_w = 2 * MLA_NOPE_D
    w_in_spec, w_out_spec, w_out_shape = _cast_along(w_cast, layer)
    return pl.pallas_call(
        _mla_body,
        grid=(N_BATCH, n_pair),
        in_specs=[
            pl.BlockSpec((SEQ_LEN, pair_w), lambda b, p: (b, p)),
            pl.BlockSpec((SEQ_LEN, LANES), lambda b, p: (b, MLA_H * MLA_NOPE_D // LANES + p)),
            pl.BlockSpec((SEQ_LEN, pair_w), lambda b, p: (b, p)),
            pl.BlockSpec((SEQ_LEN, pair_w), lambda b, p: (b, n_pair + p)),
            pl.BlockSpec((SEQ_LEN, LANES), lambda b, p: (b, 0)),
            pl.BlockSpec((1, 1, pair_w), lambda b, p: (layer, 0, p)),
            w_in_spec,
        ],
        out_specs=[pl.BlockSpec((SEQ_LEN, pair_w), lambda b, p: (b, p)), w_out_spec],
        out_shape=[jax.ShapeDtypeStruct((N_TOK, MLA_H * MLA_V_D), BF16), w_out_shape],
        scratch_shapes=[
            pltpu.VMEM((2, SEQ_LEN, 2 * LANES), BF16),
            pltpu.VMEM((2, SEQ_LEN, 2 * LANES), BF16),
        ] + _SCORE_SCRATCH,
        compiler_params=_params(("arbitrary", "arbitrary")),
        name="mla_attention",
    )(q_mla, q_mla, kv_mla, kv_mla, krope, out_gain, w_cast)


def _fox_body(q_ref, k_ref, v_ref, qa_ref, ka_ref, g_ref, w_ref, o_ref, wb_ref, kcat_ref, vaug_ref,
              s_ref, p_ref):
    wb_ref[...] = w_ref[0].astype(BF16)
    kcat_ref[:, :LANES] = k_ref[...]
    kcat_ref[:, LANES:] = ka_ref[...]
    vaug_ref[:, :LANES] = v_ref[...]
    vaug_ref[:, LANES:] = jnp.ones((SEQ_LEN, LANES), BF16)

    def causal(rows, cols):
        return cols <= rows

    def qcat_at(i, hh):
        r = slice(i * ATT_T, (i + 1) * ATT_T)
        q = q_ref[r, :]
        qa = qa_ref[r, :]
        zero = jnp.zeros_like(q)
        return jnp.concatenate([
            jnp.where(_lane_band(q.shape, hh * FOX_D, FOX_D), q, zero),
            jnp.where(_lane_band(qa.shape, hh * SUBLANES, SUBLANES), qa, zero)], axis=1)

    def finish(i, outs):
        heads = [outs[hh][:, :LANES] / outs[hh][:, LANES:] for hh in range(2)]
        o = jnp.where(_lane_band(heads[0].shape, 0, FOX_D), heads[0], heads[1])
        o_ref[i * ATT_T:(i + 1) * ATT_T, :] = _pair_norm(o, g_ref[0], FOX_D).astype(BF16)

    _softmax_attention(
        qcat_at, lambda hh, klen: kcat_ref[:klen, :], lambda hh, klen: vaug_ref[:klen, :],
        causal, finish, s_ref, p_ref)


def _fox_attention(p_att, qaug, kaug, out_gain, layer, w_cast):
    n_pair = FOX_H // 2
    assert n_pair == ATT_PAIRS
    grp = HEAD_GRP // LANES
    gain_blk = (MLA_H * MLA_V_D + SB_H * SB_D) // LANES
    w_in_spec, w_out_spec, w_out_shape = _cast_along(w_cast, layer)
    return pl.pallas_call(
        _fox_body,
        grid=(N_BATCH, n_pair),
        in_specs=[
            pl.BlockSpec((SEQ_LEN, LANES), lambda b, p: (b, 3 * grp + p)),
            pl.BlockSpec((SEQ_LEN, LANES), lambda b, p: (b, 4 * grp + p)),
            pl.BlockSpec((SEQ_LEN, LANES), lambda b, p: (b, 5 * grp + p)),
            pl.BlockSpec((SEQ_LEN, LANES), lambda b, p: (b, p)),
            pl.BlockSpec((SEQ_LEN, LANES), lambda b, p: (b, p)),
            pl.BlockSpec((1, 1, LANES), lambda b, p: (layer, 0, gain_blk + p)),
            w_in_spec,
        ],
        out_specs=[pl.BlockSpec((SEQ_LEN, LANES), lambda b, p: (b, p)), w_out_spec],
        out_shape=[jax.ShapeDtypeStruct((N_TOK, FOX_H * FOX_D), BF16), w_out_shape],
        scratch_shapes=[
            pltpu.VMEM((SEQ_LEN, 2 * LANES), BF16),
            pltpu.VMEM((SEQ_LEN, 2 * LANES), BF16),
        ] + _SCORE_SCRATCH,
        compiler_params=_params(("arbitrary", "arbitrary")),
        name="fox_attention",
    )(p_att, p_att, p_att, qaug, kaug, out_gain, w_cast)


def _sb_body(*refs, with_ada):
    if with_ada:
        (q_ref, k_ref, v_ref, tri_ref, g_ref, w_ref, c_ref, wa_ref, ba_ref,
         o_ref, wb_ref, mod_ref, u_ref, p_ref) = refs
        _ada_columns(c_ref, wa_ref, ba_ref, mod_ref)
    else:
        q_ref, k_ref, v_ref, tri_ref, g_ref, w_ref, o_ref, wb_ref, u_ref, p_ref = refs
    wb_ref[...] = w_ref[0].astype(BF16)

    def score_tiles(i):
        q = q_ref[i * ATT_T:(i + 1) * ATT_T, :]
        zero = jnp.zeros_like(q)
        q_heads = [jnp.where(_lane_band(q.shape, hh * SB_D, SB_D), q, zero) for hh in range(2)]

        def tile(hh, kb):
            u_ref[i % 2, hh, kb] = _dot_nt(q_heads[hh], k_ref[kb * ATT_T:(kb + 1) * ATT_T, :])

        return [functools.partial(tile, hh, kb) for kb in range(i + 1) for hh in range(2)]

    def weights(i, pending):
        nkb = i + 1
        right = [None, None]
        for kb in reversed(range(nkb)):
            on_diagonal = kb == nkb - 1
            for hh in range(2):
                if pending:
                    pending.pop()()
                u = u_ref[i % 2, hh, kb]
                drop = jnp.maximum(u, 0.0) + jnp.log2(1.0 + jnp.exp2(-jnp.abs(u)))
                if on_diagonal:
                    rows, cols = _rel_positions()
                    strict = cols < rows
                    drop = jnp.where(strict, drop, 0.0)
                log_w = (u - drop) + _dot(drop.astype(BF16), tri_ref[...])
                if right[hh] is not None:
                    log_w = log_w - right[hh]
                w = jnp.exp2(log_w)
                if on_diagonal:
                    w = jnp.where(strict, w, 0.0)
                p_ref[i % 2, hh, kb] = w.astype(BF16)
                total = jnp.sum(drop, axis=-1, keepdims=True)
                right[hh] = total if right[hh] is None else right[hh] + total

    for tile in score_tiles(QB_ORDER[0]):
        tile()
    for n, i in enumerate(QB_ORDER):
        nkb = i + 1
        pending = score_tiles(QB_ORDER[n + 1]) if n + 1 < N_QB else []
        weights(i, pending)
        while pending:
            pending.pop()()
        heads = [_weights_times_v(p_ref.at[i % 2], hh, nkb, v_ref[:nkb * ATT_T, :]) for hh in range(2)]
        o = jnp.where(_lane_band(heads[0].shape, 0, SB_D), heads[0], heads[1])
        o_ref[i * ATT_T:(i + 1) * ATT_T, :] = _pair_norm(o, g_ref[0], SB_D).astype(BF16)


def _sb_attention(p_att, out_gain, layer, w_cast, ada_next=None):
    n_pair = SB_H // 2
    assert n_pair == ATT_PAIRS
    grp = HEAD_GRP // LANES
    gain_blk = MLA_H * MLA_V_D // LANES
    w_in_spec, w_out_spec, w_out_shape = _cast_along(w_cast, layer)
    tri = jnp.asarray(-np.tril(np.ones((ATT_T, ATT_T), np.float32), -1), BF16)
    in_specs = [
        pl.BlockSpec((SEQ_LEN, LANES), lambda b, p: (b, p)),
        pl.BlockSpec((SEQ_LEN, LANES), lambda b, p: (b, grp + p)),
        pl.BlockSpec((SEQ_LEN, LANES), lambda b, p: (b, 2 * grp + p)),
        _resident((ATT_T, ATT_T)),
        pl.BlockSpec((1, 1, LANES), lambda b, p: (layer, 0, gain_blk + p)),
        w_in_spec,
    ]
    out_specs = [pl.BlockSpec((SEQ_LEN, LANES), lambda b, p: (b, p)), w_out_spec]
    out_shape = [jax.ShapeDtypeStruct((N_TOK, SB_H * SB_D), BF16), w_out_shape]
    operands = [p_att, p_att, p_att, tri, out_gain, w_cast]
    if ada_next is not None:
        ada_in, ada_out_spec, ada_out_shape = _ada_along(layer + 1)
        in_specs += ada_in
        out_specs.append(ada_out_spec)
        out_shape.append(ada_out_shape)
        operands += list(ada_next)
    return pl.pallas_call(
        functools.partial(_sb_body, with_ada=ada_next is not None),
        grid=(N_BATCH, n_pair),
        in_specs=in_specs,
        out_specs=out_specs,
        out_shape=out_shape,
        scratch_shapes=_SCORE_SCRATCH,
        compiler_params=_params(("arbitrary", "arbitrary")),
        name="sb_attention",
    )(*operands)


def _outproj_body(x_ref, mod_ref, om_ref, os_ref, of_ref, w_ref, gn_ref, o_ref, h_ref):
    n_mla = MLA_H * MLA_V_D
    n_sb = SB_H * SB_D
    mod = mod_ref[0]
    gain = gn_ref[0] * (1.0 + mod[4:5])
    half = x_ref.shape[0] // 2
    for part in range(2):
        rows = slice(part * half, (part + 1) * half)
        y = (_dot(om_ref[rows, :], w_ref[:n_mla, :]) + _dot(os_ref[rows, :], w_ref[n_mla:n_mla + n_sb, :])
             + _dot(of_ref[rows, :], w_ref[n_mla + n_sb:, :]))
        y = x_ref[rows, :] + mod[2:3] * y
        o_ref[rows, :] = y
        h_ref[rows, :] = (_rms(y, gain) + mod[3:4]).astype(BF16)


def _out_project(x2, mod_l, layer, o_mla, o_sb, o_fx, w_out, gn_ffn):
    tm = PROJ_TM
    tiles_per_seq = SEQ_LEN // tm
    row = lambda i: (i, 0)
    return pl.pallas_call(
        _outproj_body,
        grid=(N_TOK // tm,),
        in_specs=[
            pl.BlockSpec((tm, D_MODEL), row),
            pl.BlockSpec((1, 6, D_MODEL), lambda i: (i // tiles_per_seq, 0, 0)),
            pl.BlockSpec((tm, MLA_H * MLA_V_D), row),
            pl.BlockSpec((tm, SB_H * SB_D), row),
            pl.BlockSpec((tm, FOX_H * FOX_D), row),
            _resident((D_MODEL, D_MODEL)),
            _layer_resident((1, D_MODEL), layer),
        ],
        out_specs=[pl.BlockSpec((tm, D_MODEL), row), pl.BlockSpec((tm, D_MODEL), row)],
        out_shape=[jax.ShapeDtypeStruct((N_TOK, D_MODEL), F32), jax.ShapeDtypeStruct((N_TOK, D_MODEL), BF16)],
        compiler_params=_params(("arbitrary",)),
        name="out_project",
    )(x2, mod_l, o_mla, o_sb, o_fx, w_out, gn_ffn)


def _ffn_body(x_ref, h_ref, mod_ref, w1_ref, w2_ref, gf_ref, o_ref, *, final):
    j = pl.program_id(1)
    mod = mod_ref[0]

    @pl.when(j == 0)
    def _():
        o_ref[...] = x_ref[...]

    a = jnp.maximum(_dot(h_ref[...], w1_ref[...]), 0.0)
    o_ref[...] += mod[5:6] * _dot((a * a).astype(BF16), w2_ref[...])

    if final:
        @pl.when(j == pl.num_programs(1) - 1)
        def _():
            inv = _inv_rms(o_ref)
            o_ref[...] = o_ref[...] * inv * gf_ref[...]


def _ffn(x2, h2, mod_l, layer, w1, w2, g_final, final):
    tm, tf = FFN_TM, FFN_TF
    tiles_per_seq = SEQ_LEN // tm
    return pl.pallas_call(
        functools.partial(_ffn_body, final=final),
        grid=(N_TOK // tm, FF_DIM // tf),
        in_specs=[
            pl.BlockSpec((tm, D_MODEL), lambda i, j: (i, 0)),
            pl.BlockSpec((tm, D_MODEL), lambda i, j: (i, 0)),
            pl.BlockSpec((1, 6, D_MODEL), lambda i, j: (i // tiles_per_seq, 0, 0)),
            pl.BlockSpec((D_MODEL, tf), lambda i, j: (0, j)),
            pl.BlockSpec((tf, D_MODEL), lambda i, j: (j, 0)),
            pl.BlockSpec((1, D_MODEL), lambda i, j: (0, 0)),
        ],
        out_specs=pl.BlockSpec((tm, D_MODEL), lambda i, j: (i, 0)),
        out_shape=jax.ShapeDtypeStruct((N_TOK, D_MODEL), F32),
        compiler_params=_params(("arbitrary", "arbitrary")),
        name="ffn",
    )(x2, h2, mod_l, w1, w2, g_final)


def _swap_halves(w):
    half = w.shape[-1] // 2
    return jnp.concatenate([w[..., half:], w[..., :half]], axis=-1)


def _pack_w_in(w):
    wt = jnp.swapaxes(w, 1, 2).astype(BF16)
    kr = wt[:, R_KR:R_ATT]
    kr_sw = jnp.concatenate([kr[:, MLA_ROPE_D // 2:], kr[:, :MLA_ROPE_D // 2]], axis=1)
    wt_kr = jnp.concatenate([kr, kr, kr_sw, kr_sw], axis=1)
    wt_f = jnp.pad(wt[:, R_F:], ((0, 0), (0, LANES - FOX_H), (0, 0)))
    return wt, wt_kr, wt_f


def _pack_w_uq(w):
    w4 = w.astype(BF16).reshape(N_LAYERS, MLA_QL, MLA_H, MLA_NOPE_D + MLA_ROPE_D)
    flat = lambda a: a.reshape(N_LAYERS, MLA_QL, -1)
    rope = w4[..., MLA_NOPE_D:]
    return jnp.concatenate([flat(w4[..., :MLA_NOPE_D]), flat(rope), flat(_swap_halves(rope))], axis=-1)


def _pack_w_ukv(w):
    w4 = w.astype(BF16).reshape(N_LAYERS, MLA_KVL, MLA_H, MLA_NOPE_D + MLA_V_D)
    flat = lambda a: a.reshape(N_LAYERS, MLA_KVL, -1)
    return jnp.concatenate([flat(w4[..., :MLA_NOPE_D]), flat(w4[..., MLA_NOPE_D:])], axis=-1)


def _rope_tables():
    pos = jnp.arange(SEQ_LEN, dtype=F32)
    inv = ROPE_THETA ** (-jnp.arange(0, MLA_ROPE_D, 2, dtype=F32) / MLA_ROPE_D)
    ang = pos[:, None] * inv[None, :]
    cos, sin = jnp.cos(ang), jnp.sin(ang)
    cos_t = jnp.tile(jnp.concatenate([cos, cos], axis=1), (1, MLA_H))
    sin_t = jnp.tile(jnp.concatenate([-sin, sin], axis=1), (1, MLA_H))
    return cos_t, sin_t


def kernel(x, c, w_ada, b_ada, norm_mix, w_in, q_norm, w_uq, kv_norm, w_ukv, b_forget, out_norm,
           w_out, norm_ffn, w_ff1, w_ff2, final_norm):
    assert x.shape == (N_BATCH, SEQ_LEN, D_MODEL) and c.shape == (N_BATCH, D_MODEL)
    cos_t, sin_t = _rope_tables()
    c_pad = jnp.pad(c, ((0, SUBLANES - N_BATCH), (0, 0)))
    b_ada3 = b_ada.reshape(N_LAYERS, 1, 6 * D_MODEL)
    per_batch = lambda m: m[:N_BATCH].reshape(N_BATCH, 6, D_MODEL)
    mod_l = per_batch(_ada_mod(c_pad, w_ada, b_ada3, 0))

    rows = lambda v: v.reshape(N_LAYERS, 1, -1)
    assert w_in.shape == (N_LAYERS, D_MODEL, IN_W)
    wt_in, wt_kr, wt_f = _pack_w_in(w_in)
    wuq_p, wukv_p = _pack_w_uq(w_uq), _pack_w_ukv(w_ukv)
    bf_pad = jnp.pad(b_forget, ((0, 0), (0, LANES - FOX_H))).reshape(N_LAYERS, 1, LANES)
    out_gain = rows(out_norm)

    x2 = x.reshape(N_TOK, D_MODEL)
    for l in range(N_LAYERS):
        q_mla, kv_mla, krope, p_att, logf = _project(
            x2, mod_l, l, rows(norm_mix), wt_in, wt_kr, wt_f, rows(q_norm), wuq_p, rows(kv_norm), wukv_p, bf_pad,
            cos_t, sin_t)
        qaug, kaug = _fgate(logf)
        o_mla, wff1_b = _mla_attention(q_mla, kv_mla, krope, out_gain, l, w_ff1)
        if l + 1 < N_LAYERS:
            o_sb, wout_b, mod_next = _sb_attention(p_att, out_gain, l, w_out, (c_pad, w_ada, b_ada3))
        else:
            (o_sb, wout_b), mod_next = _sb_attention(p_att, out_gain, l, w_out), None
        o_fx, wff2_b = _fox_attention(p_att, qaug, kaug, out_gain, l, w_ff2)
        x2, h_ffn = _out_project(x2, mod_l, l, o_mla, o_sb, o_fx, wout_b, rows(norm_ffn))
        x2 = _ffn(x2, h_ffn, mod_l, l, wff1_b, wff2_b, final_norm.reshape(1, -1),
                  final=(l == N_LAYERS - 1))
        mod_l = per_batch(mod_next) if mod_next is not None else None
    return x2.reshape(N_BATCH, SEQ_LEN, D_MODEL)
```

```python
import functools

import numpy as np
import jax
import jax.numpy as jnp
from jax import lax
from jax.experimental import pallas as pl
from jax.experimental.pallas import tpu as pltpu

F32 = jnp.float32
BF16 = jnp.bfloat16

D_MODEL = 2048
N_BATCH = 4
SEQ_LEN = 2048
N_LAYERS = 2
CHUNK_LEN = 64
CHUNK_SHIFT = CHUNK_LEN.bit_length() - 1
assert 1 << CHUNK_SHIFT == CHUNK_LEN
NORM_EPS = 1e-6
LOG2_E = 1.4426950408889634
MLA_H = 8
MLA_QL = 512
MLA_KVL = 256
MLA_NOPE_D = 128
MLA_ROPE_D = 64
MLA_V_D = 128
ROPE_THETA = 10000.0
SB_H = 8
SB_D = 64
FOX_H = 8
FOX_D = 64
FF_DIM = 4 * D_MODEL
N_TOK = N_BATCH * SEQ_LEN

LANES = 128
SUBLANES = 8
VMEM_LIMIT = 56 * 1024 * 1024

R_CQ = 0
R_CKV = R_CQ + MLA_QL
R_KR = R_CKV + MLA_KVL
R_ATT = R_KR + MLA_ROPE_D
ATT_W = 3 * SB_H * SB_D + 3 * FOX_H * FOX_D
R_F = R_ATT + ATT_W
IN_W = R_F + FOX_H
HEAD_GRP = SB_H * SB_D

PROJ_TM = 512
FFN_TM = 1024
FFN_TF = 512
ADA_TN = 1024
ATT_PAIRS = 4
ATT_T = 256
N_QB = SEQ_LEN // ATT_T
QB_ORDER = tuple(reversed(range(N_QB)))
CUM_T = 256


def _dot(a, b):
    return jnp.dot(a, b, preferred_element_type=F32)


def _dot_nt(a, b):
    return lax.dot_general(a, b, (((1,), (1,)), ((), ())), preferred_element_type=F32)


def _split3(x):
    hi = x.astype(BF16)
    r1 = x - hi.astype(F32)
    mid = r1.astype(BF16)
    lo = (r1 - mid.astype(F32)).astype(BF16)
    return hi, mid, lo


def _softplus(z):
    return jnp.maximum(z, 0.0) + jnp.log(1.0 + jnp.exp(-jnp.abs(z)))


def _rms(x, gain):
    return x * lax.rsqrt(jnp.mean(x * x, axis=-1, keepdims=True) + NORM_EPS) * gain


def _inv_rms(x_ref):
    x = x_ref[...]
    return lax.rsqrt(jnp.mean(x * x, axis=-1, keepdims=True) + NORM_EPS)


def _modulated_norm(x_ref, norm_gain, mod_scale, mod_shift, h_ref):
    inv = _inv_rms(x_ref)
    gain = norm_gain * (1.0 + mod_scale)
    h_ref[...] = (x_ref[...] * inv * gain + mod_shift).astype(BF16)


def _resident(shape):
    nd = len(shape)
    return pl.BlockSpec(shape, lambda *_: (0,) * nd, pipeline_mode=pl.Buffered(1))


def _layer_resident(shape, layer):
    nd = len(shape)
    return pl.BlockSpec((1,) + shape, lambda *_: (layer,) + (0,) * nd, pipeline_mode=pl.Buffered(1))


def _params(sem):
    return pltpu.CompilerParams(dimension_semantics=sem, vmem_limit_bytes=VMEM_LIMIT)


def _ada_columns(c_ref, w_ref, b_ref, o_ref):
    c = c_ref[...]
    c_act = (c * jax.nn.sigmoid(c)).astype(BF16)
    o_ref[...] = _dot(c_act, w_ref[0].astype(BF16)) + b_ref[0]


def _ada_mod(c_pad, w_ada, b_ada3, layer):
    n_out = 6 * D_MODEL
    return pl.pallas_call(
        _ada_columns,
        grid=(n_out // ADA_TN,),
        in_specs=[
            pl.BlockSpec((SUBLANES, D_MODEL), lambda j: (0, 0)),
            pl.BlockSpec((1, D_MODEL, ADA_TN), lambda j: (layer, 0, j)),
            pl.BlockSpec((1, 1, ADA_TN), lambda j: (layer, 0, j)),
        ],
        out_specs=pl.BlockSpec((SUBLANES, ADA_TN), lambda j: (0, j)),
        out_shape=jax.ShapeDtypeStruct((SUBLANES, n_out), F32),
        compiler_params=_params(("arbitrary",)),
        name="ada_mod",
    )(c_pad, w_ada, b_ada3)


def _ada_along(layer):
    n_out = 6 * D_MODEL
    slab = n_out // (N_BATCH * ATT_PAIRS)
    assert slab * N_BATCH * ATT_PAIRS == n_out and slab % LANES == 0
    step = lambda b, p: b * ATT_PAIRS + p
    in_specs = [
        pl.BlockSpec((SUBLANES, D_MODEL), lambda b, p: (0, 0)),
        pl.BlockSpec((1, D_MODEL, slab), lambda b, p: (layer, 0, step(b, p))),
        pl.BlockSpec((1, 1, slab), lambda b, p: (layer, 0, step(b, p))),
    ]
    out_spec = pl.BlockSpec((SUBLANES, slab), lambda b, p: (0, step(b, p)))
    return in_specs, out_spec, jax.ShapeDtypeStruct((SUBLANES, n_out), F32)


def _proj_body(x_ref, mod_ref, gn_ref, wt_ref, wkr_ref, wf_ref, qn_ref, wuq_ref, kvn_ref, wukv_ref, bf_ref,
               cos_ref, sin_ref, q_out, kv_out, kr_out, att_out, lf_out, h_ref):
    mod = mod_ref[0]
    _modulated_norm(x_ref, gn_ref[0], mod[1:2], mod[0:1], h_ref)

    def proj(r0, width):
        return _dot_nt(h_ref[...], wt_ref[0, r0:r0 + width, :])

    mla_scale = (MLA_NOPE_D + MLA_ROPE_D) ** -0.5 * LOG2_E
    n_nope = MLA_H * MLA_NOPE_D
    n_rope = MLA_H * MLA_ROPE_D

    cq = _rms(proj(R_CQ, MLA_QL), qn_ref[0]).astype(BF16)
    q = _dot(cq, wuq_ref[0])
    q_out[:, :n_nope] = (q[:, :n_nope] * mla_scale).astype(BF16)
    q_rot = q[:, n_nope:n_nope + n_rope] * cos_ref[...] + q[:, n_nope + n_rope:] * sin_ref[...]
    q_out[:, n_nope:] = (q_rot * mla_scale).astype(BF16)

    ckv = _rms(proj(R_CKV, MLA_KVL), kvn_ref[0]).astype(BF16)
    kv_out[...] = _dot(ckv, wukv_ref[0]).astype(BF16)

    kr = _dot_nt(h_ref[...], wkr_ref[0])
    kr_out[...] = (kr[:, :LANES] * cos_ref[:, :LANES] + kr[:, LANES:] * sin_ref[:, :LANES]).astype(BF16)

    lf_out[...] = -_softplus(-(_dot_nt(h_ref[...], wf_ref[0]) + bf_ref[0]))

    sb_scale = SB_D ** -0.5 * LOG2_E
    fox_scale = FOX_D ** -0.5 * LOG2_E
    scales = (sb_scale, 1.0, 1.0, fox_scale, 1.0, 1.0)
    for g, sc in enumerate(scales):
        blk = proj(R_ATT + g * HEAD_GRP, HEAD_GRP)
        if sc != 1.0:
            blk = blk * sc
        att_out[:, g * HEAD_GRP:(g + 1) * HEAD_GRP] = blk.astype(BF16)


def _project(x2, mod_l, layer, gn, wt_in, wt_kr, wt_f, qn, wuq, kvn, wukv, bfp, cos_t, sin_t):
    tm = PROJ_TM
    tiles_per_seq = SEQ_LEN // tm
    q_w = MLA_H * (MLA_NOPE_D + MLA_ROPE_D)
    kv_w = MLA_H * (MLA_NOPE_D + MLA_V_D)
    rope_w = MLA_H * MLA_ROPE_D
    row = lambda i: (i, 0)
    return pl.pallas_call(
        _proj_body,
        grid=(N_TOK // tm,),
        in_specs=[
            pl.BlockSpec((tm, D_MODEL), row),
            pl.BlockSpec((1, 6, D_MODEL), lambda i: (i // tiles_per_seq, 0, 0)),
            _layer_resident((1, D_MODEL), layer),
            _layer_resident((IN_W, D_MODEL), layer),
            _layer_resident((2 * LANES, D_MODEL), layer),
            _layer_resident((LANES, D_MODEL), layer),
            _layer_resident((1, MLA_QL), layer),
            _layer_resident((MLA_QL, q_w + rope_w), layer),
            _layer_resident((1, MLA_KVL), layer),
            _layer_resident((MLA_KVL, kv_w), layer),
            _layer_resident((1, LANES), layer),
            pl.BlockSpec((tm, rope_w), lambda i: (i % tiles_per_seq, 0)),
            pl.BlockSpec((tm, rope_w), lambda i: (i % tiles_per_seq, 0)),
        ],
        out_specs=[
            pl.BlockSpec((tm, q_w), row),
            pl.BlockSpec((tm, kv_w), row),
            pl.BlockSpec((tm, LANES), row),
            pl.BlockSpec((tm, ATT_W), row),
            pl.BlockSpec((tm, LANES), row),
        ],
        out_shape=[
            jax.ShapeDtypeStruct((N_TOK, q_w), BF16),
            jax.ShapeDtypeStruct((N_TOK, kv_w), BF16),
            jax.ShapeDtypeStruct((N_TOK, LANES), BF16),
            jax.ShapeDtypeStruct((N_TOK, ATT_W), BF16),
            jax.ShapeDtypeStruct((N_TOK, LANES), F32),
        ],
        scratch_shapes=[pltpu.VMEM((tm, D_MODEL), BF16)],
        compiler_params=_params(("arbitrary",)),
        name="project",
    )(x2, mod_l, gn, wt_in, wt_kr, wt_f, qn, wuq, kvn, wukv, bfp, cos_t, sin_t)


def _fgate_body(lf_ref, tri_ref, pq_ref, pk_ref, cq_ref, ck_ref, qa_out, ka_out):
    n_blk = SEQ_LEN // CUM_T
    local = []
    for blk in range(n_blk):
        hi, mid, lo = _split3(lf_ref[blk * CUM_T:(blk + 1) * CUM_T, :] * LOG2_E)
        tri = tri_ref[...]
        local.append(_dot(tri, hi) + _dot(tri, mid) + _dot(tri, lo))
    carry = jnp.zeros((1, LANES), F32)
    for blk in range(n_blk):
        rows = slice(blk * CUM_T, (blk + 1) * CUM_T)
        f_blk = local[blk] + carry
        carry = f_blk[CUM_T - 1:CUM_T, :]
        f_hi, f_mid, f_lo = _split3(f_blk)
        qa = _dot(f_hi, pq_ref[0]) + _dot(f_mid, pq_ref[1]) + _dot(f_lo, pq_ref[2]) + cq_ref[...]
        ka = ck_ref[...] - (_dot(f_hi, pk_ref[0]) + _dot(f_mid, pk_ref[1]) + _dot(f_lo, pk_ref[2]))
        qa_out[rows, :] = qa.astype(BF16)
        ka_out[rows, :] = ka.astype(BF16)


def _fgate_constants():
    tri = np.tril(np.ones((CUM_T, CUM_T), np.float32))
    n_pair = FOX_H // 2
    n_terms = 3
    assert 2 * n_terms <= SUBLANES
    pq = np.zeros((n_terms, LANES, n_pair * LANES), np.float32)
    pk = np.zeros((n_terms, LANES, n_pair * LANES), np.float32)
    cq = np.zeros((1, n_pair * LANES), np.float32)
    ck = np.zeros((1, n_pair * LANES), np.float32)
    for h in range(FOX_H):
        base = (h // 2) * LANES + (h % 2) * SUBLANES
        for t in range(n_terms):
            pq[t, h, base + t] = 1.0
            pk[t, h, base + n_terms + t] = 1.0
            cq[0, base + n_terms + t] = 1.0
            ck[0, base + t] = 1.0
    as_bf16 = lambda a: jnp.asarray(a, BF16)
    return as_bf16(tri), as_bf16(pq), as_bf16(pk), jnp.asarray(cq), jnp.asarray(ck)


def _fgate(logf):
    tri, pq, pk, cq, ck = _fgate_constants()
    aug_w = (FOX_H // 2) * LANES
    seq_blk = lambda b: (b, 0)
    return pl.pallas_call(
        _fgate_body,
        grid=(N_BATCH,),
        in_specs=[
            pl.BlockSpec((SEQ_LEN, LANES), seq_blk),
            _resident((CUM_T, CUM_T)),
            _resident((3, LANES, aug_w)),
            _resident((3, LANES, aug_w)),
            _resident((1, aug_w)),
            _resident((1, aug_w)),
        ],
        out_specs=[pl.BlockSpec((SEQ_LEN, aug_w), seq_blk), pl.BlockSpec((SEQ_LEN, aug_w), seq_blk)],
        out_shape=[jax.ShapeDtypeStruct((N_TOK, aug_w), BF16), jax.ShapeDtypeStruct((N_TOK, aug_w), BF16)],
        compiler_params=_params(("arbitrary",)),
        name="fgate",
    )(logf, tri, pq, pk, cq, ck)


def _lane_band(shape, lo, width):
    lane = lax.broadcasted_iota(jnp.int32, shape, len(shape) - 1)
    return (lane >= lo) & (lane < lo + width)


def _rel_positions():
    rows = lax.broadcasted_iota(jnp.int32, (ATT_T, ATT_T), 0)
    cols = lax.broadcasted_iota(jnp.int32, (ATT_T, ATT_T), 1)
    return rows, cols


def _col_block(a, kb):
    return a[:, kb * ATT_T:(kb + 1) * ATT_T]


def _weights_times_v(p_ref, hh, nkb, v):
    p = jnp.concatenate([p_ref[hh, kb] for kb in range(nkb)], axis=1) if nkb > 1 else p_ref[hh, 0]
    return _dot(p, v)


def _softmax_attention(qcat_at, kcat_at, vaug_at, mask_fn, finish, s_ref, p_ref):
    def scores(i):
        nkb = i + 1
        row_max = []
        for hh in range(2):
            s = _dot_nt(qcat_at(i, hh), kcat_at(hh, nkb * ATT_T))
            blocks = [_col_block(s, kb) for kb in range(nkb)]
            blocks[-1] = jnp.where(mask_fn(*_rel_positions()), blocks[-1], -jnp.inf)
            m = None
            for kb in range(nkb):
                s_ref[i % 2, hh, kb] = blocks[kb]
                bm = jnp.maximum(blocks[kb][:, :LANES], blocks[kb][:, LANES:])
                m = bm if m is None else jnp.maximum(m, bm)
            row_max.append(jnp.max(m, axis=-1, keepdims=True))
        return row_max

    row_max = scores(QB_ORDER[0])
    for n, i in enumerate(QB_ORDER):
        nkb = i + 1
        next_max = scores(QB_ORDER[n + 1]) if n + 1 < N_QB else None
        for kb in range(nkb):
            for hh in range(2):
                p_ref[i % 2, hh, kb] = jnp.exp2(s_ref[i % 2, hh, kb] - row_max[hh]).astype(BF16)
        finish(i, [_weights_times_v(p_ref.at[i % 2], hh, nkb, vaug_at(hh, nkb * ATT_T)) for hh in range(2)])
        row_max = next_max


def _pair_norm(o, gain, head_d):
    first = _lane_band(o.shape, 0, head_d)
    sq = o * o
    ss_a = jnp.sum(jnp.where(first, sq, 0.0), axis=-1, keepdims=True)
    ss_b = jnp.sum(sq, axis=-1, keepdims=True) - ss_a
    inv = jnp.where(first, lax.rsqrt(ss_a / head_d + NORM_EPS), lax.rsqrt(ss_b / head_d + NORM_EPS))
    return o * inv * gain


_SCORE_SCRATCH = [
    pltpu.VMEM((2, 2, N_QB, ATT_T, ATT_T), F32),
    pltpu.VMEM((2, 2, N_QB, ATT_T, ATT_T), BF16),
]
def _cast_along(w, layer):
    _, n_rows, n_cols = w.shape
    slab = n_rows // (N_BATCH * ATT_PAIRS)
    assert slab * N_BATCH * ATT_PAIRS == n_rows and slab % (2 * SUBLANES) == 0
    in_spec = pl.BlockSpec((1, slab, n_cols), lambda b, p: (layer, b * ATT_PAIRS + p, 0))
    out_spec = pl.BlockSpec((slab, n_cols), lambda b, p: (b * ATT_PAIRS + p, 0))
    return in_spec, out_spec, jax.ShapeDtypeStruct((n_rows, n_cols), BF16)


def _mla_body(qn_ref, qr_ref, kn_ref, v_ref, kr_ref, g_ref, w_ref, o_ref, wb_ref, kcat_ref, vaug_ref,
              s_ref, p_ref):
    wb_ref[...] = w_ref[0].astype(BF16)
    d = MLA_NOPE_D
    for hh in range(2):
        kcat_ref[hh, :, :d] = kn_ref[:, hh * d:(hh + 1) * d]
        kcat_ref[hh, :, d:] = kr_ref[...]
        vaug_ref[hh, :, :d] = v_ref[:, hh * d:(hh + 1) * d]
        vaug_ref[hh, :, d:] = jnp.ones((SEQ_LEN, LANES), BF16)

    def chunk_mask(rows, cols):
        return cols < ((rows >> CHUNK_SHIFT) + 1) * CHUNK_LEN

    def qcat_at(i, hh):
        r = slice(i * ATT_T, (i + 1) * ATT_T)
        qr = qr_ref[r, :]
        qr = jnp.where(_lane_band(qr.shape, hh * MLA_ROPE_D, MLA_ROPE_D), qr, jnp.zeros_like(qr))
        return jnp.concatenate([qn_ref[r, hh * d:(hh + 1) * d], qr], axis=1)

    def finish(i, outs):
        r = slice(i * ATT_T, (i + 1) * ATT_T)
        for hh in range(2):
            o = outs[hh][:, :d] / outs[hh][:, d:]
            o_ref[r, hh * d:(hh + 1) * d] = _rms(o, g_ref[0, :, hh * d:(hh + 1) * d]).astype(BF16)

    _softmax_attention(
        qcat_at, lambda hh, klen: kcat_ref[hh, :klen, :], lambda hh, klen: vaug_ref[hh, :klen, :],
        chunk_mask, finish, s_ref, p_ref)


def _mla_attention(q_mla, kv_mla, krope, out_gain, layer, w_cast):
    n_pair = MLA_H // 2
    assert n_pair == ATT_PAIRS
    pair_w = 2 * MLA_NOPE_D
    w_in_spec, w_out_spec, w_out_shape = _cast_along(w_cast, layer)
    return pl.pallas_call(
        _mla_body,
        grid=(N_BATCH, n_pair),
        in_specs=[
            pl.BlockSpec((SEQ_LEN, pair_w), lambda b, p: (b, p)),
            pl.BlockSpec((SEQ_LEN, LANES), lambda b, p: (b, MLA_H * MLA_NOPE_D // LANES + p)),
            pl.BlockSpec((SEQ_LEN, pair_w), lambda b, p: (b, p)),
            pl.BlockSpec((SEQ_LEN, pair_w), lambda b, p: (b, n_pair + p)),
            pl.BlockSpec((SEQ_LEN, LANES), lambda b, p: (b, 0)),
            pl.BlockSpec((1, 1, pair_w), lambda b, p: (layer, 0, p)),
            w_in_spec,
        ],
        out_specs=[pl.BlockSpec((SEQ_LEN, pair_w), lambda b, p: (b, p)), w_out_spec],
        out_shape=[jax.ShapeDtypeStruct((N_TOK, MLA_H * MLA_V_D), BF16), w_out_shape],
        scratch_shapes=[
            pltpu.VMEM((2, SEQ_LEN, 2 * LANES), BF16),
            pltpu.VMEM((2, SEQ_LEN, 2 * LANES), BF16),
        ] + _SCORE_SCRATCH,
        compiler_params=_params(("arbitrary", "arbitrary")),
        name="mla_attention",
    )(q_mla, q_mla, kv_mla, kv_mla, krope, out_gain, w_cast)


def _fox_body(q_ref, k_ref, v_ref, qa_ref, ka_ref, g_ref, w_ref, o_ref, wb_ref, kcat_ref, vaug_ref,
              s_ref, p_ref):
    wb_ref[...] = w_ref[0].astype(BF16)
    kcat_ref[:, :LANES] = k_ref[...]
    kcat_ref[:, LANES:] = ka_ref[...]
    vaug_ref[:, :LANES] = v_ref[...]
    vaug_ref[:, LANES:] = jnp.ones((SEQ_LEN, LANES), BF16)

    def causal(rows, cols):
        return cols <= rows

    def qcat_at(i, hh):
        r = slice(i * ATT_T, (i + 1) * ATT_T)
        q = q_ref[r, :]
        qa = qa_ref[r, :]
        zero = jnp.zeros_like(q)
        return jnp.concatenate([
            jnp.where(_lane_band(q.shape, hh * FOX_D, FOX_D), q, zero),
            jnp.where(_lane_band(qa.shape, hh * SUBLANES, SUBLANES), qa, zero)], axis=1)

    def finish(i, outs):
        heads = [outs[hh][:, :LANES] / outs[hh][:, LANES:] for hh in range(2)]
        o = jnp.where(_lane_band(heads[0].shape, 0, FOX_D), heads[0], heads[1])
        o_ref[i * ATT_T:(i + 1) * ATT_T, :] = _pair_norm(o, g_ref[0], FOX_D).astype(BF16)

    _softmax_attention(
        qcat_at, lambda hh, klen: kcat_ref[:klen, :], lambda hh, klen: vaug_ref[:klen, :],
        causal, finish, s_ref, p_ref)


def _fox_attention(p_att, qaug, kaug, out_gain, layer, w_cast):
    n_pair = FOX_H // 2
    assert n_pair == ATT_PAIRS
    grp = HEAD_GRP // LANES
    gain_blk = (MLA_H * MLA_V_D + SB_H * SB_D) // LANES
    w_in_spec, w_out_spec, w_out_shape = _cast_along(w_cast, layer)
    return pl.pallas_call(
        _fox_body,
        grid=(N_BATCH, n_pair),
        in_specs=[
            pl.BlockSpec((SEQ_LEN, LANES), lambda b, p: (b, 3 * grp + p)),
            pl.BlockSpec((SEQ_LEN, LANES), lambda b, p: (b, 4 * grp + p)),
            pl.BlockSpec((SEQ_LEN, LANES), lambda b, p: (b, 5 * grp + p)),
            pl.BlockSpec((SEQ_LEN, LANES), lambda b, p: (b, p)),
            pl.BlockSpec((SEQ_LEN, LANES), lambda b, p: (b, p)),
            pl.BlockSpec((1, 1, LANES), lambda b, p: (layer, 0, gain_blk + p)),
            w_in_spec,
        ],
        out_specs=[pl.BlockSpec((SEQ_LEN, LANES), lambda b, p: (b, p)), w_out_spec],
        out_shape=[jax.ShapeDtypeStruct((N_TOK, FOX_H * FOX_D), BF16), w_out_shape],
        scratch_shapes=[
            pltpu.VMEM((SEQ_LEN, 2 * LANES), BF16),
            pltpu.VMEM((SEQ_LEN, 2 * LANES), BF16),
        ] + _SCORE_SCRATCH,
        compiler_params=_params(("arbitrary", "arbitrary")),
        name="fox_attention",
    )(p_att, p_att, p_att, qaug, kaug, out_gain, w_cast)


def _sb_body(*refs, with_ada):
    if with_ada:
        (q_ref, k_ref, v_ref, tri_ref, g_ref, w_ref, c_ref, wa_ref, ba_ref,
         o_ref, wb_ref, mod_ref, u_ref, p_ref) = refs
        _ada_columns(c_ref, wa_ref, ba_ref, mod_ref)
    else:
        q_ref, k_ref, v_ref, tri_ref, g_ref, w_ref, o_ref, wb_ref, u_ref, p_ref = refs
    wb_ref[...] = w_ref[0].astype(BF16)

    def score_tiles(i):
        q = q_ref[i * ATT_T:(i + 1) * ATT_T, :]
        zero = jnp.zeros_like(q)
        q_heads = [jnp.where(_lane_band(q.shape, hh * SB_D, SB_D), q, zero) for hh in range(2)]

        def tile(hh, kb):
            u_ref[i % 2, hh, kb] = _dot_nt(q_heads[hh], k_ref[kb * ATT_T:(kb + 1) * ATT_T, :])

        return [functools.partial(tile, hh, kb) for kb in range(i + 1) for hh in range(2)]

    def weights(i, pending):
        nkb = i + 1
        right = [None, None]
        for kb in reversed(range(nkb)):
            on_diagonal = kb == nkb - 1
            for hh in range(2):
                if pending:
                    pending.pop()()
                u = u_ref[i % 2, hh, kb]
                drop = jnp.maximum(u, 0.0) + jnp.log2(1.0 + jnp.exp2(-jnp.abs(u)))
                if on_diagonal:
                    rows, cols = _rel_positions()
                    strict = cols < rows
                    drop = jnp.where(strict, drop, 0.0)
                log_w = (u - drop) + _dot(drop.astype(BF16), tri_ref[...])
                if right[hh] is not None:
                    log_w = log_w - right[hh]
                w = jnp.exp2(log_w)
                if on_diagonal:
                    w = jnp.where(strict, w, 0.0)
                p_ref[i % 2, hh, kb] = w.astype(BF16)
                total = jnp.sum(drop, axis=-1, keepdims=True)
                right[hh] = total if right[hh] is None else right[hh] + total

    for tile in score_tiles(QB_ORDER[0]):
        tile()
    for n, i in enumerate(QB_ORDER):
        nkb = i + 1
        pending = score_tiles(QB_ORDER[n + 1]) if n + 1 < N_QB else []
        weights(i, pending)
        while pending:
            pending.pop()()
        heads = [_weights_times_v(p_ref.at[i % 2], hh, nkb, v_ref[:nkb * ATT_T, :]) for hh in range(2)]
        o = jnp.where(_lane_band(heads[0].shape, 0, SB_D), heads[0], heads[1])
        o_ref[i * ATT_T:(i + 1) * ATT_T, :] = _pair_norm(o, g_ref[0], SB_D).astype(BF16)


def _sb_attention(p_att, out_gain, layer, w_cast, ada_next=None):
    n_pair = SB_H // 2
    assert n_pair == ATT_PAIRS
    grp = HEAD_GRP // LANES
    gain_blk = MLA_H * MLA_V_D // LANES
    w_in_spec, w_out_spec, w_out_shape = _cast_along(w_cast, layer)
    tri = jnp.asarray(-np.tril(np.ones((ATT_T, ATT_T), np.float32), -1), BF16)
    in_specs = [
        pl.BlockSpec((SEQ_LEN, LANES), lambda b, p: (b, p)),
        pl.BlockSpec((SEQ_LEN, LANES), lambda b, p: (b, grp + p)),
        pl.BlockSpec((SEQ_LEN, LANES), lambda b, p: (b, 2 * grp + p)),
        _resident((ATT_T, ATT_T)),
        pl.BlockSpec((1, 1, LANES), lambda b, p: (layer, 0, gain_blk + p)),
        w_in_spec,
    ]
    out_specs = [pl.BlockSpec((SEQ_LEN, LANES), lambda b, p: (b, p)), w_out_spec]
    out_shape = [jax.ShapeDtypeStruct((N_TOK, SB_H * SB_D), BF16), w_out_shape]
    operands = [p_att, p_att, p_att, tri, out_gain, w_cast]
    if ada_next is not None:
        ada_in, ada_out_spec, ada_out_shape = _ada_along(layer + 1)
        in_specs += ada_in
        out_specs.append(ada_out_spec)
        out_shape.append(ada_out_shape)
        operands += list(ada_next)
    return pl.pallas_call(
        functools.partial(_sb_body, with_ada=ada_next is not None),
        grid=(N_BATCH, n_pair),
        in_specs=in_specs,
        out_specs=out_specs,
        out_shape=out_shape,
        scratch_shapes=_SCORE_SCRATCH,
        compiler_params=_params(("arbitrary", "arbitrary")),
        name="sb_attention",
    )(*operands)


def _outproj_body(x_ref, mod_ref, om_ref, os_ref, of_ref, w_ref, gn_ref, o_ref, h_ref):
    n_mla = MLA_H * MLA_V_D
    n_sb = SB_H * SB_D
    mod = mod_ref[0]
    gain = gn_ref[0] * (1.0 + mod[4:5])
    half = x_ref.shape[0] // 2
    for part in range(2):
        rows = slice(part * half, (part + 1) * half)
        y = (_dot(om_ref[rows, :], w_ref[:n_mla, :]) + _dot(os_ref[rows, :], w_ref[n_mla:n_mla + n_sb, :])
             + _dot(of_ref[rows, :], w_ref[n_mla + n_sb:, :]))
        y = x_ref[rows, :] + mod[2:3] * y
        o_ref[rows, :] = y
        h_ref[rows, :] = (_rms(y, gain) + mod[3:4]).astype(BF16)


def _out_project(x2, mod_l, layer, o_mla, o_sb, o_fx, w_out, gn_ffn):
    tm = PROJ_TM
    tiles_per_seq = SEQ_LEN // tm
    row = lambda i: (i, 0)
    return pl.pallas_call(
        _outproj_body,
        grid=(N_TOK // tm,),
        in_specs=[
            pl.BlockSpec((tm, D_MODEL), row),
            pl.BlockSpec((1, 6, D_MODEL), lambda i: (i // tiles_per_seq, 0, 0)),
            pl.BlockSpec((tm, MLA_H * MLA_V_D), row),
            pl.BlockSpec((tm, SB_H * SB_D), row),
            pl.BlockSpec((tm, FOX_H * FOX_D), row),
            _resident((D_MODEL, D_MODEL)),
            _layer_resident((1, D_MODEL), layer),
        ],
        out_specs=[pl.BlockSpec((tm, D_MODEL), row), pl.BlockSpec((tm, D_MODEL), row)],
        out_shape=[jax.ShapeDtypeStruct((N_TOK, D_MODEL), F32), jax.ShapeDtypeStruct((N_TOK, D_MODEL), BF16)],
        compiler_params=_params(("arbitrary",)),
        name="out_project",
    )(x2, mod_l, o_mla, o_sb, o_fx, w_out, gn_ffn)


def _ffn_body(x_ref, h_ref, mod_ref, w1_ref, w2_ref, *rest, final):
    gf_ref, o_ref = rest if final else (None,) + rest
    j = pl.program_id(1)
    mod = mod_ref[0]

    @pl.when(j == 0)
    def _():
        o_ref[...] = x_ref[...]

    a = jnp.maximum(_dot(h_ref[...], w1_ref[...]), 0.0)
    o_ref[...] += mod[5:6] * _dot((a * a).astype(BF16), w2_ref[...])

    if final:
        @pl.when(j == pl.num_programs(1) - 1)
        def _():
            inv = _inv_rms(o_ref)
            o_ref[...] = o_ref[...] * inv * gf_ref[...]


def _ffn(x2, h2, mod_l, w1, w2, g_final=None):
    tm, tf = FFN_TM, FFN_TF
    tiles_per_seq = SEQ_LEN // tm
    final = g_final is not None
    in_specs = [
        pl.BlockSpec((tm, D_MODEL), lambda i, j: (i, 0)),
        pl.BlockSpec((tm, D_MODEL), lambda i, j: (i, 0)),
        pl.BlockSpec((1, 6, D_MODEL), lambda i, j: (i // tiles_per_seq, 0, 0)),
        pl.BlockSpec((D_MODEL, tf), lambda i, j: (0, j)),
        pl.BlockSpec((tf, D_MODEL), lambda i, j: (j, 0)),
    ]
    operands = [x2, h2, mod_l, w1, w2]
    if final:
        in_specs.append(pl.BlockSpec((1, D_MODEL), lambda i, j: (0, 0)))
        operands.append(g_final.reshape(1, D_MODEL))
    return pl.pallas_call(
        functools.partial(_ffn_body, final=final),
        grid=(N_TOK // tm, FF_DIM // tf),
        in_specs=in_specs,
        out_specs=pl.BlockSpec((tm, D_MODEL), lambda i, j: (i, 0)),
        out_shape=jax.ShapeDtypeStruct((N_TOK, D_MODEL), F32),
        compiler_params=_params(("arbitrary", "arbitrary")),
        name="ffn",
    )(*operands)


def _swap_halves(w):
    half = w.shape[-1] // 2
    return jnp.concatenate([w[..., half:], w[..., :half]], axis=-1)


def _pack_w_in(w):
    wt = jnp.swapaxes(w, 1, 2).astype(BF16)
    kr = wt[:, R_KR:R_ATT]
    kr_sw = jnp.concatenate([kr[:, MLA_ROPE_D // 2:], kr[:, :MLA_ROPE_D // 2]], axis=1)
    wt_kr = jnp.concatenate([kr, kr, kr_sw, kr_sw], axis=1)
    wt_f = jnp.pad(wt[:, R_F:], ((0, 0), (0, LANES - FOX_H), (0, 0)))
    return wt, wt_kr, wt_f


def _pack_w_uq(w):
    w4 = w.astype(BF16).reshape(N_LAYERS, MLA_QL, MLA_H, MLA_NOPE_D + MLA_ROPE_D)
    flat = lambda a: a.reshape(N_LAYERS, MLA_QL, -1)
    rope = w4[..., MLA_NOPE_D:]
    return jnp.concatenate([flat(w4[..., :MLA_NOPE_D]), flat(rope), flat(_swap_halves(rope))], axis=-1)


def _pack_w_ukv(w):
    w4 = w.astype(BF16).reshape(N_LAYERS, MLA_KVL, MLA_H, MLA_NOPE_D + MLA_V_D)
    flat = lambda a: a.reshape(N_LAYERS, MLA_KVL, -1)
    return jnp.concatenate([flat(w4[..., :MLA_NOPE_D]), flat(w4[..., MLA_NOPE_D:])], axis=-1)


def _rope_tables():
    pos = jnp.arange(SEQ_LEN, dtype=F32)
    inv = ROPE_THETA ** (-jnp.arange(0, MLA_ROPE_D, 2, dtype=F32) / MLA_ROPE_D)
    ang = pos[:, None] * inv[None, :]
    cos, sin = jnp.cos(ang), jnp.sin(ang)
    cos_t = jnp.tile(jnp.concatenate([cos, cos], axis=1), (1, MLA_H))
    sin_t = jnp.tile(jnp.concatenate([-sin, sin], axis=1), (1, MLA_H))
    return cos_t, sin_t


def kernel(x, c, w_ada, b_ada, norm_mix, w_in, q_norm, w_uq, kv_norm, w_ukv, b_forget, out_norm,
           w_out, norm_ffn, w_ff1, w_ff2, final_norm):
    assert x.shape == (N_BATCH, SEQ_LEN, D_MODEL) and c.shape == (N_BATCH, D_MODEL)
    cos_t, sin_t = _rope_tables()
    c_pad = jnp.pad(c, ((0, SUBLANES - N_BATCH), (0, 0)))
    b_ada3 = b_ada.reshape(N_LAYERS, 1, 6 * D_MODEL)
    per_batch = lambda m: m[:N_BATCH].reshape(N_BATCH, 6, D_MODEL)
    mod_l = per_batch(_ada_mod(c_pad, w_ada, b_ada3, 0))

    rows = lambda v: v.reshape(N_LAYERS, 1, -1)
    assert w_in.shape == (N_LAYERS, D_MODEL, IN_W)
    wt_in, wt_kr, wt_f = _pack_w_in(w_in)
    wuq_p, wukv_p = _pack_w_uq(w_uq), _pack_w_ukv(w_ukv)
    bf_pad = jnp.pad(b_forget, ((0, 0), (0, LANES - FOX_H))).reshape(N_LAYERS, 1, LANES)
    out_gain = rows(out_norm)

    x2 = x.reshape(N_TOK, D_MODEL)
    for l in range(N_LAYERS):
        q_mla, kv_mla, krope, p_att, logf = _project(
            x2, mod_l, l, rows(norm_mix), wt_in, wt_kr, wt_f, rows(q_norm), wuq_p, rows(kv_norm), wukv_p, bf_pad,
            cos_t, sin_t)
        qaug, kaug = _fgate(logf)
        o_mla, wff1_b = _mla_attention(q_mla, kv_mla, krope, out_gain, l, w_ff1)
        if l + 1 < N_LAYERS:
            o_sb, wout_b, mod_next = _sb_attention(p_att, out_gain, l, w_out, (c_pad, w_ada, b_ada3))
        else:
            (o_sb, wout_b), mod_next = _sb_attention(p_att, out_gain, l, w_out), None
        o_fx, wff2_b = _fox_attention(p_att, qaug, kaug, out_gain, l, w_ff2)
        x2, h_ffn = _out_project(x2, mod_l, l, o_mla, o_sb, o_fx, wout_b, rows(norm_ffn))
        x2 = _ffn(x2, h_ffn, mod_l, wff1_b, wff2_b, final_norm if l == N_LAYERS - 1 else None)
        mod_l = per_batch(mod_next) if mod_next is not None else None
    return x2.reshape(N_BATCH, SEQ_LEN, D_MODEL)
```

```python
import functools

import numpy as np
import jax
import jax.numpy as jnp
from jax import lax
from jax.experimental import pallas as pl
from jax.experimental.pallas import tpu as pltpu

F32 = jnp.float32
BF16 = jnp.bfloat16

D_MODEL = 2048
N_BATCH = 4
SEQ_LEN = 2048
N_LAYERS = 2
CHUNK_LEN = 64
CHUNK_SHIFT = CHUNK_LEN.bit_length() - 1
assert 1 << CHUNK_SHIFT == CHUNK_LEN
NORM_EPS = 1e-6
LOG2_E = 1.4426950408889634
MLA_H = 8
MLA_QL = 512
MLA_KVL = 256
MLA_NOPE_D = 128
MLA_ROPE_D = 64
MLA_V_D = 128
ROPE_THETA = 10000.0
SB_H = 8
SB_D = 64
FOX_H = 8
FOX_D = 64
FF_DIM = 4 * D_MODEL
N_TOK = N_BATCH * SEQ_LEN

LANES = 128
SUBLANES = 8
VMEM_LIMIT = 56 * 1024 * 1024

R_CQ = 0
R_CKV = R_CQ + MLA_QL
R_KR = R_CKV + MLA_KVL
R_ATT = R_KR + MLA_ROPE_D
ATT_W = 3 * SB_H * SB_D + 3 * FOX_H * FOX_D
R_F = R_ATT + ATT_W
IN_W = R_F + FOX_H
HEAD_GRP = SB_H * SB_D

PROJ_TM = 512
FFN_TM = 1024
FFN_TF = 512
ADA_TN = 1024
ATT_PAIRS = 4
ATT_T = 256
N_QB = SEQ_LEN // ATT_T
QB_ORDER = tuple(reversed(range(N_QB)))
CUM_T = 256


def _dot(a, b):
    return jnp.dot(a, b, preferred_element_type=F32)


def _dot_nt(a, b):
    return lax.dot_general(a, b, (((1,), (1,)), ((), ())), preferred_element_type=F32)


def _split3(x):
    hi = x.astype(BF16)
    r1 = x - hi.astype(F32)
    mid = r1.astype(BF16)
    lo = (r1 - mid.astype(F32)).astype(BF16)
    return hi, mid, lo


def _softplus(z):
    return jnp.maximum(z, 0.0) + jnp.log(1.0 + jnp.exp(-jnp.abs(z)))


def _rms(x, gain):
    return x * lax.rsqrt(jnp.mean(x * x, axis=-1, keepdims=True) + NORM_EPS) * gain


def _inv_rms(x_ref):
    x = x_ref[...]
    return lax.rsqrt(jnp.mean(x * x, axis=-1, keepdims=True) + NORM_EPS)


def _modulated_norm(x_ref, norm_gain, mod_scale, mod_shift, h_ref):
    inv = _inv_rms(x_ref)
    gain = norm_gain * (1.0 + mod_scale)
    h_ref[...] = (x_ref[...] * inv * gain + mod_shift).astype(BF16)


def _resident(shape):
    nd = len(shape)
    return pl.BlockSpec(shape, lambda *_: (0,) * nd, pipeline_mode=pl.Buffered(1))


def _layer_resident(shape, layer):
    nd = len(shape)
    return pl.BlockSpec((1,) + shape, lambda *_: (layer,) + (0,) * nd, pipeline_mode=pl.Buffered(1))


def _params(sem):
    return pltpu.CompilerParams(dimension_semantics=sem, vmem_limit_bytes=VMEM_LIMIT)


def _ada_columns(c_ref, w_ref, b_ref, o_ref):
    c = c_ref[...]
    c_act = (c * jax.nn.sigmoid(c)).astype(BF16)
    o_ref[...] = _dot(c_act, w_ref[0].astype(BF16)) + b_ref[0]


def _ada_mod(c_pad, w_ada, b_ada3, layer):
    n_out = 6 * D_MODEL
    return pl.pallas_call(
        _ada_columns,
        grid=(n_out // ADA_TN,),
        in_specs=[
            pl.BlockSpec((SUBLANES, D_MODEL), lambda j: (0, 0)),
            pl.BlockSpec((1, D_MODEL, ADA_TN), lambda j: (layer, 0, j)),
            pl.BlockSpec((1, 1, ADA_TN), lambda j: (layer, 0, j)),
        ],
        out_specs=pl.BlockSpec((SUBLANES, ADA_TN), lambda j: (0, j)),
        out_shape=jax.ShapeDtypeStruct((SUBLANES, n_out), F32),
        compiler_params=_params(("arbitrary",)),
        name="ada_mod",
    )(c_pad, w_ada, b_ada3)


def _ada_along(layer):
    n_out = 6 * D_MODEL
    slab = n_out // (N_BATCH * ATT_PAIRS)
    assert slab * N_BATCH * ATT_PAIRS == n_out and slab % LANES == 0
    step = lambda b, p: b * ATT_PAIRS + p
    in_specs = [
        pl.BlockSpec((SUBLANES, D_MODEL), lambda b, p: (0, 0)),
        pl.BlockSpec((1, D_MODEL, slab), lambda b, p: (layer, 0, step(b, p))),
        pl.BlockSpec((1, 1, slab), lambda b, p: (layer, 0, step(b, p))),
    ]
    out_spec = pl.BlockSpec((SUBLANES, slab), lambda b, p: (0, step(b, p)))
    return in_specs, out_spec, jax.ShapeDtypeStruct((SUBLANES, n_out), F32)


def _proj_body(x_ref, mod_ref, gn_ref, wt_ref, wkr_ref, wf_ref, qn_ref, wuq_ref, kvn_ref, wukv_ref, bf_ref,
               cos_ref, sin_ref, q_out, kv_out, kr_out, att_out, lf_out, h_ref):
    mod = mod_ref[0]
    _modulated_norm(x_ref, gn_ref[0], mod[1:2], mod[0:1], h_ref)

    def proj(r0, width):
        return _dot_nt(h_ref[...], wt_ref[0, r0:r0 + width, :])

    mla_scale = (MLA_NOPE_D + MLA_ROPE_D) ** -0.5 * LOG2_E
    n_nope = MLA_H * MLA_NOPE_D
    n_rope = MLA_H * MLA_ROPE_D

    cq = _rms(proj(R_CQ, MLA_QL), qn_ref[0]).astype(BF16)
    q = _dot(cq, wuq_ref[0])
    q_out[:, :n_nope] = (q[:, :n_nope] * mla_scale).astype(BF16)
    q_rot = q[:, n_nope:n_nope + n_rope] * cos_ref[...] + q[:, n_nope + n_rope:] * sin_ref[...]
    q_out[:, n_nope:] = (q_rot * mla_scale).astype(BF16)

    ckv = _rms(proj(R_CKV, MLA_KVL), kvn_ref[0]).astype(BF16)
    kv_out[...] = _dot(ckv, wukv_ref[0]).astype(BF16)

    kr = _dot_nt(h_ref[...], wkr_ref[0])
    kr_out[...] = (kr[:, :LANES] * cos_ref[:, :LANES] + kr[:, LANES:] * sin_ref[:, :LANES]).astype(BF16)

    lf_out[...] = -_softplus(-(_dot_nt(h_ref[...], wf_ref[0]) + bf_ref[0]))

    sb_scale = SB_D ** -0.5 * LOG2_E
    fox_scale = FOX_D ** -0.5 * LOG2_E
    scales = (sb_scale, 1.0, 1.0, fox_scale, 1.0, 1.0)
    for g, sc in enumerate(scales):
        blk = proj(R_ATT + g * HEAD_GRP, HEAD_GRP)
        if sc != 1.0:
            blk = blk * sc
        att_out[:, g * HEAD_GRP:(g + 1) * HEAD_GRP] = blk.astype(BF16)


def _project(x2, mod_l, layer, gn, wt_in, wt_kr, wt_f, qn, wuq, kvn, wukv, bfp, cos_t, sin_t):
    tm = PROJ_TM
    tiles_per_seq = SEQ_LEN // tm
    q_w = MLA_H * (MLA_NOPE_D + MLA_ROPE_D)
    kv_w = MLA_H * (MLA_NOPE_D + MLA_V_D)
    rope_w = MLA_H * MLA_ROPE_D
    row = lambda i: (i, 0)
    return pl.pallas_call(
        _proj_body,
        grid=(N_TOK // tm,),
        in_specs=[
            pl.BlockSpec((tm, D_MODEL), row),
            pl.BlockSpec((1, 6, D_MODEL), lambda i: (i // tiles_per_seq, 0, 0)),
            _layer_resident((1, D_MODEL), layer),
            _layer_resident((IN_W, D_MODEL), layer),
            _layer_resident((2 * LANES, D_MODEL), layer),
            _layer_resident((LANES, D_MODEL), layer),
            _layer_resident((1, MLA_QL), layer),
            _layer_resident((MLA_QL, q_w + rope_w), layer),
            _layer_resident((1, MLA_KVL), layer),
            _layer_resident((MLA_KVL, kv_w), layer),
            _layer_resident((1, LANES), layer),
            pl.BlockSpec((tm, rope_w), lambda i: (i % tiles_per_seq, 0)),
            pl.BlockSpec((tm, rope_w), lambda i: (i % tiles_per_seq, 0)),
        ],
        out_specs=[
            pl.BlockSpec((tm, q_w), row),
            pl.BlockSpec((tm, kv_w), row),
            pl.BlockSpec((tm, LANES), row),
            pl.BlockSpec((tm, ATT_W), row),
            pl.BlockSpec((tm, LANES), row),
        ],
        out_shape=[
            jax.ShapeDtypeStruct((N_TOK, q_w), BF16),
            jax.ShapeDtypeStruct((N_TOK, kv_w), BF16),
            jax.ShapeDtypeStruct((N_TOK, LANES), BF16),
            jax.ShapeDtypeStruct((N_TOK, ATT_W), BF16),
            jax.ShapeDtypeStruct((N_TOK, LANES), F32),
        ],
        scratch_shapes=[pltpu.VMEM((tm, D_MODEL), BF16)],
        compiler_params=_params(("arbitrary",)),
        name="project",
    )(x2, mod_l, gn, wt_in, wt_kr, wt_f, qn, wuq, kvn, wukv, bfp, cos_t, sin_t)


def _fgate_body(lf_ref, tri_ref, pq_ref, pk_ref, cq_ref, ck_ref, qa_out, ka_out):
    n_blk = SEQ_LEN // CUM_T
    local = []
    for blk in range(n_blk):
        hi, mid, lo = _split3(lf_ref[blk * CUM_T:(blk + 1) * CUM_T, :] * LOG2_E)
        tri = tri_ref[...]
        local.append(_dot(tri, hi) + _dot(tri, mid) + _dot(tri, lo))
    carry = jnp.zeros((1, LANES), F32)
    for blk in range(n_blk):
        rows = slice(blk * CUM_T, (blk + 1) * CUM_T)
        f_blk = local[blk] + carry
        carry = f_blk[CUM_T - 1:CUM_T, :]
        f_hi, f_mid, f_lo = _split3(f_blk)
        qa = _dot(f_hi, pq_ref[0]) + _dot(f_mid, pq_ref[1]) + _dot(f_lo, pq_ref[2]) + cq_ref[...]
        ka = ck_ref[...] - (_dot(f_hi, pk_ref[0]) + _dot(f_mid, pk_ref[1]) + _dot(f_lo, pk_ref[2]))
        qa_out[rows, :] = qa.astype(BF16)
        ka_out[rows, :] = ka.astype(BF16)


def _fgate_constants():
    tri = np.tril(np.ones((CUM_T, CUM_T), np.float32))
    n_pair = FOX_H // 2
    n_terms = 3
    assert 2 * n_terms <= SUBLANES
    pq = np.zeros((n_terms, LANES, n_pair * LANES), np.float32)
    pk = np.zeros((n_terms, LANES, n_pair * LANES), np.float32)
    cq = np.zeros((1, n_pair * LANES), np.float32)
    ck = np.zeros((1, n_pair * LANES), np.float32)
    for h in range(FOX_H):
        base = (h // 2) * LANES + (h % 2) * SUBLANES
        for t in range(n_terms):
            pq[t, h, base + t] = 1.0
            pk[t, h, base + n_terms + t] = 1.0
            cq[0, base + n_terms + t] = 1.0
            ck[0, base + t] = 1.0
    as_bf16 = lambda a: jnp.asarray(a, BF16)
    return as_bf16(tri), as_bf16(pq), as_bf16(pk), jnp.asarray(cq), jnp.asarray(ck)


def _fgate(logf):
    tri, pq, pk, cq, ck = _fgate_constants()
    aug_w = (FOX_H // 2) * LANES
    seq_blk = lambda b: (b, 0)
    return pl.pallas_call(
        _fgate_body,
        grid=(N_BATCH,),
        in_specs=[
            pl.BlockSpec((SEQ_LEN, LANES), seq_blk),
            _resident((CUM_T, CUM_T)),
            _resident((3, LANES, aug_w)),
            _resident((3, LANES, aug_w)),
            _resident((1, aug_w)),
            _resident((1, aug_w)),
        ],
        out_specs=[pl.BlockSpec((SEQ_LEN, aug_w), seq_blk), pl.BlockSpec((SEQ_LEN, aug_w), seq_blk)],
        out_shape=[jax.ShapeDtypeStruct((N_TOK, aug_w), BF16), jax.ShapeDtypeStruct((N_TOK, aug_w), BF16)],
        compiler_params=_params(("arbitrary",)),
        name="fgate",
    )(logf, tri, pq, pk, cq, ck)


def _lane_band(shape, lo, width):
    lane = lax.broadcasted_iota(jnp.int32, shape, len(shape) - 1)
    return (lane >= lo) & (lane < lo + width)


def _rel_positions():
    rows = lax.broadcasted_iota(jnp.int32, (ATT_T, ATT_T), 0)
    cols = lax.broadcasted_iota(jnp.int32, (ATT_T, ATT_T), 1)
    return rows, cols


def _col_block(a, kb):
    return a[:, kb * ATT_T:(kb + 1) * ATT_T]


def _weights_times_v(p_ref, hh, nkb, v):
    p = jnp.concatenate([p_ref[hh, kb] for kb in range(nkb)], axis=1) if nkb > 1 else p_ref[hh, 0]
    return _dot(p, v)


def _softmax_attention(qcat_at, kcat_at, vaug_at, mask_fn, finish, s_ref, p_ref):
    def scores(i):
        nkb = i + 1
        row_max = []
        for hh in range(2):
            s = _dot_nt(qcat_at(i, hh), kcat_at(hh, nkb * ATT_T))
            blocks = [_col_block(s, kb) for kb in range(nkb)]
            blocks[-1] = jnp.where(mask_fn(*_rel_positions()), blocks[-1], -jnp.inf)
            m = None
            for kb in range(nkb):
                s_ref[i % 2, hh, kb] = blocks[kb]
                bm = jnp.maximum(blocks[kb][:, :LANES], blocks[kb][:, LANES:])
                m = bm if m is None else jnp.maximum(m, bm)
            row_max.append(jnp.max(m, axis=-1, keepdims=True))
        return row_max

    row_max = scores(QB_ORDER[0])
    for n, i in enumerate(QB_ORDER):
        nkb = i + 1
        next_max = scores(QB_ORDER[n + 1]) if n + 1 < N_QB else None
        for kb in range(nkb):
            for hh in range(2):
                p_ref[i % 2, hh, kb] = jnp.exp2(s_ref[i % 2, hh, kb] - row_max[hh]).astype(BF16)
        finish(i, [_weights_times_v(p_ref.at[i % 2], hh, nkb, vaug_at(hh, nkb * ATT_T)) for hh in range(2)])
        row_max = next_max


def _pair_norm(o, gain, head_d):
    first = _lane_band(o.shape, 0, head_d)
    sq = o * o
    ss_a = jnp.sum(jnp.where(first, sq, 0.0), axis=-1, keepdims=True)
    ss_b = jnp.sum(sq, axis=-1, keepdims=True) - ss_a
    inv = jnp.where(first, lax.rsqrt(ss_a / head_d + NORM_EPS), lax.rsqrt(ss_b / head_d + NORM_EPS))
    return o * inv * gain


_SCORE_SCRATCH = [
    pltpu.VMEM((2, 2, N_QB, ATT_T, ATT_T), F32),
    pltpu.VMEM((2, 2, N_QB, ATT_T, ATT_T), BF16),
]
def _cast_along(w, layer):
    _, n_rows, n_cols = w.shape
    slab = n_rows // (N_BATCH * ATT_PAIRS)
    assert slab * N_BATCH * ATT_PAIRS == n_rows and slab % (2 * SUBLANES) == 0
    in_spec = pl.BlockSpec((1, slab, n_cols), lambda b, p: (layer, b * ATT_PAIRS + p, 0))
    out_spec = pl.BlockSpec((slab, n_cols), lambda b, p: (b * ATT_PAIRS + p, 0))
    return in_spec, out_spec, jax.ShapeDtypeStruct((n_rows, n_cols), BF16)


def _mla_body(qn_ref, qr_ref, kn_ref, v_ref, kr_ref, g_ref, w_ref, o_ref, wb_ref, kcat_ref, vaug_ref,
              s_ref, p_ref):
    wb_ref[...] = w_ref[0].astype(BF16)
    d = MLA_NOPE_D
    for hh in range(2):
        kcat_ref[hh, :, :d] = kn_ref[:, hh * d:(hh + 1) * d]
        kcat_ref[hh, :, d:] = kr_ref[...]
        vaug_ref[hh, :, :d] = v_ref[:, hh * d:(hh + 1) * d]
        vaug_ref[hh, :, d:] = jnp.ones((SEQ_LEN, LANES), BF16)

    def chunk_mask(rows, cols):
        return cols < ((rows >> CHUNK_SHIFT) + 1) * CHUNK_LEN

    def qcat_at(i, hh):
        r = slice(i * ATT_T, (i + 1) * ATT_T)
        qr = qr_ref[r, :]
        qr = jnp.where(_lane_band(qr.shape, hh * MLA_ROPE_D, MLA_ROPE_D), qr, jnp.zeros_like(qr))
        return jnp.concatenate([qn_ref[r, hh * d:(hh + 1) * d], qr], axis=1)

    def finish(i, outs):
        r = slice(i * ATT_T, (i + 1) * ATT_T)
        for hh in range(2):
            o = outs[hh][:, :d] / outs[hh][:, d:]
            o_ref[r, hh * d:(hh + 1) * d] = _rms(o, g_ref[0, :, hh * d:(hh + 1) * d]).astype(BF16)

    _softmax_attention(
        qcat_at, lambda hh, klen: kcat_ref[hh, :klen, :], lambda hh, klen: vaug_ref[hh, :klen, :],
        chunk_mask, finish, s_ref, p_ref)


def _mla_attention(q_mla, kv_mla, krope, out_gain, layer, w_cast):
    n_pair = MLA_H // 2
    assert n_pair == ATT_PAIRS
    pair_w = 2 * MLA_NOPE_D
    w_in_spec, w_out_spec, w_out_shape = _cast_along(w_cast, layer)
    return pl.pallas_call(
        _mla_body,
        grid=(N_BATCH, n_pair),
        in_specs=[
            pl.BlockSpec((SEQ_LEN, pair_w), lambda b, p: (b, p)),
            pl.BlockSpec((SEQ_LEN, LANES), lambda b, p: (b, MLA_H * MLA_NOPE_D // LANES + p)),
            pl.BlockSpec((SEQ_LEN, pair_w), lambda b, p: (b, p)),
            pl.BlockSpec((SEQ_LEN, pair_w), lambda b, p: (b, n_pair + p)),
            pl.BlockSpec((SEQ_LEN, LANES), lambda b, p: (b, 0)),
            pl.BlockSpec((1, 1, pair_w), lambda b, p: (layer, 0, p)),
            w_in_spec,
        ],
        out_specs=[pl.BlockSpec((SEQ_LEN, pair_w), lambda b, p: (b, p)), w_out_spec],
        out_shape=[jax.ShapeDtypeStruct((N_TOK, MLA_H * MLA_V_D), BF16), w_out_shape],
        scratch_shapes=[
            pltpu.VMEM((2, SEQ_LEN, 2 * LANES), BF16),
            pltpu.VMEM((2, SEQ_LEN, 2 * LANES), BF16),
        ] + _SCORE_SCRATCH,
        compiler_params=_params(("arbitrary", "arbitrary")),
        name="mla_attention",
    )(q_mla, q_mla, kv_mla, kv_mla, krope, out_gain, w_cast)


def _fox_body(q_ref, k_ref, v_ref, qa_ref, ka_ref, g_ref, w_ref, o_ref, wb_ref, kcat_ref, vaug_ref,
              s_ref, p_ref):
    wb_ref[...] = w_ref[0].astype(BF16)
    kcat_ref[:, :LANES] = k_ref[...]
    kcat_ref[:, LANES:] = ka_ref[...]
    vaug_ref[:, :LANES] = v_ref[...]
    vaug_ref[:, LANES:] = jnp.ones((SEQ_LEN, LANES), BF16)

    def causal(rows, cols):
        return cols <= rows

    def qcat_at(i, hh):
        r = slice(i * ATT_T, (i + 1) * ATT_T)
        q = q_ref[r, :]
        qa = qa_ref[r, :]
        zero = jnp.zeros_like(q)
        return jnp.concatenate([
            jnp.where(_lane_band(q.shape, hh * FOX_D, FOX_D), q, zero),
            jnp.where(_lane_band(qa.shape, hh * SUBLANES, SUBLANES), qa, zero)], axis=1)

    def finish(i, outs):
        heads = [outs[hh][:, :LANES] / outs[hh][:, LANES:] for hh in range(2)]
        o = jnp.where(_lane_band(heads[0].shape, 0, FOX_D), heads[0], heads[1])
        o_ref[i * ATT_T:(i + 1) * ATT_T, :] = _pair_norm(o, g_ref[0], FOX_D).astype(BF16)

    _softmax_attention(
        qcat_at, lambda hh, klen: kcat_ref[:klen, :], lambda hh, klen: vaug_ref[:klen, :],
        causal, finish, s_ref, p_ref)


def _fox_attention(p_att, qaug, kaug, out_gain, layer, w_cast):
    n_pair = FOX_H // 2
    assert n_pair == ATT_PAIRS
    grp = HEAD_GRP // LANES
    gain_blk = (MLA_H * MLA_V_D + SB_H * SB_D) // LANES
    w_in_spec, w_out_spec, w_out_shape = _cast_along(w_cast, layer)
    return pl.pallas_call(
        _fox_body,
        grid=(N_BATCH, n_pair),
        in_specs=[
            pl.BlockSpec((SEQ_LEN, LANES), lambda b, p: (b, 3 * grp + p)),
            pl.BlockSpec((SEQ_LEN, LANES), lambda b, p: (b, 4 * grp + p)),
            pl.BlockSpec((SEQ_LEN, LANES), lambda b, p: (b, 5 * grp + p)),
            pl.BlockSpec((SEQ_LEN, LANES), lambda b, p: (b, p)),
            pl.BlockSpec((SEQ_LEN, LANES), lambda b, p: (b, p)),
            pl.BlockSpec((1, 1, LANES), lambda b, p: (layer, 0, gain_blk + p)),
            w_in_spec,
        ],
        out_specs=[pl.BlockSpec((SEQ_LEN, LANES), lambda b, p: (b, p)), w_out_spec],
        out_shape=[jax.ShapeDtypeStruct((N_TOK, FOX_H * FOX_D), BF16), w_out_shape],
        scratch_shapes=[
            pltpu.VMEM((SEQ_LEN, 2 * LANES), BF16),
            pltpu.VMEM((SEQ_LEN, 2 * LANES), BF16),
        ] + _SCORE_SCRATCH,
        compiler_params=_params(("arbitrary", "arbitrary")),
        name="fox_attention",
    )(p_att, p_att, p_att, qaug, kaug, out_gain, w_cast)


def _sb_body(*refs, with_ada):
    if with_ada:
        (q_ref, k_ref, v_ref, tri_ref, g_ref, w_ref, c_ref, wa_ref, ba_ref,
         o_ref, wb_ref, mod_ref, u_ref, p_ref) = refs
        _ada_columns(c_ref, wa_ref, ba_ref, mod_ref)
    else:
        q_ref, k_ref, v_ref, tri_ref, g_ref, w_ref, o_ref, wb_ref, u_ref, p_ref = refs
    wb_ref[...] = w_ref[0].astype(BF16)

    def score_tiles(i):
        q = q_ref[i * ATT_T:(i + 1) * ATT_T, :]
        zero = jnp.zeros_like(q)
        q_heads = [jnp.where(_lane_band(q.shape, hh * SB_D, SB_D), q, zero) for hh in range(2)]

        def tile(hh, kb):
            u_ref[i % 2, hh, kb] = _dot_nt(q_heads[hh], k_ref[kb * ATT_T:(kb + 1) * ATT_T, :])

        return [functools.partial(tile, hh, kb) for kb in range(i + 1) for hh in range(2)]

    def weights(i, pending):
        nkb = i + 1
        right = [None, None]
        for kb in reversed(range(nkb)):
            on_diagonal = kb == nkb - 1
            for hh in range(2):
                if pending:
                    pending.pop()()
                u = u_ref[i % 2, hh, kb]
                drop = jnp.maximum(u, 0.0) + jnp.log2(1.0 + jnp.exp2(-jnp.abs(u)))
                if on_diagonal:
                    rows, cols = _rel_positions()
                    strict = cols < rows
                    drop = jnp.where(strict, drop, 0.0)
                log_w = (u - drop) + _dot(drop.astype(BF16), tri_ref[...])
                if right[hh] is not None:
                    log_w = log_w - right[hh]
                w = jnp.exp2(log_w)
                if on_diagonal:
                    w = jnp.where(strict, w, 0.0)
                p_ref[i % 2, hh, kb] = w.astype(BF16)
                total = jnp.sum(drop, axis=-1, keepdims=True)
                right[hh] = total if right[hh] is None else right[hh] + total

    for tile in score_tiles(QB_ORDER[0]):
        tile()
    for n, i in enumerate(QB_ORDER):
        nkb = i + 1
        pending = score_tiles(QB_ORDER[n + 1]) if n + 1 < N_QB else []
        weights(i, pending)
        while pending:
            pending.pop()()
        heads = [_weights_times_v(p_ref.at[i % 2], hh, nkb, v_ref[:nkb * ATT_T, :]) for hh in range(2)]
        o = jnp.where(_lane_band(heads[0].shape, 0, SB_D), heads[0], heads[1])
        o_ref[i * ATT_T:(i + 1) * ATT_T, :] = _pair_norm(o, g_ref[0], SB_D).astype(BF16)


def _sb_attention(p_att, out_gain, layer, w_cast, ada_next=None):
    n_pair = SB_H // 2
    assert n_pair == ATT_PAIRS
    grp = HEAD_GRP // LANES
    gain_blk = MLA_H * MLA_V_D // LANES
    w_in_spec, w_out_spec, w_out_shape = _cast_along(w_cast, layer)
    tri = jnp.asarray(-np.tril(np.ones((ATT_T, ATT_T), np.float32), -1), BF16)
    in_specs = [
        pl.BlockSpec((SEQ_LEN, LANES), lambda b, p: (b, p)),
        pl.BlockSpec((SEQ_LEN, LANES), lambda b, p: (b, grp + p)),
        pl.BlockSpec((SEQ_LEN, LANES), lambda b, p: (b, 2 * grp + p)),
        _resident((ATT_T, ATT_T)),
        pl.BlockSpec((1, 1, LANES), lambda b, p: (layer, 0, gain_blk + p)),
        w_in_spec,
    ]
    out_specs = [pl.BlockSpec((SEQ_LEN, LANES), lambda b, p: (b, p)), w_out_spec]
    out_shape = [jax.ShapeDtypeStruct((N_TOK, SB_H * SB_D), BF16), w_out_shape]
    operands = [p_att, p_att, p_att, tri, out_gain, w_cast]
    if ada_next is not None:
        ada_in, ada_out_spec, ada_out_shape = _ada_along(layer + 1)
        in_specs += ada_in
        out_specs.append(ada_out_spec)
        out_shape.append(ada_out_shape)
        operands += list(ada_next)
    return pl.pallas_call(
        functools.partial(_sb_body, with_ada=ada_next is not None),
        grid=(N_BATCH, n_pair),
        in_specs=in_specs,
        out_specs=out_specs,
        out_shape=out_shape,
        scratch_shapes=_SCORE_SCRATCH,
        compiler_params=_params(("arbitrary", "arbitrary")),
        name="sb_attention",
    )(*operands)


def _outproj_body(x_ref, mod_ref, om_ref, os_ref, of_ref, w_ref, gn_ref, o_ref, h_ref):
    n_mla = MLA_H * MLA_V_D
    n_sb = SB_H * SB_D
    mod = mod_ref[0]
    gain = gn_ref[0] * (1.0 + mod[4:5])
    half = x_ref.shape[0] // 2
    for part in range(2):
        rows = slice(part * half, (part + 1) * half)
        y = (_dot(om_ref[rows, :], w_ref[:n_mla, :]) + _dot(os_ref[rows, :], w_ref[n_mla:n_mla + n_sb, :])
             + _dot(of_ref[rows, :], w_ref[n_mla + n_sb:, :]))
        y = x_ref[rows, :] + mod[2:3] * y
        o_ref[rows, :] = y
        h_ref[rows, :] = (_rms(y, gain) + mod[3:4]).astype(BF16)


def _out_project(x2, mod_l, layer, o_mla, o_sb, o_fx, w_out, gn_ffn):
    tm = PROJ_TM
    tiles_per_seq = SEQ_LEN // tm
    row = lambda i: (i, 0)
    return pl.pallas_call(
        _outproj_body,
        grid=(N_TOK // tm,),
        in_specs=[
            pl.BlockSpec((tm, D_MODEL), row),
            pl.BlockSpec((1, 6, D_MODEL), lambda i: (i // tiles_per_seq, 0, 0)),
            pl.BlockSpec((tm, MLA_H * MLA_V_D), row),
            pl.BlockSpec((tm, SB_H * SB_D), row),
            pl.BlockSpec((tm, FOX_H * FOX_D), row),
            _resident((D_MODEL, D_MODEL)),
            _layer_resident((1, D_MODEL), layer),
        ],
        out_specs=[pl.BlockSpec((tm, D_MODEL), row), pl.BlockSpec((tm, D_MODEL), row)],
        out_shape=[jax.ShapeDtypeStruct((N_TOK, D_MODEL), F32), jax.ShapeDtypeStruct((N_TOK, D_MODEL), BF16)],
        compiler_params=_params(("arbitrary",)),
        name="out_project",
    )(x2, mod_l, o_mla, o_sb, o_fx, w_out, gn_ffn)


def _ffn_body(x_hbm, h_hbm, mod_ref, w1_ref, w2_ref, *rest, final):
    gf_ref, o_ref, x_buf, h_buf, sem = rest if final else (None,) + rest
    i, j = pl.program_id(0), pl.program_id(1)
    tm = o_ref.shape[0]
    slot = i % 2
    mod = mod_ref[0]

    def tile_copies(tile, to_slot):
        rows = pl.ds(pl.multiple_of(tile * tm, tm), tm)
        return (pltpu.make_async_copy(x_hbm.at[rows, :], x_buf.at[to_slot], sem.at[0, to_slot]),
                pltpu.make_async_copy(h_hbm.at[rows, :], h_buf.at[to_slot], sem.at[1, to_slot]))

    @pl.when(j == 0)
    def _():
        @pl.when(i == 0)
        def _():
            for copy in tile_copies(0, 0):
                copy.start()

        @pl.when(i + 1 < pl.num_programs(0))
        def _():
            for copy in tile_copies(i + 1, 1 - slot):
                copy.start()

        for copy in tile_copies(i, slot):
            copy.wait()
        o_ref[...] = x_buf[slot]

    a = jnp.maximum(_dot(h_buf[slot], w1_ref[...]), 0.0)
    o_ref[...] += mod[5:6] * _dot((a * a).astype(BF16), w2_ref[...])

    if final:
        @pl.when(j == pl.num_programs(1) - 1)
        def _():
            inv = _inv_rms(o_ref)
            o_ref[...] = o_ref[...] * inv * gf_ref[...]


def _ffn(x2, h2, mod_l, w1, w2, g_final=None):
    tm, tf = FFN_TM, FFN_TF
    tiles_per_seq = SEQ_LEN // tm
    final = g_final is not None
    in_specs = [
        pl.BlockSpec(memory_space=pl.ANY),
        pl.BlockSpec(memory_space=pl.ANY),
        pl.BlockSpec((1, 6, D_MODEL), lambda i, j: (i // tiles_per_seq, 0, 0)),
        pl.BlockSpec((D_MODEL, tf), lambda i, j: (0, j)),
        pl.BlockSpec((tf, D_MODEL), lambda i, j: (j, 0)),
    ]
    operands = [x2, h2, mod_l, w1, w2]
    if final:
        in_specs.append(pl.BlockSpec((1, D_MODEL), lambda i, j: (0, 0)))
        operands.append(g_final.reshape(1, D_MODEL))
    return pl.pallas_call(
        functools.partial(_ffn_body, final=final),
        grid=(N_TOK // tm, FF_DIM // tf),
        in_specs=in_specs,
        out_specs=pl.BlockSpec((tm, D_MODEL), lambda i, j: (i, 0)),
        out_shape=jax.ShapeDtypeStruct((N_TOK, D_MODEL), F32),
        scratch_shapes=[
            pltpu.VMEM((2, tm, D_MODEL), F32),
            pltpu.VMEM((2, tm, D_MODEL), BF16),
            pltpu.SemaphoreType.DMA((2, 2)),
        ],
        compiler_params=_params(("arbitrary", "arbitrary")),
        name="ffn",
    )(*operands)


def _swap_halves(w):
    half = w.shape[-1] // 2
    return jnp.concatenate([w[..., half:], w[..., :half]], axis=-1)


def _pack_w_in(w):
    wt = jnp.swapaxes(w, 1, 2).astype(BF16)
    kr = wt[:, R_KR:R_ATT]
    kr_sw = jnp.concatenate([kr[:, MLA_ROPE_D // 2:], kr[:, :MLA_ROPE_D // 2]], axis=1)
    wt_kr = jnp.concatenate([kr, kr, kr_sw, kr_sw], axis=1)
    wt_f = jnp.pad(wt[:, R_F:], ((0, 0), (0, LANES - FOX_H), (0, 0)))
    return wt, wt_kr, wt_f


def _pack_w_uq(w):
    w4 = w.astype(BF16).reshape(N_LAYERS, MLA_QL, MLA_H, MLA_NOPE_D + MLA_ROPE_D)
    flat = lambda a: a.reshape(N_LAYERS, MLA_QL, -1)
    rope = w4[..., MLA_NOPE_D:]
    return jnp.concatenate([flat(w4[..., :MLA_NOPE_D]), flat(rope), flat(_swap_halves(rope))], axis=-1)


def _pack_w_ukv(w):
    w4 = w.astype(BF16).reshape(N_LAYERS, MLA_KVL, MLA_H, MLA_NOPE_D + MLA_V_D)
    flat = lambda a: a.reshape(N_LAYERS, MLA_KVL, -1)
    return jnp.concatenate([flat(w4[..., :MLA_NOPE_D]), flat(w4[..., MLA_NOPE_D:])], axis=-1)


def _rope_tables():
    pos = jnp.arange(SEQ_LEN, dtype=F32)
    inv = ROPE_THETA ** (-jnp.arange(0, MLA_ROPE_D, 2, dtype=F32) / MLA_ROPE_D)
    ang = pos[:, None] * inv[None, :]
    cos, sin = jnp.cos(ang), jnp.sin(ang)
    cos_t = jnp.tile(jnp.concatenate([cos, cos], axis=1), (1, MLA_H))
    sin_t = jnp.tile(jnp.concatenate([-sin, sin], axis=1), (1, MLA_H))
    return cos_t, sin_t


def kernel(x, c, w_ada, b_ada, norm_mix, w_in, q_norm, w_uq, kv_norm, w_ukv, b_forget, out_norm,
           w_out, norm_ffn, w_ff1, w_ff2, final_norm):
    assert x.shape == (N_BATCH, SEQ_LEN, D_MODEL) and c.shape == (N_BATCH, D_MODEL)
    cos_t, sin_t = _rope_tables()
    c_pad = jnp.pad(c, ((0, SUBLANES - N_BATCH), (0, 0)))
    b_ada3 = b_ada.reshape(N_LAYERS, 1, 6 * D_MODEL)
    per_batch = lambda m: m[:N_BATCH].reshape(N_BATCH, 6, D_MODEL)
    mod_l = per_batch(_ada_mod(c_pad, w_ada, b_ada3, 0))

    rows = lambda v: v.reshape(N_LAYERS, 1, -1)
    assert w_in.shape == (N_LAYERS, D_MODEL, IN_W)
    wt_in, wt_kr, wt_f = _pack_w_in(w_in)
    wuq_p, wukv_p = _pack_w_uq(w_uq), _pack_w_ukv(w_ukv)
    bf_pad = jnp.pad(b_forget, ((0, 0), (0, LANES - FOX_H))).reshape(N_LAYERS, 1, LANES)
    out_gain = rows(out_norm)

    x2 = x.reshape(N_TOK, D_MODEL)
    for l in range(N_LAYERS):
        q_mla, kv_mla, krope, p_att, logf = _project(
            x2, mod_l, l, rows(norm_mix), wt_in, wt_kr, wt_f, rows(q_norm), wuq_p, rows(kv_norm), wukv_p, bf_pad,
            cos_t, sin_t)
        qaug, kaug = _fgate(logf)
        o_mla, wff1_b = _mla_attention(q_mla, kv_mla, krope, out_gain, l, w_ff1)
        if l + 1 < N_LAYERS:
            o_sb, wout_b, mod_next = _sb_attention(p_att, out_gain, l, w_out, (c_pad, w_ada, b_ada3))
        else:
            (o_sb, wout_b), mod_next = _sb_attention(p_att, out_gain, l, w_out), None
        o_fx, wff2_b = _fox_attention(p_att, qaug, kaug, out_gain, l, w_ff2)
        x2, h_ffn = _out_project(x2, mod_l, l, o_mla, o_sb, o_fx, wout_b, rows(norm_ffn))
        x2 = _ffn(x2, h_ffn, mod_l, wff1_b, wff2_b, final_norm if l == N_LAYERS - 1 else None)
        mod_l = per_batch(mod_next) if mod_next is not None else None
    return x2.reshape(N_BATCH, SEQ_LEN, D_MODEL)
```

```python
import functools

import numpy as np
import jax
import jax.numpy as jnp
from jax import lax
from jax.experimental import pallas as pl
from jax.experimental.pallas import tpu as pltpu

F32 = jnp.float32
BF16 = jnp.bfloat16

D_MODEL = 2048
N_BATCH = 4
SEQ_LEN = 2048
N_LAYERS = 2
CHUNK_LEN = 64
CHUNK_SHIFT = CHUNK_LEN.bit_length() - 1
assert 1 << CHUNK_SHIFT == CHUNK_LEN
NORM_EPS = 1e-6
LOG2_E = 1.4426950408889634
MLA_H = 8
MLA_QL = 512
MLA_KVL = 256
MLA_NOPE_D = 128
MLA_ROPE_D = 64
MLA_V_D = 128
ROPE_THETA = 10000.0
SB_H = 8
SB_D = 64
FOX_H = 8
FOX_D = 64
FF_DIM = 4 * D_MODEL
N_TOK = N_BATCH * SEQ_LEN

LANES = 128
SUBLANES = 8
VMEM_LIMIT = 56 * 1024 * 1024

R_CQ = 0
R_CKV = R_CQ + MLA_QL
R_KR = R_CKV + MLA_KVL
R_ATT = R_KR + MLA_ROPE_D
ATT_W = 3 * SB_H * SB_D + 3 * FOX_H * FOX_D
R_F = R_ATT + ATT_W
IN_W = R_F + FOX_H
HEAD_GRP = SB_H * SB_D

PROJ_TM = 512
FFN_TM = 1024
FFN_TF = 512
ADA_TN = 1024
ATT_PAIRS = 4
ATT_T = 256
N_QB = SEQ_LEN // ATT_T
QB_ORDER = tuple(reversed(range(N_QB)))
CUM_T = 256
FGATE_SEQS = 2


def _dot(a, b):
    return jnp.dot(a, b, preferred_element_type=F32)


def _dot_nt(a, b):
    return lax.dot_general(a, b, (((1,), (1,)), ((), ())), preferred_element_type=F32)


def _split3(x):
    hi = x.astype(BF16)
    r1 = x - hi.astype(F32)
    mid = r1.astype(BF16)
    lo = (r1 - mid.astype(F32)).astype(BF16)
    return hi, mid, lo


def _softplus(z):
    return jnp.maximum(z, 0.0) + jnp.log(1.0 + jnp.exp(-jnp.abs(z)))


def _rms(x, gain):
    return x * lax.rsqrt(jnp.mean(x * x, axis=-1, keepdims=True) + NORM_EPS) * gain


def _inv_rms(x_ref):
    x = x_ref[...]
    return lax.rsqrt(jnp.mean(x * x, axis=-1, keepdims=True) + NORM_EPS)


def _modulated_norm(x_ref, norm_gain, mod_scale, mod_shift, h_ref):
    inv = _inv_rms(x_ref)
    gain = norm_gain * (1.0 + mod_scale)
    h_ref[...] = (x_ref[...] * inv * gain + mod_shift).astype(BF16)


def _resident(shape):
    nd = len(shape)
    return pl.BlockSpec(shape, lambda *_: (0,) * nd, pipeline_mode=pl.Buffered(1))


def _layer_resident(shape, layer):
    nd = len(shape)
    return pl.BlockSpec((1,) + shape, lambda *_: (layer,) + (0,) * nd, pipeline_mode=pl.Buffered(1))


def _params(sem):
    return pltpu.CompilerParams(dimension_semantics=sem, vmem_limit_bytes=VMEM_LIMIT)


def _ada_columns(c_ref, w_ref, b_ref, o_ref):
    c = c_ref[...]
    c_act = (c * jax.nn.sigmoid(c)).astype(BF16)
    o_ref[...] = _dot(c_act, w_ref[0].astype(BF16)) + b_ref[0]


def _ada_mod(c_pad, w_ada, b_ada3, layer):
    n_out = 6 * D_MODEL
    return pl.pallas_call(
        _ada_columns,
        grid=(n_out // ADA_TN,),
        in_specs=[
            pl.BlockSpec((SUBLANES, D_MODEL), lambda j: (0, 0)),
            pl.BlockSpec((1, D_MODEL, ADA_TN), lambda j: (layer, 0, j)),
            pl.BlockSpec((1, 1, ADA_TN), lambda j: (layer, 0, j)),
        ],
        out_specs=pl.BlockSpec((SUBLANES, ADA_TN), lambda j: (0, j)),
        out_shape=jax.ShapeDtypeStruct((SUBLANES, n_out), F32),
        compiler_params=_params(("arbitrary",)),
        name="ada_mod",
    )(c_pad, w_ada, b_ada3)


def _ada_along(layer):
    n_out = 6 * D_MODEL
    slab = n_out // (N_BATCH * ATT_PAIRS)
    assert slab * N_BATCH * ATT_PAIRS == n_out and slab % LANES == 0
    step = lambda b, p: b * ATT_PAIRS + p
    in_specs = [
        pl.BlockSpec((SUBLANES, D_MODEL), lambda b, p: (0, 0)),
        pl.BlockSpec((1, D_MODEL, slab), lambda b, p: (layer, 0, step(b, p))),
        pl.BlockSpec((1, 1, slab), lambda b, p: (layer, 0, step(b, p))),
    ]
    out_spec = pl.BlockSpec((SUBLANES, slab), lambda b, p: (0, step(b, p)))
    return in_specs, out_spec, jax.ShapeDtypeStruct((SUBLANES, n_out), F32)


def _proj_body(x_ref, mod_ref, gn_ref, wt_ref, wkr_ref, wf_ref, qn_ref, wuq_ref, kvn_ref, wukv_ref, bf_ref,
               cos_ref, sin_ref, q_out, kv_out, kr_out, att_out, lf_out, h_ref):
    mod = mod_ref[0]
    _modulated_norm(x_ref, gn_ref[0], mod[1:2], mod[0:1], h_ref)

    def proj(r0, width):
        return _dot_nt(h_ref[...], wt_ref[0, r0:r0 + width, :])

    mla_scale = (MLA_NOPE_D + MLA_ROPE_D) ** -0.5 * LOG2_E
    n_nope = MLA_H * MLA_NOPE_D
    n_rope = MLA_H * MLA_ROPE_D

    cq = _rms(proj(R_CQ, MLA_QL), qn_ref[0]).astype(BF16)
    q = _dot(cq, wuq_ref[0])
    q_out[:, :n_nope] = (q[:, :n_nope] * mla_scale).astype(BF16)
    q_rot = q[:, n_nope:n_nope + n_rope] * cos_ref[...] + q[:, n_nope + n_rope:] * sin_ref[...]
    q_out[:, n_nope:] = (q_rot * mla_scale).astype(BF16)

    ckv = _rms(proj(R_CKV, MLA_KVL), kvn_ref[0]).astype(BF16)
    kv_out[...] = _dot(ckv, wukv_ref[0]).astype(BF16)

    kr = _dot_nt(h_ref[...], wkr_ref[0])
    kr_out[...] = (kr[:, :LANES] * cos_ref[:, :LANES] + kr[:, LANES:] * sin_ref[:, :LANES]).astype(BF16)

    lf_out[...] = -_softplus(-(_dot_nt(h_ref[...], wf_ref[0]) + bf_ref[0]))

    sb_scale = SB_D ** -0.5 * LOG2_E
    fox_scale = FOX_D ** -0.5 * LOG2_E
    scales = (sb_scale, 1.0, 1.0, fox_scale, 1.0, 1.0)
    for g, sc in enumerate(scales):
        blk = proj(R_ATT + g * HEAD_GRP, HEAD_GRP)
        if sc != 1.0:
            blk = blk * sc
        att_out[:, g * HEAD_GRP:(g + 1) * HEAD_GRP] = blk.astype(BF16)


def _project(x2, mod_l, layer, gn, wt_in, wt_kr, wt_f, qn, wuq, kvn, wukv, bfp, cos_t, sin_t):
    tm = PROJ_TM
    tiles_per_seq = SEQ_LEN // tm
    q_w = MLA_H * (MLA_NOPE_D + MLA_ROPE_D)
    kv_w = MLA_H * (MLA_NOPE_D + MLA_V_D)
    rope_w = MLA_H * MLA_ROPE_D
    row = lambda i: (i, 0)
    return pl.pallas_call(
        _proj_body,
        grid=(N_TOK // tm,),
        in_specs=[
            pl.BlockSpec((tm, D_MODEL), row),
            pl.BlockSpec((1, 6, D_MODEL), lambda i: (i // tiles_per_seq, 0, 0)),
            _layer_resident((1, D_MODEL), layer),
            _layer_resident((IN_W, D_MODEL), layer),
            _layer_resident((2 * LANES, D_MODEL), layer),
            _layer_resident((LANES, D_MODEL), layer),
            _layer_resident((1, MLA_QL), layer),
            _layer_resident((MLA_QL, q_w + rope_w), layer),
            _layer_resident((1, MLA_KVL), layer),
            _layer_resident((MLA_KVL, kv_w), layer),
            _layer_resident((1, LANES), layer),
            pl.BlockSpec((tm, rope_w), lambda i: (i % tiles_per_seq, 0)),
            pl.BlockSpec((tm, rope_w), lambda i: (i % tiles_per_seq, 0)),
        ],
        out_specs=[
            pl.BlockSpec((tm, q_w), row),
            pl.BlockSpec((tm, kv_w), row),
            pl.BlockSpec((tm, LANES), row),
            pl.BlockSpec((tm, ATT_W), row),
            pl.BlockSpec((tm, LANES), row),
        ],
        out_shape=[
            jax.ShapeDtypeStruct((N_TOK, q_w), BF16),
            jax.ShapeDtypeStruct((N_TOK, kv_w), BF16),
            jax.ShapeDtypeStruct((N_TOK, LANES), BF16),
            jax.ShapeDtypeStruct((N_TOK, ATT_W), BF16),
            jax.ShapeDtypeStruct((N_TOK, LANES), F32),
        ],
        scratch_shapes=[pltpu.VMEM((tm, D_MODEL), BF16)],
        compiler_params=_params(("arbitrary",)),
        name="project",
    )(x2, mod_l, gn, wt_in, wt_kr, wt_f, qn, wuq, kvn, wukv, bfp, cos_t, sin_t)


def _fgate_body(lf_ref, tri_ref, pq_ref, pk_ref, cq_ref, ck_ref, qa_out, ka_out):
    n_blk = SEQ_LEN // CUM_T
    seqs = range(FGATE_SEQS)
    rows_of = lambda g, blk: slice(g * SEQ_LEN + blk * CUM_T, g * SEQ_LEN + (blk + 1) * CUM_T)
    local = {}
    for blk in range(n_blk):
        for g in seqs:
            hi, mid, lo = _split3(lf_ref[rows_of(g, blk), :] * LOG2_E)
            tri = tri_ref[...]
            local[g, blk] = _dot(tri, hi) + _dot(tri, mid) + _dot(tri, lo)
    carry = [jnp.zeros((1, LANES), F32) for _ in seqs]
    for blk in range(n_blk):
        for g in seqs:
            f_blk = local[g, blk] + carry[g]
            carry[g] = f_blk[CUM_T - 1:CUM_T, :]
            f_hi, f_mid, f_lo = _split3(f_blk)
            qa = _dot(f_hi, pq_ref[0]) + _dot(f_mid, pq_ref[1]) + _dot(f_lo, pq_ref[2]) + cq_ref[...]
            ka = ck_ref[...] - (_dot(f_hi, pk_ref[0]) + _dot(f_mid, pk_ref[1]) + _dot(f_lo, pk_ref[2]))
            qa_out[rows_of(g, blk), :] = qa.astype(BF16)
            ka_out[rows_of(g, blk), :] = ka.astype(BF16)


def _fgate_constants():
    tri = np.tril(np.ones((CUM_T, CUM_T), np.float32))
    n_pair = FOX_H // 2
    n_terms = 3
    assert 2 * n_terms <= SUBLANES
    pq = np.zeros((n_terms, LANES, n_pair * LANES), np.float32)
    pk = np.zeros((n_terms, LANES, n_pair * LANES), np.float32)
    cq = np.zeros((1, n_pair * LANES), np.float32)
    ck = np.zeros((1, n_pair * LANES), np.float32)
    for h in range(FOX_H):
        base = (h // 2) * LANES + (h % 2) * SUBLANES
        for t in range(n_terms):
            pq[t, h, base + t] = 1.0
            pk[t, h, base + n_terms + t] = 1.0
            cq[0, base + n_terms + t] = 1.0
            ck[0, base + t] = 1.0
    as_bf16 = lambda a: jnp.asarray(a, BF16)
    return as_bf16(tri), as_bf16(pq), as_bf16(pk), jnp.asarray(cq), jnp.asarray(ck)


def _fgate(logf):
    tri, pq, pk, cq, ck = _fgate_constants()
    aug_w = (FOX_H // 2) * LANES
    seq_blk = lambda b: (b, 0)
    return pl.pallas_call(
        _fgate_body,
        grid=(N_BATCH // FGATE_SEQS,),
        in_specs=[
            pl.BlockSpec((FGATE_SEQS * SEQ_LEN, LANES), seq_blk),
            _resident((CUM_T, CUM_T)),
            _resident((3, LANES, aug_w)),
            _resident((3, LANES, aug_w)),
            _resident((1, aug_w)),
            _resident((1, aug_w)),
        ],
        out_specs=[pl.BlockSpec((FGATE_SEQS * SEQ_LEN, aug_w), seq_blk),
                   pl.BlockSpec((FGATE_SEQS * SEQ_LEN, aug_w), seq_blk)],
        out_shape=[jax.ShapeDtypeStruct((N_TOK, aug_w), BF16), jax.ShapeDtypeStruct((N_TOK, aug_w), BF16)],
        compiler_params=_params(("arbitrary",)),
        name="fgate",
    )(logf, tri, pq, pk, cq, ck)


def _lane_band(shape, lo, width):
    lane = lax.broadcasted_iota(jnp.int32, shape, len(shape) - 1)
    return (lane >= lo) & (lane < lo + width)


def _rel_positions():
    rows = lax.broadcasted_iota(jnp.int32, (ATT_T, ATT_T), 0)
    cols = lax.broadcasted_iota(jnp.int32, (ATT_T, ATT_T), 1)
    return rows, cols


def _col_block(a, kb):
    return a[:, kb * ATT_T:(kb + 1) * ATT_T]


def _weights_times_v(p_ref, hh, nkb, v):
    p = jnp.concatenate([p_ref[hh, kb] for kb in range(nkb)], axis=1) if nkb > 1 else p_ref[hh, 0]
    return _dot(p, v)


def _softmax_attention(qcat_at, kcat_at, vaug_at, mask_fn, finish, s_ref, p_ref):
    def scores(i):
        nkb = i + 1
        row_max = []
        for hh in range(2):
            s = _dot_nt(qcat_at(i, hh), kcat_at(hh, nkb * ATT_T))
            blocks = [_col_block(s, kb) for kb in range(nkb)]
            blocks[-1] = jnp.where(mask_fn(*_rel_positions()), blocks[-1], -jnp.inf)
            m = None
            for kb in range(nkb):
                s_ref[i % 2, hh, kb] = blocks[kb]
                bm = jnp.maximum(blocks[kb][:, :LANES], blocks[kb][:, LANES:])
                m = bm if m is None else jnp.maximum(m, bm)
            row_max.append(jnp.max(m, axis=-1, keepdims=True))
        return row_max

    row_max = scores(QB_ORDER[0])
    for n, i in enumerate(QB_ORDER):
        nkb = i + 1
        next_max = scores(QB_ORDER[n + 1]) if n + 1 < N_QB else None
        for kb in range(nkb):
            for hh in range(2):
                p_ref[i % 2, hh, kb] = jnp.exp2(s_ref[i % 2, hh, kb] - row_max[hh]).astype(BF16)
        finish(i, [_weights_times_v(p_ref.at[i % 2], hh, nkb, vaug_at(hh, nkb * ATT_T)) for hh in range(2)])
        row_max = next_max


def _pair_norm(o, gain, head_d):
    first = _lane_band(o.shape, 0, head_d)
    sq = o * o
    ss_a = jnp.sum(jnp.where(first, sq, 0.0), axis=-1, keepdims=True)
    ss_b = jnp.sum(sq, axis=-1, keepdims=True) - ss_a
    inv = jnp.where(first, lax.rsqrt(ss_a / head_d + NORM_EPS), lax.rsqrt(ss_b / head_d + NORM_EPS))
    return o * inv * gain


_SCORE_SCRATCH = [
    pltpu.VMEM((2, 2, N_QB, ATT_T, ATT_T), F32),
    pltpu.VMEM((2, 2, N_QB, ATT_T, ATT_T), BF16),
]
def _cast_along(w, layer):
    _, n_rows, n_cols = w.shape
    slab = n_rows // (N_BATCH * ATT_PAIRS)
    assert slab * N_BATCH * ATT_PAIRS == n_rows and slab % (2 * SUBLANES) == 0
    in_spec = pl.BlockSpec((1, slab, n_cols), lambda b, p: (layer, b * ATT_PAIRS + p, 0))
    out_spec = pl.BlockSpec((slab, n_cols), lambda b, p: (b * ATT_PAIRS + p, 0))
    return in_spec, out_spec, jax.ShapeDtypeStruct((n_rows, n_cols), BF16)


def _mla_body(qn_ref, qr_ref, kn_ref, v_ref, kr_ref, g_ref, w_ref, o_ref, wb_ref, kcat_ref, vaug_ref,
              s_ref, p_ref):
    wb_ref[...] = w_ref[0].astype(BF16)
    d = MLA_NOPE_D
    for hh in range(2):
        kcat_ref[hh, :, :d] = kn_ref[:, hh * d:(hh + 1) * d]
        kcat_ref[hh, :, d:] = kr_ref[...]
        vaug_ref[hh, :, :d] = v_ref[:, hh * d:(hh + 1) * d]
        vaug_ref[hh, :, d:] = jnp.ones((SEQ_LEN, LANES), BF16)

    def chunk_mask(rows, cols):
        return cols < ((rows >> CHUNK_SHIFT) + 1) * CHUNK_LEN

    def qcat_at(i, hh):
        r = slice(i * ATT_T, (i + 1) * ATT_T)
        qr = qr_ref[r, :]
        qr = jnp.where(_lane_band(qr.shape, hh * MLA_ROPE_D, MLA_ROPE_D), qr, jnp.zeros_like(qr))
        return jnp.concatenate([qn_ref[r, hh * d:(hh + 1) * d], qr], axis=1)

    def finish(i, outs):
        r = slice(i * ATT_T, (i + 1) * ATT_T)
        for hh in range(2):
            o = outs[hh][:, :d] / outs[hh][:, d:]
            o_ref[r, hh * d:(hh + 1) * d] = _rms(o, g_ref[0, :, hh * d:(hh + 1) * d]).astype(BF16)

    _softmax_attention(
        qcat_at, lambda hh, klen: kcat_ref[hh, :klen, :], lambda hh, klen: vaug_ref[hh, :klen, :],
        chunk_mask, finish, s_ref, p_ref)


def _mla_attention(q_mla, kv_mla, krope, out_gain, layer, w_cast):
    n_pair = MLA_H // 2
    assert n_pair == ATT_PAIRS
    pair_w = 2 * MLA_NOPE_D
    w_in_spec, w_out_spec, w_out_shape = _cast_along(w_cast, layer)
    return pl.pallas_call(
        _mla_body,
        grid=(N_BATCH, n_pair),
        in_specs=[
            pl.BlockSpec((SEQ_LEN, pair_w), lambda b, p: (b, p)),
            pl.BlockSpec((SEQ_LEN, LANES), lambda b, p: (b, MLA_H * MLA_NOPE_D // LANES + p)),
            pl.BlockSpec((SEQ_LEN, pair_w), lambda b, p: (b, p)),
            pl.BlockSpec((SEQ_LEN, pair_w), lambda b, p: (b, n_pair + p)),
            pl.BlockSpec((SEQ_LEN, LANES), lambda b, p: (b, 0)),
            pl.BlockSpec((1, 1, pair_w), lambda b, p: (layer, 0, p)),
            w_in_spec,
        ],
        out_specs=[pl.BlockSpec((SEQ_LEN, pair_w), lambda b, p: (b, p)), w_out_spec],
        out_shape=[jax.ShapeDtypeStruct((N_TOK, MLA_H * MLA_V_D), BF16), w_out_shape],
        scratch_shapes=[
            pltpu.VMEM((2, SEQ_LEN, 2 * LANES), BF16),
            pltpu.VMEM((2, SEQ_LEN, 2 * LANES), BF16),
        ] + _SCORE_SCRATCH,
        compiler_params=_params(("arbitrary", "arbitrary")),
        name="mla_attention",
    )(q_mla, q_mla, kv_mla, kv_mla, krope, out_gain, w_cast)


def _fox_body(q_ref, k_ref, v_ref, qa_ref, ka_ref, g_ref, w_ref, o_ref, wb_ref, kcat_ref, vaug_ref,
              s_ref, p_ref):
    wb_ref[...] = w_ref[0].astype(BF16)
    kcat_ref[:, :LANES] = k_ref[...]
    kcat_ref[:, LANES:] = ka_ref[...]
    vaug_ref[:, :LANES] = v_ref[...]
    vaug_ref[:, LANES:] = jnp.ones((SEQ_LEN, LANES), BF16)

    def causal(rows, cols):
        return cols <= rows

    def qcat_at(i, hh):
        r = slice(i * ATT_T, (i + 1) * ATT_T)
        q = q_ref[r, :]
        qa = qa_ref[r, :]
        zero = jnp.zeros_like(q)
        return jnp.concatenate([
            jnp.where(_lane_band(q.shape, hh * FOX_D, FOX_D), q, zero),
            jnp.where(_lane_band(qa.shape, hh * SUBLANES, SUBLANES), qa, zero)], axis=1)

    def finish(i, outs):
        heads = [outs[hh][:, :LANES] / outs[hh][:, LANES:] for hh in range(2)]
        o = jnp.where(_lane_band(heads[0].shape, 0, FOX_D), heads[0], heads[1])
        o_ref[i * ATT_T:(i + 1) * ATT_T, :] = _pair_norm(o, g_ref[0], FOX_D).astype(BF16)

    _softmax_attention(
        qcat_at, lambda hh, klen: kcat_ref[:klen, :], lambda hh, klen: vaug_ref[:klen, :],
        causal, finish, s_ref, p_ref)


def _fox_attention(p_att, qaug, kaug, out_gain, layer, w_cast):
    n_pair = FOX_H // 2
    assert n_pair == ATT_PAIRS
    grp = HEAD_GRP // LANES
    gain_blk = (MLA_H * MLA_V_D + SB_H * SB_D) // LANES
    w_in_spec, w_out_spec, w_out_shape = _cast_along(w_cast, layer)
    return pl.pallas_call(
        _fox_body,
        grid=(N_BATCH, n_pair),
        in_specs=[
            pl.BlockSpec((SEQ_LEN, LANES), lambda b, p: (b, 3 * grp + p)),
            pl.BlockSpec((SEQ_LEN, LANES), lambda b, p: (b, 4 * grp + p)),
            pl.BlockSpec((SEQ_LEN, LANES), lambda b, p: (b, 5 * grp + p)),
            pl.BlockSpec((SEQ_LEN, LANES), lambda b, p: (b, p)),
            pl.BlockSpec((SEQ_LEN, LANES), lambda b, p: (b, p)),
            pl.BlockSpec((1, 1, LANES), lambda b, p: (layer, 0, gain_blk + p)),
            w_in_spec,
        ],
        out_specs=[pl.BlockSpec((SEQ_LEN, LANES), lambda b, p: (b, p)), w_out_spec],
        out_shape=[jax.ShapeDtypeStruct((N_TOK, FOX_H * FOX_D), BF16), w_out_shape],
        scratch_shapes=[
            pltpu.VMEM((SEQ_LEN, 2 * LANES), BF16),
            pltpu.VMEM((SEQ_LEN, 2 * LANES), BF16),
        ] + _SCORE_SCRATCH,
        compiler_params=_params(("arbitrary", "arbitrary")),
        name="fox_attention",
    )(p_att, p_att, p_att, qaug, kaug, out_gain, w_cast)


def _sb_body(*refs, with_ada):
    if with_ada:
        (q_ref, k_ref, v_ref, tri_ref, g_ref, w_ref, c_ref, wa_ref, ba_ref,
         o_ref, wb_ref, mod_ref, u_ref, p_ref) = refs
        _ada_columns(c_ref, wa_ref, ba_ref, mod_ref)
    else:
        q_ref, k_ref, v_ref, tri_ref, g_ref, w_ref, o_ref, wb_ref, u_ref, p_ref = refs
    wb_ref[...] = w_ref[0].astype(BF16)

    def score_tiles(i):
        q = q_ref[i * ATT_T:(i + 1) * ATT_T, :]
        zero = jnp.zeros_like(q)
        q_heads = [jnp.where(_lane_band(q.shape, hh * SB_D, SB_D), q, zero) for hh in range(2)]

        def tile(hh, kb):
            u_ref[i % 2, hh, kb] = _dot_nt(q_heads[hh], k_ref[kb * ATT_T:(kb + 1) * ATT_T, :])

        return [functools.partial(tile, hh, kb) for kb in range(i + 1) for hh in range(2)]

    def weights(i, pending):
        nkb = i + 1
        right = [None, None]
        for kb in reversed(range(nkb)):
            on_diagonal = kb == nkb - 1
            for hh in range(2):
                if pending:
                    pending.pop()()
                u = u_ref[i % 2, hh, kb]
                drop = jnp.maximum(u, 0.0) + jnp.log2(1.0 + jnp.exp2(-jnp.abs(u)))
                if on_diagonal:
                    rows, cols = _rel_positions()
                    strict = cols < rows
                    drop = jnp.where(strict, drop, 0.0)
                log_w = (u - drop) + _dot(drop.astype(BF16), tri_ref[...])
                if right[hh] is not None:
                    log_w = log_w - right[hh]
                w = jnp.exp2(log_w)
                if on_diagonal:
                    w = jnp.where(strict, w, 0.0)
                p_ref[i % 2, hh, kb] = w.astype(BF16)
                total = jnp.sum(drop, axis=-1, keepdims=True)
                right[hh] = total if right[hh] is None else right[hh] + total

    for tile in score_tiles(QB_ORDER[0]):
        tile()
    for n, i in enumerate(QB_ORDER):
        nkb = i + 1
        pending = score_tiles(QB_ORDER[n + 1]) if n + 1 < N_QB else []
        weights(i, pending)
        while pending:
            pending.pop()()
        heads = [_weights_times_v(p_ref.at[i % 2], hh, nkb, v_ref[:nkb * ATT_T, :]) for hh in range(2)]
        o = jnp.where(_lane_band(heads[0].shape, 0, SB_D), heads[0], heads[1])
        o_ref[i * ATT_T:(i + 1) * ATT_T, :] = _pair_norm(o, g_ref[0], SB_D).astype(BF16)


def _sb_attention(p_att, out_gain, layer, w_cast, ada_next=None):
    n_pair = SB_H // 2
    assert n_pair == ATT_PAIRS
    grp = HEAD_GRP // LANES
    gain_blk = MLA_H * MLA_V_D // LANES
    w_in_spec, w_out_spec, w_out_shape = _cast_along(w_cast, layer)
    tri = jnp.asarray(-np.tril(np.ones((ATT_T, ATT_T), np.float32), -1), BF16)
    in_specs = [
        pl.BlockSpec((SEQ_LEN, LANES), lambda b, p: (b, p)),
        pl.BlockSpec((SEQ_LEN, LANES), lambda b, p: (b, grp + p)),
        pl.BlockSpec((SEQ_LEN, LANES), lambda b, p: (b, 2 * grp + p)),
        _resident((ATT_T, ATT_T)),
        pl.BlockSpec((1, 1, LANES), lambda b, p: (layer, 0, gain_blk + p)),
        w_in_spec,
    ]
    out_specs = [pl.BlockSpec((SEQ_LEN, LANES), lambda b, p: (b, p)), w_out_spec]
    out_shape = [jax.ShapeDtypeStruct((N_TOK, SB_H * SB_D), BF16), w_out_shape]
    operands = [p_att, p_att, p_att, tri, out_gain, w_cast]
    if ada_next is not None:
        ada_in, ada_out_spec, ada_out_shape = _ada_along(layer + 1)
        in_specs += ada_in
        out_specs.append(ada_out_spec)
        out_shape.append(ada_out_shape)
        operands += list(ada_next)
    return pl.pallas_call(
        functools.partial(_sb_body, with_ada=ada_next is not None),
        grid=(N_BATCH, n_pair),
        in_specs=in_specs,
        out_specs=out_specs,
        out_shape=out_shape,
        scratch_shapes=_SCORE_SCRATCH,
        compiler_params=_params(("arbitrary", "arbitrary")),
        name="sb_attention",
    )(*operands)


def _outproj_body(x_ref, mod_ref, om_ref, os_ref, of_ref, w_ref, gn_ref, o_ref, h_ref):
    n_mla = MLA_H * MLA_V_D
    n_sb = SB_H * SB_D
    mod = mod_ref[0]
    gain = gn_ref[0] * (1.0 + mod[4:5])
    half = x_ref.shape[0] // 2
    for part in range(2):
        rows = slice(part * half, (part + 1) * half)
        y = (_dot(om_ref[rows, :], w_ref[:n_mla, :]) + _dot(os_ref[rows, :], w_ref[n_mla:n_mla + n_sb, :])
             + _dot(of_ref[rows, :], w_ref[n_mla + n_sb:, :]))
        y = x_ref[rows, :] + mod[2:3] * y
        o_ref[rows, :] = y
        h_ref[rows, :] = (_rms(y, gain) + mod[3:4]).astype(BF16)


def _out_project(x2, mod_l, layer, o_mla, o_sb, o_fx, w_out, gn_ffn):
    tm = PROJ_TM
    tiles_per_seq = SEQ_LEN // tm
    row = lambda i: (i, 0)
    return pl.pallas_call(
        _outproj_body,
        grid=(N_TOK // tm,),
        in_specs=[
            pl.BlockSpec((tm, D_MODEL), row),
            pl.BlockSpec((1, 6, D_MODEL), lambda i: (i // tiles_per_seq, 0, 0)),
            pl.BlockSpec((tm, MLA_H * MLA_V_D), row),
            pl.BlockSpec((tm, SB_H * SB_D), row),
            pl.BlockSpec((tm, FOX_H * FOX_D), row),
            _resident((D_MODEL, D_MODEL)),
            _layer_resident((1, D_MODEL), layer),
        ],
        out_specs=[pl.BlockSpec((tm, D_MODEL), row), pl.BlockSpec((tm, D_MODEL), row)],
        out_shape=[jax.ShapeDtypeStruct((N_TOK, D_MODEL), F32), jax.ShapeDtypeStruct((N_TOK, D_MODEL), BF16)],
        compiler_params=_params(("arbitrary",)),
        name="out_project",
    )(x2, mod_l, o_mla, o_sb, o_fx, w_out, gn_ffn)


def _ffn_body(x_hbm, h_hbm, mod_ref, w1_ref, w2_ref, *rest, final):
    gf_ref, o_ref, x_buf, h_buf, sem = rest if final else (None,) + rest
    i, j = pl.program_id(0), pl.program_id(1)
    tm = o_ref.shape[0]
    slot = i % 2
    mod = mod_ref[0]

    def tile_copies(tile, to_slot):
        rows = pl.ds(pl.multiple_of(tile * tm, tm), tm)
        return (pltpu.make_async_copy(x_hbm.at[rows, :], x_buf.at[to_slot], sem.at[0, to_slot]),
                pltpu.make_async_copy(h_hbm.at[rows, :], h_buf.at[to_slot], sem.at[1, to_slot]))

    @pl.when(j == 0)
    def _():
        @pl.when(i == 0)
        def _():
            for copy in tile_copies(0, 0):
                copy.start()

        @pl.when(i + 1 < pl.num_programs(0))
        def _():
            for copy in tile_copies(i + 1, 1 - slot):
                copy.start()

        for copy in tile_copies(i, slot):
            copy.wait()
        o_ref[...] = x_buf[slot]

    a = jnp.maximum(_dot(h_buf[slot], w1_ref[...]), 0.0)
    o_ref[...] += mod[5:6] * _dot((a * a).astype(BF16), w2_ref[...])

    if final:
        @pl.when(j == pl.num_programs(1) - 1)
        def _():
            inv = _inv_rms(o_ref)
            o_ref[...] = o_ref[...] * inv * gf_ref[...]


def _ffn(x2, h2, mod_l, w1, w2, g_final=None):
    tm, tf = FFN_TM, FFN_TF
    tiles_per_seq = SEQ_LEN // tm
    final = g_final is not None
    in_specs = [
        pl.BlockSpec(memory_space=pl.ANY),
        pl.BlockSpec(memory_space=pl.ANY),
        pl.BlockSpec((1, 6, D_MODEL), lambda i, j: (i // tiles_per_seq, 0, 0)),
        pl.BlockSpec((D_MODEL, tf), lambda i, j: (0, j)),
        pl.BlockSpec((tf, D_MODEL), lambda i, j: (j, 0)),
    ]
    operands = [x2, h2, mod_l, w1, w2]
    if final:
        in_specs.append(pl.BlockSpec((1, D_MODEL), lambda i, j: (0, 0)))
        operands.append(g_final.reshape(1, D_MODEL))
    return pl.pallas_call(
        functools.partial(_ffn_body, final=final),
        grid=(N_TOK // tm, FF_DIM // tf),
        in_specs=in_specs,
        out_specs=pl.BlockSpec((tm, D_MODEL), lambda i, j: (i, 0)),
        out_shape=jax.ShapeDtypeStruct((N_TOK, D_MODEL), F32),
        scratch_shapes=[
            pltpu.VMEM((2, tm, D_MODEL), F32),
            pltpu.VMEM((2, tm, D_MODEL), BF16),
            pltpu.SemaphoreType.DMA((2, 2)),
        ],
        compiler_params=_params(("arbitrary", "arbitrary")),
        name="ffn",
    )(*operands)


def _swap_halves(w):
    half = w.shape[-1] // 2
    return jnp.concatenate([w[..., half:], w[..., :half]], axis=-1)


def _pack_w_in(w):
    wt = jnp.swapaxes(w, 1, 2).astype(BF16)
    kr = wt[:, R_KR:R_ATT]
    kr_sw = jnp.concatenate([kr[:, MLA_ROPE_D // 2:], kr[:, :MLA_ROPE_D // 2]], axis=1)
    wt_kr = jnp.concatenate([kr, kr, kr_sw, kr_sw], axis=1)
    wt_f = jnp.pad(wt[:, R_F:], ((0, 0), (0, LANES - FOX_H), (0, 0)))
    return wt, wt_kr, wt_f


def _pack_w_uq(w):
    w4 = w.astype(BF16).reshape(N_LAYERS, MLA_QL, MLA_H, MLA_NOPE_D + MLA_ROPE_D)
    flat = lambda a: a.reshape(N_LAYERS, MLA_QL, -1)
    rope = w4[..., MLA_NOPE_D:]
    return jnp.concatenate([flat(w4[..., :MLA_NOPE_D]), flat(rope), flat(_swap_halves(rope))], axis=-1)


def _pack_w_ukv(w):
    w4 = w.astype(BF16).reshape(N_LAYERS, MLA_KVL, MLA_H, MLA_NOPE_D + MLA_V_D)
    flat = lambda a: a.reshape(N_LAYERS, MLA_KVL, -1)
    return jnp.concatenate([flat(w4[..., :MLA_NOPE_D]), flat(w4[..., MLA_NOPE_D:])], axis=-1)


def _rope_tables():
    pos = jnp.arange(SEQ_LEN, dtype=F32)
    inv = ROPE_THETA ** (-jnp.arange(0, MLA_ROPE_D, 2, dtype=F32) / MLA_ROPE_D)
    ang = pos[:, None] * inv[None, :]
    cos, sin = jnp.cos(ang), jnp.sin(ang)
    cos_t = jnp.tile(jnp.concatenate([cos, cos], axis=1), (1, MLA_H))
    sin_t = jnp.tile(jnp.concatenate([-sin, sin], axis=1), (1, MLA_H))
    return cos_t, sin_t


def kernel(x, c, w_ada, b_ada, norm_mix, w_in, q_norm, w_uq, kv_norm, w_ukv, b_forget, out_norm,
           w_out, norm_ffn, w_ff1, w_ff2, final_norm):
    assert x.shape == (N_BATCH, SEQ_LEN, D_MODEL) and c.shape == (N_BATCH, D_MODEL)
    cos_t, sin_t = _rope_tables()
    c_pad = jnp.pad(c, ((0, SUBLANES - N_BATCH), (0, 0)))
    b_ada3 = b_ada.reshape(N_LAYERS, 1, 6 * D_MODEL)
    per_batch = lambda m: m[:N_BATCH].reshape(N_BATCH, 6, D_MODEL)
    mod_l = per_batch(_ada_mod(c_pad, w_ada, b_ada3, 0))

    rows = lambda v: v.reshape(N_LAYERS, 1, -1)
    assert w_in.shape == (N_LAYERS, D_MODEL, IN_W)
    wt_in, wt_kr, wt_f = _pack_w_in(w_in)
    wuq_p, wukv_p = _pack_w_uq(w_uq), _pack_w_ukv(w_ukv)
    bf_pad = jnp.pad(b_forget, ((0, 0), (0, LANES - FOX_H))).reshape(N_LAYERS, 1, LANES)
    out_gain = rows(out_norm)

    x2 = x.reshape(N_TOK, D_MODEL)
    for l in range(N_LAYERS):
        q_mla, kv_mla, krope, p_att, logf = _project(
            x2, mod_l, l, rows(norm_mix), wt_in, wt_kr, wt_f, rows(q_norm), wuq_p, rows(kv_norm), wukv_p, bf_pad,
            cos_t, sin_t)
        qaug, kaug = _fgate(logf)
        o_mla, wff1_b = _mla_attention(q_mla, kv_mla, krope, out_gain, l, w_ff1)
        if l + 1 < N_LAYERS:
            o_sb, wout_b, mod_next = _sb_attention(p_att, out_gain, l, w_out, (c_pad, w_ada, b_ada3))
        else:
            (o_sb, wout_b), mod_next = _sb_attention(p_att, out_gain, l, w_out), None
        o_fx, wff2_b = _fox_attention(p_att, qaug, kaug, out_gain, l, w_ff2)
        x2, h_ffn = _out_project(x2, mod_l, l, o_mla, o_sb, o_fx, wout_b, rows(norm_ffn))
        x2 = _ffn(x2, h_ffn, mod_l, wff1_b, wff2_b, final_norm if l == N_LAYERS - 1 else None)
        mod_l = per_batch(mod_next) if mod_next is not None else None
    return x2.reshape(N_BATCH, SEQ_LEN, D_MODEL)
```
